```python
import math, functools
import jax, jax.numpy as jnp
from jax import lax
import numpy as np

D_MODEL = 1024
BATCH = 32
SEQ = 256
DEPTH = 2
DEC_BATCH = 8
DEC_SEQ = 4096
PAST_LEN = 256

GRID_W = 64
HEAD_DIM = 64
BLOCK = 128
ROPE_BASE = 10000.0
N_AB = (DEPTH + 1) // 2
N_CD = DEPTH // 2
A_HEADS = (D_MODEL // 2) // HEAD_DIM
A_KV_HEADS = A_HEADS // 4
A_REP = A_HEADS // A_KV_HEADS
WINDOW = 128
B_HEADS = (D_MODEL // 2) // HEAD_DIM
B_DK = HEAD_DIM
B_DV = HEAD_DIM
RET_CHUNK = 128
C_WIDTH = D_MODEL // 2
C_GROUP = 16
C_GROUPS = C_WIDTH // C_GROUP
C_STATE = 64
D_QK = HEAD_DIM
D_V = 2 * HEAD_DIM
D_HEADS = (D_MODEL // 2) // D_V
FFN_HIDDEN = -(-8 * D_MODEL // (3 * 256)) * 256
AB_SPLITS = (A_HEADS * HEAD_DIM, A_KV_HEADS * HEAD_DIM, A_KV_HEADS * HEAD_DIM,
             B_HEADS * B_DK, B_HEADS * B_DK, B_HEADS * B_DV, B_HEADS * B_DV)
AB_COLS = sum(AB_SPLITS)
AB_OUT = A_HEADS * HEAD_DIM + B_HEADS * B_DV
CD_SPLITS = (C_WIDTH, D_HEADS * 2 * D_QK, D_HEADS * 2 * D_QK, D_HEADS * D_V)
CD_COLS = sum(CD_SPLITS)
CD_OUT = C_WIDTH + D_HEADS * D_V
NEG_INF = -1e30

kernel_name = 'hybrid_diffusion_prefix_trunk_step'


def rms_norm(x, g, eps=1e-6):
    xf = x.astype(jnp.float32)
    y = xf * lax.rsqrt(jnp.mean(xf * xf, axis=-1, keepdims=True) + eps)
    return (y * g.astype(jnp.float32)).astype(x.dtype)


def split_cols(p, sizes):
    return jnp.split(p, np.cumsum(sizes)[:-1].tolist(), axis=-1)


def ada_modulation(cond, w, b):
    m = jax.nn.silu(cond) @ w + b
    return jnp.split(m[..., None, :], 6, axis=-1)


def swiglu(h, w1, w3, w2):
    return (jax.nn.silu(h @ w1) * (h @ w3)) @ w2


def axial_rope(n_tok, dim):
    rows = n_tok // GRID_W
    r = jnp.repeat(jnp.arange(rows, dtype=jnp.float32), GRID_W)
    col = jnp.tile(jnp.arange(GRID_W, dtype=jnp.float32), rows)
    n_freq = dim // 4
    inv = ROPE_BASE ** (-jnp.arange(n_freq, dtype=jnp.float32) / n_freq)
    ang = jnp.concatenate([r[:, None] * inv, col[:, None] * inv], axis=-1)
    return jnp.cos(ang), jnp.sin(ang)


def apply_rope(x, rope):
    cos, sin = rope
    half = x.shape[-1] // 2
    shp = (cos.shape[0],) + (1,) * (x.ndim - 3) + (half,)
    cos = cos.reshape(shp)
    sin = sin.reshape(shp)
    xf = x.astype(jnp.float32)
    x1, x2 = xf[..., :half], xf[..., half:]
    return jnp.concatenate([x1 * cos - x2 * sin, x2 * cos + x1 * sin], axis=-1).astype(x.dtype)


def over_query_blocks(fn, q):
    nb = q.shape[1] // BLOCK
    def one(bi):
        return fn(bi, lax.dynamic_slice_in_dim(q, bi * BLOCK, BLOCK, axis=1))
    out = lax.map(one, jnp.arange(nb))
    out = jnp.moveaxis(out, 0, 1)
    return out.reshape(out.shape[:1] + (nb * BLOCK,) + out.shape[3:])


def sink_attend(q, k, v, mask, sink):
    s = jnp.einsum('bqgrd,bkgd->bgrqk', q, k, preferred_element_type=jnp.float32) * (HEAD_DIM ** -0.5)
    if mask is not None:
        s = jnp.where(mask, s, NEG_INF)
    sk = sink.astype(jnp.float32)[None, :, :, None, None]
    m = jnp.maximum(jnp.max(s, axis=-1, keepdims=True), sk)
    p = jnp.exp(s - m)
    p = p / (jnp.sum(p, axis=-1, keepdims=True) + jnp.exp(sk - m))
    return jnp.einsum('bgrqk,bkgd->bqgrd', p.astype(v.dtype), v)


def window_attn_latent(q, k, v, ck, cv, sink):
    T = q.shape[1]
    Tc = ck.shape[1]
    span = BLOCK + 2 * WINDOW
    pad = ((0, 0), (WINDOW, WINDOW), (0, 0), (0, 0))
    kp = jnp.pad(k, pad)
    vp = jnp.pad(v, pad)
    def block(bi, qb):
        q0 = bi * BLOCK
        kb = lax.dynamic_slice_in_dim(kp, q0, span, axis=1)
        vb = lax.dynamic_slice_in_dim(vp, q0, span, axis=1)
        qpos = q0 + jnp.arange(BLOCK)
        kpos = q0 - WINDOW + jnp.arange(span)
        band = ((jnp.abs(qpos[:, None] - kpos[None, :]) <= WINDOW)
                & (kpos >= 0)[None, :] & (kpos < T)[None, :])
        mask = jnp.concatenate([jnp.ones((BLOCK, Tc), dtype=bool), band], axis=1)
        return sink_attend(qb, jnp.concatenate([ck, kb], axis=1),
                           jnp.concatenate([cv, vb], axis=1), mask, sink)
    return over_query_blocks(block, q)


def diff_attend(q, k, v, lam):
    s = jnp.einsum('bqhcd,bkhcd->bhcqk', q, k, preferred_element_type=jnp.float32) * (D_QK ** -0.5)
    p = jax.nn.softmax(s, axis=-1)
    w = p[:, :, 0] - lam * p[:, :, 1]
    return jnp.einsum('bhqk,bkhe->bqhe', w.astype(v.dtype), v)


def retention_scan(q, k, v, log_g, s0):
    B, T, H, dk = q.shape
    dv = v.shape[-1]
    L = RET_CHUNK
    n = T // L
    lg = log_g.astype(jnp.float32)
    idx = jnp.arange(L, dtype=jnp.float32)
    diff = idx[:, None] - idx[None, :]
    inner_decay = jnp.where(diff >= 0, jnp.exp(lg[:, None, None] * jnp.maximum(diff, 0.0)), 0.0)
    q_decay = jnp.exp(lg[None, :] * (idx[:, None] + 1.0))
    k_decay = jnp.exp(lg[None, :] * (L - 1.0 - idx[:, None]))
    chunk_decay = jnp.exp(lg * L)
    qc = q.reshape(B, n, L, H, dk)
    kc = k.reshape(B, n, L, H, dk)
    vc = v.reshape(B, n, L, H, dv).astype(jnp.float32)
    s = jnp.einsum('bnihd,bnjhd->bnhij', qc, kc, preferred_element_type=jnp.float32) * inner_decay
    inner = jnp.einsum('bnhij,bnjhe->bnihe', s, vc)
    kv = jnp.einsum('bnjhd,bnjhe->nbhde', kc * k_decay[:, :, None], vc)
    def step(state, kv_c):
        return chunk_decay[None, :, None, None] * state + kv_c, state
    s_last, s_prev = lax.scan(step, s0.astype(jnp.float32), kv)
    cross = jnp.einsum('bnihd,nbhde->bnihe', qc * q_decay[:, :, None], s_prev)
    return (inner + cross).reshape(B, T, H, dv), s_last


def retention_bidir(q, k, v, log_g, s0):
    of, sf = retention_scan(q, k, v, log_g[0], s0[:, 0])
    ob, sb = retention_scan(jnp.flip(q, 1), jnp.flip(k, 1), jnp.flip(v, 1), log_g[1], s0[:, 1])
    return of + jnp.flip(ob, 1), jnp.stack([sf, sb], axis=1)


def s5_zoh(lam_re, lam_im, log_dt):
    lr = lam_re.astype(jnp.float32)
    li = lam_im.astype(jnp.float32)
    dt = jnp.exp(log_dt.astype(jnp.float32))[:, None]
    mag = jnp.exp(lr * dt)
    a_re = mag * jnp.cos(li * dt)
    a_im = mag * jnp.sin(li * dt)
    den = lr * lr + li * li
    f_re = ((a_re - 1.0) * lr + a_im * li) / den
    f_im = (a_im * lr - (a_re - 1.0) * li) / den
    return a_re, a_im, f_re, f_im


def s5_combine(e1, e2):
    a1r, a1i, b1r, b1i = e1
    a2r, a2i, b2r, b2i = e2
    return (a1r * a2r - a1i * a2i, a1r * a2i + a1i * a2r,
            a2r * b1r - a2i * b1i + b2r, a2r * b1i + a2i * b1r + b2i)


def s5_bidir(u, lam_re, lam_im, log_dt, b_re, b_im, c_re, c_im, d_skip, h0_re, h0_im):
    B, T, _ = u.shape
    ug = u.reshape(B, T, C_GROUPS, C_GROUP).astype(jnp.float32)
    bu_re = jnp.einsum('btgc,gpc->btgp', ug, b_re.astype(jnp.float32))
    bu_im = jnp.einsum('btgc,gpc->btgp', ug, b_im.astype(jnp.float32))
    hs_re, hs_im, fin_re, fin_im = [], [], [], []
    for dr, reverse in ((0, False), (1, True)):
        a_re, a_im, f_re, f_im = s5_zoh(lam_re[dr], lam_im[dr], log_dt[dr])
        x_re = f_re * bu_re - f_im * bu_im
        x_im = f_re * bu_im + f_im * bu_re
        i_re = h0_re[:, dr].astype(jnp.float32)
        i_im = h0_im[:, dr].astype(jnp.float32)
        first = T - 1 if reverse else 0
        x_re = x_re.at[:, first].add(a_re * i_re - a_im * i_im)
        x_im = x_im.at[:, first].add(a_re * i_im + a_im * i_re)
        ar = jnp.broadcast_to(a_re, (1, T) + a_re.shape)
        ai = jnp.broadcast_to(a_im, (1, T) + a_im.shape)
        _, _, h_re, h_im = lax.associative_scan(s5_combine, (ar, ai, x_re, x_im), reverse=reverse, axis=1)
        last = 0 if reverse else T - 1
        fin_re.append(h_re[:, last])
        fin_im.append(h_im[:, last])
        hs_re.append(h_re)
        hs_im.append(h_im)
    h_re = hs_re[0] + hs_re[1]
    h_im = hs_im[0] + hs_im[1]
    y = (jnp.einsum('btgp,gcp->btgc', h_re, c_re.astype(jnp.float32))
         - jnp.einsum('btgp,gcp->btgc', h_im, c_im.astype(jnp.float32)))
    y = y.reshape(B, T, C_WIDTH) + d_skip.astype(jnp.float32) * u.astype(jnp.float32)
    return y.astype(u.dtype), jnp.stack(fin_re, axis=1), jnp.stack(fin_im, axis=1)


def ab_mixer(h, rope, ctx, w_in, w_out, q_g, k_g, sink, ret_decay, ret_g):
    B, T, _ = h.shape
    qa, ka, va, qb, kb, vb, gb = split_cols(h @ w_in, AB_SPLITS)
    qa = rms_norm(qa.reshape(B, T, A_KV_HEADS, A_REP, HEAD_DIM), q_g)
    ka = rms_norm(ka.reshape(B, T, A_KV_HEADS, HEAD_DIM), k_g)
    va = va.reshape(B, T, A_KV_HEADS, HEAD_DIM)
    qb = qb.reshape(B, T, B_HEADS, B_DK) * (B_DK ** -0.5)
    kb = kb.reshape(B, T, B_HEADS, B_DK)
    vb = vb.reshape(B, T, B_HEADS, B_DV)
    sink_r = sink.reshape(A_KV_HEADS, A_REP)
    log_g = jax.nn.log_sigmoid(ret_decay.astype(jnp.float32))
    if ctx is None:
        oa = over_query_blocks(lambda bi, qblk: sink_attend(qblk, ka, va, None, sink_r), qa)
        s0 = jnp.zeros((B, 2, B_HEADS, B_DK, B_DV), jnp.float32)
        ob, s_ret = retention_bidir(qb, kb, vb, log_g, s0)
        ctx_out = (ka, va, s_ret)
    else:
        ck, cv, s0 = ctx
        oa = window_attn_latent(apply_rope(qa, rope), apply_rope(ka, rope), va, ck, cv, sink_r)
        ob, _ = retention_bidir(qb, kb, vb, log_g, s0)
        ctx_out = None
    ob = rms_norm(ob, ret_g).astype(h.dtype) * jax.nn.silu(gb.reshape(B, T, B_HEADS, B_DV))
    out = jnp.concatenate([oa.reshape(B, T, -1), ob.reshape(B, T, -1)], axis=-1) @ w_out
    return out, ctx_out


def cd_mixer(h, rope, ctx, w_in, w_out, lam_re, lam_im, log_dt, b_re, b_im, c_re, c_im, d_skip,
             glu_w, glu_b, q_g, k_g, lam_p, subln_g, lam_init):
    B, T, _ = h.shape
    u, qd, kd, vd = split_cols(h @ w_in, CD_SPLITS)
    qd = rms_norm(qd.reshape(B, T, D_HEADS, 2, D_QK), q_g)
    kd = rms_norm(kd.reshape(B, T, D_HEADS, 2, D_QK), k_g)
    vd = vd.reshape(B, T, D_HEADS, D_V)
    lp = lam_p.astype(jnp.float32)
    lam = jnp.exp(jnp.sum(lp[0] * lp[1])) - jnp.exp(jnp.sum(lp[2] * lp[3])) + lam_init
    if ctx is None:
        h0 = jnp.zeros((B, 2, C_GROUPS, C_STATE), jnp.float32)
        y, fin_re, fin_im = s5_bidir(u, lam_re, lam_im, log_dt, b_re, b_im, c_re, c_im, d_skip, h0, h0)
        od = over_query_blocks(lambda bi, qblk: diff_attend(qblk, kd, vd, lam), qd)
        ctx_out = (fin_re, fin_im, kd, vd)
    else:
        h0_re, h0_im, ck, cv = ctx
        y, _, _ = s5_bidir(u, lam_re, lam_im, log_dt, b_re, b_im, c_re, c_im, d_skip, h0_re, h0_im)
        k_all = jnp.concatenate([ck, apply_rope(kd, rope)], axis=1)
        v_all = jnp.concatenate([cv, vd], axis=1)
        od = over_query_blocks(lambda bi, qblk: diff_attend(qblk, k_all, v_all, lam), apply_rope(qd, rope))
        ctx_out = None
    g = jax.nn.gelu(y)
    oc = g * jax.nn.sigmoid(g @ glu_w + glu_b)
    od = rms_norm(od, subln_g) * (1.0 - lam_init)
    out = jnp.concatenate([oc, od.reshape(B, T, -1)], axis=-1) @ w_out
    return out, ctx_out


def trunk_layer(x, cond, mixer, ada_w, ada_b, n1_g, n2_g, w1, w3, w2):
    sh1, sc1, g1, sh2, sc2, g2 = ada_modulation(cond, ada_w, ada_b)
    h = rms_norm(x, n1_g) * (1.0 + sc1) + sh1
    out, ctx_out = mixer(h)
    x = x + g1 * out
    h = rms_norm(x, n2_g) * (1.0 + sc2) + sh2
    x = x + g2 * swiglu(h, w1, w3, w2)
    return x, ctx_out


def setup_inputs(seed: int = 0) -> dict:
    key = jax.random.key(seed)
    keys = iter(jax.random.split(key, 64))
    f32 = jnp.float32
    def nrm(shape, scale=1.0):
        return scale * jax.random.normal(next(keys), shape, f32)
    ret_e = 5.0 + jnp.arange(B_HEADS, dtype=f32)
    ret_logit = jnp.log1p(-(2.0 ** -ret_e)) + ret_e * math.log(2.0)
    n_idx = jnp.arange(C_STATE, dtype=f32)
    return {
        'x_prompt': nrm((BATCH, SEQ, D_MODEL)),
        'x_sample': nrm((DEC_BATCH, DEC_SEQ, D_MODEL)),
        'c': nrm((DEC_BATCH, D_MODEL)),
        'c_ctx': nrm((D_MODEL,)),
        'cache_k_a': nrm((DEC_BATCH, N_AB, PAST_LEN, A_KV_HEADS, HEAD_DIM)),
        'cache_v_a': nrm((DEC_BATCH, N_AB, PAST_LEN, A_KV_HEADS, HEAD_DIM)),
        'state_ret': nrm((DEC_BATCH, N_AB, 2, B_HEADS, B_DK, B_DV)),
        'state_ssm_re': nrm((DEC_BATCH, N_CD, 2, C_GROUPS, C_STATE), 0.1),
        'state_ssm_im': nrm((DEC_BATCH, N_CD, 2, C_GROUPS, C_STATE), 0.1),
        'cache_k_d': nrm((DEC_BATCH, N_CD, PAST_LEN, D_HEADS, 2, D_QK)),
        'cache_v_d': nrm((DEC_BATCH, N_CD, PAST_LEN, D_HEADS, D_V)),
        'ada_w': nrm((DEPTH, D_MODEL, 6 * D_MODEL), 0.5 * D_MODEL ** -0.5),
        'ada_b': nrm((DEPTH, 6 * D_MODEL), 0.02),
        'norm1_g': 1.0 + nrm((DEPTH, D_MODEL), 0.02),
        'norm2_g': 1.0 + nrm((DEPTH, D_MODEL), 0.02),
        'ffn_w1': nrm((DEPTH, D_MODEL, FFN_HIDDEN), D_MODEL ** -0.5),
        'ffn_w3': nrm((DEPTH, D_MODEL, FFN_HIDDEN), D_MODEL ** -0.5),
        'ffn_w2': nrm((DEPTH, FFN_HIDDEN, D_MODEL), FFN_HIDDEN ** -0.5),
        'ab_w_in': nrm((N_AB, D_MODEL, AB_COLS), D_MODEL ** -0.5),
        'ab_w_out': nrm((N_AB, AB_OUT, D_MODEL), AB_OUT ** -0.5),
        'a_q_norm': 1.0 + nrm((N_AB, HEAD_DIM), 0.02),
        'a_k_norm': 1.0 + nrm((N_AB, HEAD_DIM), 0.02),
        'a_sink': nrm((N_AB, A_HEADS), 0.5),
        'ret_decay': ret_logit + nrm((N_AB, 2, B_HEADS), 0.1),
        'ret_norm': 1.0 + nrm((N_AB, B_DV), 0.02),
        'cd_w_in': nrm((N_CD, D_MODEL, CD_COLS), D_MODEL ** -0.5),
        'cd_w_out': nrm((N_CD, CD_OUT, D_MODEL), CD_OUT ** -0.5),
        'ssm_lambda_re': -0.5 + nrm((N_CD, 2, C_GROUPS, C_STATE), 0.01),
        'ssm_lambda_im': math.pi * n_idx + nrm((N_CD, 2, C_GROUPS, C_STATE), 0.01),
        'ssm_log_dt': jax.random.uniform(next(keys), (N_CD, 2, C_GROUPS), f32, math.log(1e-3), math.log(1e-1)),
        'ssm_b_re': nrm((N_CD, C_GROUPS, C_STATE, C_GROUP), (2 * C_GROUP) ** -0.5),
        'ssm_b_im': nrm((N_CD, C_GROUPS, C_STATE, C_GROUP), (2 * C_GROUP) ** -0.5),
        'ssm_c_re': nrm((N_CD, C_GROUPS, C_GROUP, C_STATE), (2 * C_STATE) ** -0.5),
        'ssm_c_im': nrm((N_CD, C_GROUPS, C_GROUP, C_STATE), (2 * C_STATE) ** -0.5),
        'ssm_d': nrm((N_CD, C_WIDTH)),
        'ssm_glu_w': nrm((N_CD, C_WIDTH, C_WIDTH), C_WIDTH ** -0.5),
        'ssm_glu_b': nrm((N_CD, C_WIDTH), 0.02),
        'd_q_norm': 1.0 + nrm((N_CD, D_QK), 0.02),
        'd_k_norm': 1.0 + nrm((N_CD, D_QK), 0.02),
        'd_lambda': nrm((N_CD, 4, D_QK), 0.1),
        'd_subln': 1.0 + nrm((N_CD, D_V), 0.02),
    }


def reference(x_prompt, x_sample, c, c_ctx, cache_k_a, cache_v_a, state_ret, state_ssm_re, state_ssm_im,
              cache_k_d, cache_v_d, ada_w, ada_b, norm1_g, norm2_g, ffn_w1, ffn_w3, ffn_w2,
              ab_w_in, ab_w_out, a_q_norm, a_k_norm, a_sink, ret_decay, ret_norm,
              cd_w_in, cd_w_out, ssm_lambda_re, ssm_lambda_im, ssm_log_dt, ssm_b_re, ssm_b_im,
              ssm_c_re, ssm_c_im, ssm_d, ssm_glu_w, ssm_glu_b, d_q_norm, d_k_norm, d_lambda, d_subln):
    rope = axial_rope(x_sample.shape[1], HEAD_DIM)
    yp, ys = x_prompt, x_sample
    new_ka, new_va, new_ret, new_sr, new_si, new_kd, new_vd = [], [], [], [], [], [], []
    for l in range(DEPTH):
        j = l // 2
        ffn_p = (ada_w[l], ada_b[l], norm1_g[l], norm2_g[l], ffn_w1[l], ffn_w3[l], ffn_w2[l])
        if l % 2 == 0:
            mix_p = (ab_w_in[j], ab_w_out[j], a_q_norm[j], a_k_norm[j], a_sink[j], ret_decay[j], ret_norm[j])
            yp, (k_a, v_a, s_r) = trunk_layer(yp, c_ctx, lambda h: ab_mixer(h, None, None, *mix_p), *ffn_p)
            ctx_l = (cache_k_a[:, j], cache_v_a[:, j], state_ret[:, j])
            ys, _ = trunk_layer(ys, c, lambda h: ab_mixer(h, rope, ctx_l, *mix_p), *ffn_p)
            new_ka.append(k_a)
            new_va.append(v_a)
            new_ret.append(s_r)
        else:
            lam_init = 0.8 - 0.6 * math.exp(-0.3 * l)
            mix_p = (cd_w_in[j], cd_w_out[j], ssm_lambda_re[j], ssm_lambda_im[j], ssm_log_dt[j],
                     ssm_b_re[j], ssm_b_im[j], ssm_c_re[j], ssm_c_im[j], ssm_d[j], ssm_glu_w[j], ssm_glu_b[j],
                     d_q_norm[j], d_k_norm[j], d_lambda[j], d_subln[j], lam_init)
            yp, (s_re, s_im, k_d, v_d) = trunk_layer(yp, c_ctx, lambda h: cd_mixer(h, None, None, *mix_p), *ffn_p)
            ctx_l = (state_ssm_re[:, j], state_ssm_im[:, j], cache_k_d[:, j], cache_v_d[:, j])
            ys, _ = trunk_layer(ys, c, lambda h: cd_mixer(h, rope, ctx_l, *mix_p), *ffn_p)
            new_sr.append(s_re)
            new_si.append(s_im)
            new_kd.append(k_d)
            new_vd.append(v_d)
    new_cache_k_a = jnp.stack(new_ka, axis=1)
    new_cache_v_a = jnp.stack(new_va, axis=1)
    new_state_ret = jnp.stack(new_ret, axis=1)
    new_state_ssm_re = jnp.stack(new_sr, axis=1)
    new_state_ssm_im = jnp.stack(new_si, axis=1)
    new_cache_k_d = jnp.stack(new_kd, axis=1)
    new_cache_v_d = jnp.stack(new_vd, axis=1)
    return (yp, ys, new_cache_k_a, new_cache_v_a, new_state_ret, new_state_ssm_re, new_state_ssm_im,
            new_cache_k_d, new_cache_v_d)
```

```python
import functools
import math

import jax
import jax.numpy as jnp
from jax import lax
from jax.experimental import pallas as pl
from jax.experimental.pallas import tpu as pltpu

F32 = jnp.float32
BF16 = jnp.bfloat16

LANES = 128
SUBLANES = 8
HEAD_DIM = 64
GRID_W = 64
ROPE_BASE = 10000.0
WINDOW = 128
RET_CHUNK = 128
C_GROUP = 16
C_STATE = 64
NEG_INF = -1e30
EPS = 1e-6
VMEM_LIMIT = 56 * 1024 * 1024


def _params(*sem):
    return pltpu.CompilerParams(dimension_semantics=sem, vmem_limit_bytes=VMEM_LIMIT)


def _mm(a, b):
    return jnp.dot(a.astype(BF16), b.astype(BF16), preferred_element_type=F32)


def _mm_nt(a, b):
    return lax.dot_general(a.astype(BF16), b.astype(BF16), (((1,), (1,)), ((), ())),
                           preferred_element_type=F32)


def _mm_tn(a, b):
    return lax.dot_general(a.astype(BF16), b.astype(BF16), (((0,), (0,)), ((), ())),
                           preferred_element_type=F32)


def _rms(x):
    return x * lax.rsqrt(jnp.mean(x * x, axis=-1, keepdims=True) + EPS)


def _silu(x):
    return x * jax.nn.sigmoid(x)


def _lane_lo():
    return lax.broadcasted_iota(jnp.int32, (1, LANES), 1) < HEAD_DIM


def _same_head():
    r = lax.broadcasted_iota(jnp.int32, (LANES, LANES), 0) < HEAD_DIM
    c = lax.broadcasted_iota(jnp.int32, (LANES, LANES), 1) < HEAD_DIM
    return r == c


def _head_ones():
    return _same_head().astype(BF16)


def _headnorm64(x, bd):
    sq = x * x
    hi = sq.astype(BF16)
    lo = (sq - hi.astype(F32)).astype(BF16)
    ss = (jnp.dot(hi, bd, preferred_element_type=F32) + jnp.dot(lo, bd, preferred_element_type=F32))
    return x * lax.rsqrt(ss * (1.0 / HEAD_DIM) + EPS)


def _rope128(x, cos, sin_signed):
    lane = lax.broadcasted_iota(jnp.int32, (1, LANES), 1)
    first = (lane & (HEAD_DIM // 2)) == 0
    partner = jnp.where(first, pltpu.roll(x, LANES - HEAD_DIM // 2, 1), pltpu.roll(x, HEAD_DIM // 2, 1))
    return x * cos + partner * sin_signed


def _ada_kernel(c_ref, w_ref, b_ref, o_ref):
    o_ref[0] = _mm(_silu(c_ref[...]), w_ref[0]) + b_ref[0]


def _ada(cond, ada_w, ada_b):
    depth, d, n = ada_w.shape
    tn = n // 4
    return pl.pallas_call(
        _ada_kernel,
        grid=(depth, n // tn),
        in_specs=[pl.BlockSpec(cond.shape, lambda l, j: (0, 0)),
                  pl.BlockSpec((1, d, tn), lambda l, j: (l, 0, j)),
                  pl.BlockSpec((1, 1, tn), lambda l, j: (l, 0, j))],
        out_specs=pl.BlockSpec((1, cond.shape[0], tn), lambda l, j: (l, 0, j)),
        out_shape=jax.ShapeDtypeStruct((depth, cond.shape[0], n), F32),
        compiler_params=_params("parallel", "parallel"),
        name="ada_modulation",
    )(cond, ada_w, ada_b.reshape(depth, 1, n))


RAW, NORM_Q, NORM_K, SCALE_Q = 0, 1, 2, 3


def _inproj_kernel(*refs, plan, rope):
    x_ref, sh_ref, sc_ref, ng_ref, w_ref, gq_ref, gk_ref = refs[:7]
    pos = 7
    if rope:
        cos_ref, sin_ref = refs[7:9]
        pos = 9
    outs = refs[pos:]
    x = x_ref[0]
    h = _rms(x) * ng_ref[...] * (1.0 + sc_ref[0]) + sh_ref[0]
    p = _mm(h, w_ref[...])
    bd = _head_ones()
    for j, (op, oi, dj) in enumerate(plan):
        chunk = p[:, LANES * j:LANES * (j + 1)]
        if op in (NORM_Q, NORM_K):
            g = gq_ref[...] if op == NORM_Q else gk_ref[...]
            chunk = _headnorm64(chunk, bd) * g
            if rope:
                chunk = _rope128(chunk, cos_ref[...], sin_ref[...])
        elif op == SCALE_Q:
            chunk = chunk * (HEAD_DIM ** -0.5)
        outs[oi][0, :, LANES * dj:LANES * (dj + 1)] = chunk


def _inproj(x, sh, sc, ng, w, gq, gk, rope_tabs, plan, out_widths, tm, name):
    b, t, d = x.shape
    n = w.shape[1]
    rope = rope_tabs is not None
    in_specs = [pl.BlockSpec((1, tm, d), lambda i, j: (i, j, 0)),
                pl.BlockSpec((1, 1, d), lambda i, j: (i, 0, 0)),
                pl.BlockSpec((1, 1, d), lambda i, j: (i, 0, 0)),
                pl.BlockSpec((1, d), lambda i, j: (0, 0)),
                pl.BlockSpec((d, n), lambda i, j: (0, 0)),
                pl.BlockSpec((1, LANES), lambda i, j: (0, 0)),
                pl.BlockSpec((1, LANES), lambda i, j: (0, 0))]
    args = [x, sh, sc, ng, w, gq, gk]
    if rope:
        in_specs += [pl.BlockSpec((tm, LANES), lambda i, j: (j, 0))] * 2
        args += list(rope_tabs)
    return pl.pallas_call(
        functools.partial(_inproj_kernel, plan=plan, rope=rope),
        grid=(b, t // tm),
        in_specs=in_specs,
        out_specs=[pl.BlockSpec((1, tm, wd), lambda i, j: (i, j, 0)) for wd in out_widths],
        out_shape=[jax.ShapeDtypeStruct((b, t, wd), F32) for wd in out_widths],
        compiler_params=_params("parallel", "parallel"),
        name=name,
    )(*args)


def _attn_a_kernel(*refs, latent, lq):
    if latent:
        (sink_ref, q_ref, kp_ref, kc_ref, kn_ref, vp_ref, vc_ref, vn_ref, ck_ref, cv_ref, o_ref) = refs
        i = pl.program_id(1)
        last = pl.num_programs(1) - 1
        k_all = jnp.concatenate([ck_ref[0], kp_ref[0], kc_ref[0], kn_ref[0]], axis=0)
        v_all = jnp.concatenate([cv_ref[0], vp_ref[0], vc_ref[0], vn_ref[0]], axis=0)
        tc = ck_ref.shape[1]
        nk = k_all.shape[0]
        r = lax.broadcasted_iota(jnp.int32, (lq, nk), 0)
        c = lax.broadcasted_iota(jnp.int32, (lq, nk), 1) - tc
        mask = ((c < 0)
                | ((c >= 0) & (c < lq) & (c >= r) & (i > 0))
                | ((c >= lq) & (c < 2 * lq))
                | ((c >= 2 * lq) & (c - 2 * lq <= r) & (i < last)))
        mask2 = jnp.concatenate([mask, mask], axis=0)
    else:
        sink_ref, q_ref, k_ref, v_ref, o_ref = refs
        k_all = k_ref[0]
        v_all = v_ref[0]
        mask2 = None
    lo = _lane_lo()
    row = lax.broadcasted_iota(jnp.int32, (2 * lq, 1), 0)
    kr = pltpu.roll(k_all, HEAD_DIM, 1)
    vr = pltpu.roll(v_all, HEAD_DIM, 1)
    for g in range(2):
        k2 = (jnp.where(lo, k_all, kr) if g == 0 else jnp.where(lo, kr, k_all)).astype(BF16)
        v2 = (jnp.where(lo, v_all, vr) if g == 0 else jnp.where(lo, vr, v_all)).astype(BF16)
        for jp in range(2):
            j = 2 * g + jp
            q = q_ref[0, :, LANES * j:LANES * (j + 1)] * (HEAD_DIM ** -0.5)
            q2 = jnp.concatenate([jnp.where(lo, q, 0.0), jnp.where(lo, 0.0, q)], axis=0)
            s = _mm_nt(q2, k2)
            if mask2 is not None:
                s = jnp.where(mask2, s, NEG_INF)
            sk = jnp.where(row < lq, sink_ref[2 * j], sink_ref[2 * j + 1])
            m = jnp.maximum(jnp.max(s, axis=1, keepdims=True), sk)
            p = jnp.exp(s - m)
            den = jnp.sum(p, axis=1, keepdims=True) + jnp.exp(sk - m)
            o2 = _mm(p, v2) / den
            o_ref[0, :, LANES * j:LANES * (j + 1)] = jnp.where(lo, o2[:lq], o2[lq:])


def _attn_a_ctx(qkv, sink):
    b, t, _ = qkv.shape
    return pl.pallas_call(
        functools.partial(_attn_a_kernel, latent=False, lq=t),
        grid=(b,),
        in_specs=[pl.BlockSpec(memory_space=pltpu.SMEM),
                  pl.BlockSpec((1, t, 4 * LANES), lambda i: (i, 0, 0)),
                  pl.BlockSpec((1, t, LANES), lambda i: (i, 0, 4)),
                  pl.BlockSpec((1, t, LANES), lambda i: (i, 0, 5))],
        out_specs=pl.BlockSpec((1, t, 4 * LANES), lambda i: (i, 0, 0)),
        out_shape=jax.ShapeDtypeStruct((b, t, 4 * LANES), F32),
        compiler_params=_params("parallel"),
        name="attn_a_ctx",
    )(sink, qkv, qkv, qkv)


def _attn_a_lat(qkv, ck, cv, sink):
    b, t, _ = qkv.shape
    lq = WINDOW
    nb = t // lq
    tc = ck.shape[1]

    def kv_spec(col, off):
        return pl.BlockSpec((1, lq, LANES), lambda i, j: (i, jnp.clip(j + off, 0, nb - 1), col))

    return pl.pallas_call(
        functools.partial(_attn_a_kernel, latent=True, lq=lq),
        grid=(b, nb),
        in_specs=[pl.BlockSpec(memory_space=pltpu.SMEM),
                  pl.BlockSpec((1, lq, 4 * LANES), lambda i, j: (i, j, 0)),
                  kv_spec(4, -1), kv_spec(4, 0), kv_spec(4, 1),
                  kv_spec(5, -1), kv_spec(5, 0), kv_spec(5, 1),
                  pl.BlockSpec((1, tc, LANES), lambda i, j: (i, 0, 0)),
                  pl.BlockSpec((1, tc, LANES), lambda i, j: (i, 0, 0))],
        out_specs=pl.BlockSpec((1, lq, 4 * LANES), lambda i, j: (i, j, 0)),
        out_shape=jax.ShapeDtypeStruct((b, t, 4 * LANES), F32),
        compiler_params=_params("parallel", "parallel"),
        name="attn_a_lat",
    )(sink, qkv, qkv, qkv, qkv, qkv, qkv, qkv, ck, cv)


def _retention_kernel(*refs, zero_init):
    if zero_init:
        lg_ref, q_ref, k_ref, v_ref, o_ref, sfin_ref = refs
    else:
        lg_ref, q_ref, k_ref, v_ref, s0_ref, o_ref, sfin_ref = refs
    pr = pl.program_id(1)
    L = RET_CHUNK
    n = q_ref.shape[1] // L
    lo = _lane_lo()
    lgf = jnp.where(lo, lg_ref[0, 2 * pr], lg_ref[0, 2 * pr + 1])
    lgb = jnp.where(lo, lg_ref[1, 2 * pr], lg_ref[1, 2 * pr + 1])
    idx = lax.broadcasted_iota(jnp.int32, (L, 1), 0).astype(F32)
    qdf = jnp.exp(lgf * (idx + 1.0))
    kdf = jnp.exp(lgf * (L - 1.0 - idx))
    cdf = jnp.exp(lgf * float(L))
    qdb = jnp.exp(lgb * (L - idx))
    kdb = jnp.exp(lgb * idx)
    cdb = jnp.exp(lgb * float(L))
    diff = (lax.broadcasted_iota(jnp.int32, (L, L), 0) - lax.broadcasted_iota(jnp.int32, (L, L), 1)).astype(F32)

    def decay(h):
        f = jnp.exp(lg_ref[0, 2 * pr + h] * jnp.maximum(diff, 0.0))
        bwd = jnp.exp(lg_ref[1, 2 * pr + h] * jnp.maximum(-diff, 0.0))
        return jnp.where(diff > 0, f, jnp.where(diff < 0, bwd, 2.0))

    d0 = decay(0)
    d1 = decay(1)
    blk = _same_head()

    if zero_init:
        sf0 = jnp.zeros((LANES, LANES), F32)
        sb0 = jnp.zeros((LANES, LANES), F32)
    else:
        sf0 = s0_ref[0, 0, 0]
        sb0 = s0_ref[0, 0, 1]

    def fwd(c, sf):
        rows = pl.ds(pl.multiple_of(c * L, L), L)
        q = q_ref[0, rows, :]
        k = k_ref[0, rows, :]
        v = v_ref[0, rows, :]
        s0 = _mm_nt(jnp.where(lo, q, 0.0), k) * d0
        s1 = _mm_nt(jnp.where(lo, 0.0, q), k) * d1
        inner = _mm(s0, jnp.where(lo, v, 0.0)) + _mm(s1, jnp.where(lo, 0.0, v))
        o_ref[0, rows, :] = inner + _mm(q * qdf, sf)
        return sf * cdf + jnp.where(blk, _mm_tn(k * kdf, v), 0.0)

    sf = lax.fori_loop(0, n, fwd, sf0)

    def bwd(c, sb):
        rows = pl.ds(pl.multiple_of((n - 1 - c) * L, L), L)
        q = q_ref[0, rows, :]
        k = k_ref[0, rows, :]
        v = v_ref[0, rows, :]
        o_ref[0, rows, :] += _mm(q * qdb, sb)
        return sb * cdb + jnp.where(blk, _mm_tn(k * kdb, v), 0.0)

    sb = lax.fori_loop(0, n, bwd, sb0)
    sfin_ref[0, 0, 0] = sf
    sfin_ref[0, 0, 1] = sb


def _retention(qkv, log_g, s0):
    b, t, _ = qkv.shape
    npair = 4
    in_specs = [pl.BlockSpec(memory_space=pltpu.SMEM),
                pl.BlockSpec((1, t, LANES), lambda i, p: (i, 0, p)),
                pl.BlockSpec((1, t, LANES), lambda i, p: (i, 0, npair + p)),
                pl.BlockSpec((1, t, LANES), lambda i, p: (i, 0, 2 * npair + p))]
    args = [log_g, qkv, qkv, qkv]
    if s0 is not None:
        in_specs.append(pl.BlockSpec((1, 1, 2, LANES, LANES), lambda i, p: (i, p, 0, 0, 0)))
        args.append(s0)
    return pl.pallas_call(
        functools.partial(_retention_kernel, zero_init=s0 is None),
        grid=(b, npair),
        in_specs=in_specs,
        out_specs=[pl.BlockSpec((1, t, LANES), lambda i, p: (i, 0, p)),
                   pl.BlockSpec((1, 1, 2, LANES, LANES), lambda i, p: (i, p, 0, 0, 0))],
        out_shape=[jax.ShapeDtypeStruct((b, t, npair * LANES), F32),
                   jax.ShapeDtypeStruct((b, npair, 2, LANES, LANES), F32)],
        compiler_params=_params("parallel", "parallel"),
        name="retention",
    )(*args)


S5_COLS = 512


def _s5_kernel(u_ref, bre_ref, bim_ref, are_ref, aim_ref, cre_ref, cim_ref, h0_ref,
               y_ref, fin_ref, xs_ref, hc_ref, *, nb, tt):
    d = pl.program_id(0)
    i = pl.program_id(1)
    ns = are_ref.shape[2]
    half = ns // 2

    @pl.when(i == 0)
    def _():
        hc_ref[...] = h0_ref[0]

    u = u_ref[...].astype(BF16)
    uw = u.shape[1] // 2
    for hf in range(2):
        uh = u[:, uw * hf:uw * (hf + 1)]
        xs_ref[:, half * hf:half * (hf + 1)] = jnp.dot(uh, bre_ref[0, hf], preferred_element_type=F32)
        xs_ref[:, ns + half * hf:ns + half * (hf + 1)] = jnp.dot(uh, bim_ref[0, hf], preferred_element_type=F32)

    for cc in range(ns // S5_COLS):
        cre = pl.ds(cc * S5_COLS, S5_COLS)
        cim = pl.ds(ns + cc * S5_COLS, S5_COLS)
        ar = jnp.broadcast_to(are_ref[0, :, cre], (SUBLANES, S5_COLS))
        ai = jnp.broadcast_to(aim_ref[0, :, cre], (SUBLANES, S5_COLS))
        for bg in range(nb // SUBLANES):
            brow = pl.ds(bg * SUBLANES, SUBLANES)

            def step(s, carry):
                hr, hi = carry
                t = jnp.where(d == 0, s, tt - 1 - s)
                rows = pl.ds(pl.multiple_of(t * nb + bg * SUBLANES, SUBLANES), SUBLANES)
                nhr = ar * hr - ai * hi + xs_ref[rows, cre]
                nhi = ar * hi + ai * hr + xs_ref[rows, cim]
                xs_ref[rows, cre] = nhr
                xs_ref[rows, cim] = nhi
                return nhr, nhi

            hr, hi = lax.fori_loop(0, tt, step, (hc_ref[brow, cre], hc_ref[brow, cim]), unroll=2)
            hc_ref[brow, cre] = hr
            hc_ref[brow, cim] = hi

    yw = y_ref.shape[2] // 2
    for hf in range(2):
        hre = xs_ref[:, half * hf:half * (hf + 1)].astype(BF16)
        him = xs_ref[:, ns + half * hf:ns + half * (hf + 1)].astype(BF16)
        y_ref[0, :, yw * hf:yw * (hf + 1)] = (jnp.dot(hre, cre_ref[hf], preferred_element_type=F32)
                                              + jnp.dot(him, cim_ref[hf], preferred_element_type=F32))

    @pl.when(i == pl.num_programs(1) - 1)
    def _():
        fin_ref[0] = hc_ref[...]


def _s5(u_tm, nb, prm, h0):
    bre, bim, are, aim, cre, cim = prm
    rows_total, cw = u_tm.shape
    ns = are.shape[2]
    tt = 512 // nb
    rt = tt * nb
    nt = rows_total // rt

    def tile(d, i):
        return i + d * (nt - 1 - 2 * i)

    return pl.pallas_call(
        functools.partial(_s5_kernel, nb=nb, tt=tt),
        grid=(2, nt),
        in_specs=[pl.BlockSpec((rt, cw), lambda d, i: (tile(d, i), 0)),
                  pl.BlockSpec((1,) + bre.shape[1:], lambda d, i: (d, 0, 0, 0)),
                  pl.BlockSpec((1,) + bim.shape[1:], lambda d, i: (d, 0, 0, 0)),
                  pl.BlockSpec((1, 1, ns), lambda d, i: (d, 0, 0)),
                  pl.BlockSpec((1, 1, ns), lambda d, i: (d, 0, 0)),
                  pl.BlockSpec(cre.shape, lambda d, i: (0, 0, 0)),
                  pl.BlockSpec(cim.shape, lambda d, i: (0, 0, 0)),
                  pl.BlockSpec((1, nb, 2 * ns), lambda d, i: (d, 0, 0))],
        out_specs=[pl.BlockSpec((1, rt, cw), lambda d, i: (d, tile(d, i), 0)),
                   pl.BlockSpec((1, nb, 2 * ns), lambda d, i: (d, 0, 0))],
        out_shape=[jax.ShapeDtypeStruct((2, rows_total, cw), F32),
                   jax.ShapeDtypeStruct((2, nb, 2 * ns), F32)],
        scratch_shapes=[pltpu.VMEM((rt, 2 * ns), F32), pltpu.VMEM((nb, 2 * ns), F32)],
        compiler_params=_params("arbitrary", "arbitrary"),
        name="s5_scan",
    )(u_tm, bre, bim, are, aim, cre, cim, h0)


def _s5_params(lam_re, lam_im, log_dt, b_re, b_im, c_re, c_im):
    dt = jnp.exp(log_dt)[..., None]
    mag = jnp.exp(lam_re * dt)
    a_re = mag * jnp.cos(lam_im * dt)
    a_im = mag * jnp.sin(lam_im * dt)
    den = lam_re * lam_re + lam_im * lam_im
    f_re = ((a_re - 1.0) * lam_re + a_im * lam_im) / den
    f_im = (a_im * lam_re - (a_re - 1.0) * lam_im) / den
    fb_re = f_re[..., None] * b_re[None] - f_im[..., None] * b_im[None]
    fb_im = f_re[..., None] * b_im[None] + f_im[..., None] * b_re[None]
    g = lam_re.shape[1]
    hg = g // 2
    eye = jnp.eye(hg, dtype=F32)

    def bmat(fb):
        fb = fb.reshape(2, 2, hg, C_STATE, C_GROUP)
        m = jnp.einsum('dhgpc,gk->dhgckp', fb, eye)
        return m.reshape(2, 2, hg * C_GROUP, hg * C_STATE).astype(BF16)

    def cmat(cm):
        cm = cm.reshape(2, hg, C_GROUP, C_STATE)
        m = jnp.einsum('hgcp,gk->hgpkc', cm, eye)
        return m.reshape(2, hg * C_STATE, hg * C_GROUP).astype(BF16)

    ns = g * C_STATE
    return (bmat(fb_re), bmat(fb_im), a_re.reshape(2, 1, ns), a_im.reshape(2, 1, ns),
            cmat(c_re), cmat(-c_im))


def _diff_attn_kernel(*refs, latent, kc):
    if latent:
        lam_ref, q_ref, k_ref, v_ref, ck_ref, cv_ref, o_ref = refs
    else:
        lam_ref, q_ref, k_ref, v_ref, o_ref = refs
    lq = q_ref.shape[1]
    lo = _lane_lo()
    q = q_ref[0] * (HEAD_DIM ** -0.5)
    q2 = jnp.concatenate([jnp.where(lo, q, 0.0), jnp.where(lo, 0.0, q)], axis=0).astype(BF16)

    def step(kb, vb, carry):
        m, l, acc = carry
        s = _mm_nt(q2, kb)
        m_new = jnp.maximum(m, jnp.max(s, axis=1, keepdims=True))
        alpha = jnp.exp(m - m_new)
        p = jnp.exp(s - m_new)
        l = alpha * l + jnp.sum(p, axis=1, keepdims=True)
        acc = alpha * acc + _mm(p, vb)
        return m_new, l, acc

    carry = (jnp.full((2 * lq, 1), NEG_INF, F32), jnp.zeros((2 * lq, 1), F32),
             jnp.zeros((2 * lq, LANES), F32))
    if latent:
        carry = step(ck_ref[0], cv_ref[0], carry)

        def body(c, carry):
            rows = pl.ds(pl.multiple_of(c * kc, kc), kc)
            return step(k_ref[0, rows, :], v_ref[0, rows, :], carry)

        carry = lax.fori_loop(0, k_ref.shape[1] // kc, body, carry)
    else:
        carry = step(k_ref[0], v_ref[0], carry)
    _, l, acc = carry
    o2 = acc / l
    o_ref[0] = o2[:lq] - lam_ref[0] * o2[lq:]


def _diff_attn(qkv, lam, cache=None, lq=256, kc=512):
    b, t, _ = qkv.shape
    nh = 4
    latent = cache is not None
    in_specs = [pl.BlockSpec(memory_space=pltpu.SMEM),
                pl.BlockSpec((1, lq, LANES), lambda i, h, j: (i, j, h)),
                pl.BlockSpec((1, t, LANES), lambda i, h, j: (i, 0, nh + h)),
                pl.BlockSpec((1, t, LANES), lambda i, h, j: (i, 0, 2 * nh + h))]
    args = [lam, qkv, qkv, qkv]
    if latent:
        tc = cache[0].shape[1]
        in_specs += [pl.BlockSpec((1, tc, LANES), lambda i, h, j: (i, 0, h))] * 2
        args += list(cache)
    return pl.pallas_call(
        functools.partial(_diff_attn_kernel, latent=latent, kc=kc),
        grid=(b, nh, t // lq),
        in_specs=in_specs,
        out_specs=pl.BlockSpec((1, lq, LANES), lambda i, h, j: (i, j, h)),
        out_shape=jax.ShapeDtypeStruct((b, t, nh * LANES), F32),
        compiler_params=_params("parallel", "parallel", "parallel"),
        name="diff_attn_lat" if latent else "diff_attn_ctx",
    )(*args)


def _resid_ffn(x, mix, g1, n2g, sh2, sc2, g2, w1_ref, w3_ref, w2_ref):
    x1 = x + g1 * mix
    h = (_rms(x1) * n2g * (1.0 + sc2) + sh2).astype(BF16)
    a = jnp.dot(h, w1_ref[...], preferred_element_type=F32)
    b = jnp.dot(h, w3_ref[...], preferred_element_type=F32)
    return x1 + g2 * _mm(_silu(a) * b, w2_ref[...])


def _out_ab_kernel(x_ref, oa_ref, ob_ref, gb_ref, rg_ref, wo_ref, g1_ref, n2g_ref, sh2_ref, sc2_ref,
                   g2_ref, w1_ref, w3_ref, w2_ref, y_ref):
    bd = _head_ones()
    half = wo_ref.shape[0] // 2
    mix = _mm(oa_ref[0], wo_ref[:half, :])
    for j in range(half // LANES):
        cols = slice(LANES * j, LANES * (j + 1))
        ob = _headnorm64(ob_ref[0, :, cols], bd) * rg_ref[...] * _silu(gb_ref[0, :, cols])
        mix += _mm(ob, wo_ref[half + LANES * j:half + LANES * (j + 1), :])
    y_ref[0] = _resid_ffn(x_ref[0], mix, g1_ref[0], n2g_ref[...], sh2_ref[0], sc2_ref[0], g2_ref[0],
                          w1_ref, w3_ref, w2_ref)


def _out_cd_kernel(x_ref, yf_ref, yb_ref, u_ref, od_ref, dsk_ref, gw_ref, gbias_ref, sg_ref, wo_ref,
                   g1_ref, n2g_ref, sh2_ref, sc2_ref, g2_ref, w1_ref, w3_ref, w2_ref, y_ref, *, od_scale):
    half = wo_ref.shape[0] // 2
    y = yf_ref[0] + yb_ref[0] + dsk_ref[...] * u_ref[0]
    g = jax.nn.gelu(y)
    oc = g * jax.nn.sigmoid(_mm(g, gw_ref[...]) + gbias_ref[...])
    mix = _mm(oc, wo_ref[:half, :])
    for j in range(half // LANES):
        od = _rms(od_ref[0, :, LANES * j:LANES * (j + 1)]) * sg_ref[...] * od_scale
        mix += _mm(od, wo_ref[half + LANES * j:half + LANES * (j + 1), :])
    y_ref[0] = _resid_ffn(x_ref[0], mix, g1_ref[0], n2g_ref[...], sh2_ref[0], sc2_ref[0], g2_ref[0],
                          w1_ref, w3_ref, w2_ref)


def _const_spec(shape):
    nd = len(shape)
    return pl.BlockSpec(shape, lambda i, j: (0,) * nd, pipeline_mode=pl.Buffered(1))


def _out_call(kern, name, x, tok_args, const_args, mod, ffn, tm):
    b, t, d = x.shape
    g1, sh2, sc2, g2 = mod
    n2g, w1, w3, w2 = ffn

    def tok(a):
        return pl.BlockSpec((1, tm, a.shape[2]), lambda i, j: (i, j, 0))

    def per_batch(a):
        return pl.BlockSpec((1, 1, d), lambda i, j: (i, 0, 0))

    args = [x] + list(tok_args) + list(const_args) + [g1, n2g, sh2, sc2, g2, w1, w3, w2]
    in_specs = ([tok(x)] + [tok(a) for a in tok_args] + [_const_spec(a.shape) for a in const_args]
                + [per_batch(g1), _const_spec(n2g.shape), per_batch(sh2), per_batch(sc2), per_batch(g2),
                   _const_spec(w1.shape), _const_spec(w3.shape), _const_spec(w2.shape)])
    return pl.pallas_call(
        kern,
        grid=(b, t // tm),
        in_specs=in_specs,
        out_specs=tok(x),
        out_shape=jax.ShapeDtypeStruct(x.shape, F32),
        compiler_params=_params("parallel", "parallel"),
        name=name,
    )(*args)


def _ab_plan():
    plan = [(NORM_Q, 0, j) for j in range(4)] + [(NORM_K, 0, 4), (RAW, 0, 5)]
    plan += [(SCALE_Q, 1, j) for j in range(4)] + [(RAW, 1, 4 + j) for j in range(8)]
    plan += [(RAW, 2, j) for j in range(4)]
    return tuple(plan)


def _cd_plan():
    plan = [(RAW, 0, j) for j in range(4)]
    plan += [(NORM_Q, 1, j) for j in range(4)] + [(NORM_K, 1, 4 + j) for j in range(4)]
    plan += [(RAW, 1, 8 + j) for j in range(4)]
    return tuple(plan)


AB_PLAN = _ab_plan()
CD_PLAN = _cd_plan()


def _rope_tables(t):
    rows = t // GRID_W
    r = jnp.repeat(jnp.arange(rows, dtype=F32), GRID_W)
    col = jnp.tile(jnp.arange(GRID_W, dtype=F32), rows)
    n_freq = HEAD_DIM // 4
    inv = ROPE_BASE ** (-jnp.arange(n_freq, dtype=F32) / n_freq)
    ang = jnp.concatenate([r[:, None] * inv, col[:, None] * inv], axis=-1)
    cos, sin = jnp.cos(ang), jnp.sin(ang)
    cos128 = jnp.tile(cos, (1, 4))
    sin128 = jnp.tile(jnp.concatenate([-sin, sin], axis=-1), (1, 2))
    return cos128, sin128


def _tile2(g):
    return jnp.tile(g.astype(F32), 2).reshape(1, LANES)


def _time_major(a, b, t):
    w = a.shape[-1]
    return a.reshape(b, t, w).transpose(1, 0, 2).reshape(t * b, w)


def _batch_major(a, b, t):
    w = a.shape[-1]
    return a.reshape(t, b, w).transpose(1, 0, 2)


def kernel(x_prompt, x_sample, c, c_ctx, cache_k_a, cache_v_a, state_ret, state_ssm_re, state_ssm_im,
           cache_k_d, cache_v_d, ada_w, ada_b, norm1_g, norm2_g, ffn_w1, ffn_w3, ffn_w2,
           ab_w_in, ab_w_out, a_q_norm, a_k_norm, a_sink, ret_decay, ret_norm,
           cd_w_in, cd_w_out, ssm_lambda_re, ssm_lambda_im, ssm_log_dt, ssm_b_re, ssm_b_im,
           ssm_c_re, ssm_c_im, ssm_d, ssm_glu_w, ssm_glu_b, d_q_norm, d_k_norm, d_lambda, d_subln):
    nbc, tcx, d = x_prompt.shape
    nbl, tl, _ = x_sample.shape
    depth = ada_w.shape[0]
    rope_tabs = _rope_tables(tl)

    ncond = -(-(nbl + 1) // SUBLANES) * SUBLANES
    cond = jnp.zeros((ncond, d), F32).at[:nbl].set(c).at[nbl].set(c_ctx)
    mods = _ada(cond, ada_w, ada_b)

    def mod_split(l, lo, hi):
        m = mods[l, lo:hi].reshape(hi - lo, 1, 6, d)
        return [m[:, :, k] for k in range(6)]

    yp = x_prompt.reshape(1, nbc * tcx, d)
    ys = x_sample
    new_ka, new_va, new_ret, new_sr, new_si, new_kd, new_vd = [], [], [], [], [], [], []
    for l in range(depth):
        j = l // 2
        ffn = (norm2_g[l].reshape(1, d), ffn_w1[l].astype(BF16), ffn_w3[l].astype(BF16), ffn_w2[l].astype(BF16))
        n1g = norm1_g[l].reshape(1, d)
        sh1c, sc1c, g1c, sh2c, sc2c, g2c = mod_split(l, nbl, nbl + 1)
        sh1l, sc1l, g1l, sh2l, sc2l, g2l = mod_split(l, 0, nbl)
        if l % 2 == 0:
            w_in = ab_w_in[j].astype(BF16)
            w_out = ab_w_out[j].astype(BF16)
            gq, gk = _tile2(a_q_norm[j]), _tile2(a_k_norm[j])
            rg = _tile2(ret_norm[j])
            log_g = jax.nn.log_sigmoid(ret_decay[j].astype(F32))
            widths = (6 * LANES, 12 * LANES, 4 * LANES)
            qkv_a, qkv_b, gb = _inproj(yp, sh1c, sc1c, n1g, w_in, gq, gk, None, AB_PLAN, widths, 256, "inproj_ab_ctx")
            qkv_a = qkv_a.reshape(nbc, tcx, -1)
            oa = _attn_a_ctx(qkv_a, a_sink[j])
            ob, s_ret = _retention(qkv_b.reshape(nbc, tcx, -1), log_g, None)
            yp = _out_call(_out_ab_kernel, "out_ab_ctx", yp,
                           [oa.reshape(1, nbc * tcx, -1), ob.reshape(1, nbc * tcx, -1), gb],
                           [rg, w_out], (g1c, sh2c, sc2c, g2c), ffn, 256)
            new_ka.append(qkv_a[:, :, 4 * LANES:5 * LANES].reshape(nbc, tcx, 2, HEAD_DIM))
            new_va.append(qkv_a[:, :, 5 * LANES:6 * LANES].reshape(nbc, tcx, 2, HEAD_DIM))
            s_ret = jnp.stack([s_ret[..., :HEAD_DIM, :HEAD_DIM], s_ret[..., HEAD_DIM:, HEAD_DIM:]], axis=2)
            new_ret.append(s_ret.transpose(0, 3, 1, 2, 4, 5).reshape(nbc, 2, 8, HEAD_DIM, HEAD_DIM))
            qkv_a, qkv_b, gb = _inproj(ys, sh1l, sc1l, n1g, w_in, gq, gk, rope_tabs, AB_PLAN, widths, 256, "inproj_ab_lat")
            ck = cache_k_a[:, j].reshape(nbl, -1, LANES)
            cv = cache_v_a[:, j].reshape(nbl, -1, LANES)
            oa = _attn_a_lat(qkv_a, ck, cv, a_sink[j])
            s0 = state_ret[:, j].reshape(nbl, 2, 4, 2, HEAD_DIM, HEAD_DIM)
            z = jnp.zeros_like(s0[:, :, :, 0])
            s0 = jnp.concatenate([jnp.concatenate([s0[:, :, :, 0], z], axis=-1),
                                  jnp.concatenate([z, s0[:, :, :, 1]], axis=-1)], axis=-2)
            s0 = s0.transpose(0, 2, 1, 3, 4)
            ob, _ = _retention(qkv_b, log_g, s0)
            ys = _out_call(_out_ab_kernel, "out_ab_lat", ys, [oa, ob, gb], [rg, w_out],
                           (g1l, sh2l, sc2l, g2l), ffn, 256)
        else:
            lam_init = 0.8 - 0.6 * math.exp(-0.3 * l)
            w_in = cd_w_in[j].astype(BF16)
            w_out = cd_w_out[j].astype(BF16)
            gq, gk = _tile2(d_q_norm[j]), _tile2(d_k_norm[j])
            lp = d_lambda[j].astype(F32)
            lam = (jnp.exp(jnp.sum(lp[0] * lp[1])) - jnp.exp(jnp.sum(lp[2] * lp[3])) + lam_init).reshape(1)
            prm = _s5_params(ssm_lambda_re[j], ssm_lambda_im[j], ssm_log_dt[j], ssm_b_re[j], ssm_b_im[j],
                             ssm_c_re[j], ssm_c_im[j])
            ns = prm[2].shape[2]
            consts = [ssm_d[j].reshape(1, -1), ssm_glu_w[j].astype(BF16), ssm_glu_b[j].reshape(1, -1),
                      d_subln[j].reshape(1, LANES), w_out]
            kern = functools.partial(_out_cd_kernel, od_scale=1.0 - lam_init)
            widths = (4 * LANES, 12 * LANES)
            u, qkv_d = _inproj(yp, sh1c, sc1c, n1g, w_in, gq, gk, None, CD_PLAN, widths, 256, "inproj_cd_ctx")
            h0 = jnp.zeros((2, nbc, 2 * ns), F32)
            y2, fin = _s5(_time_major(u, nbc, tcx), nbc, prm, h0)
            yf = _batch_major(y2[0], nbc, tcx).reshape(1, nbc * tcx, -1)
            yb = _batch_major(y2[1], nbc, tcx).reshape(1, nbc * tcx, -1)
            qkv_d = qkv_d.reshape(nbc, tcx, -1)
            od = _diff_attn(qkv_d, lam, None, lq=tcx)
            yp = _out_call(kern, "out_cd_ctx", yp, [yf, yb, u, od.reshape(1, nbc * tcx, -1)], consts,
                           (g1c, sh2c, sc2c, g2c), ffn, 256)
            g = ns // C_STATE
            new_sr.append(fin[:, :, :ns].reshape(2, nbc, g, C_STATE).transpose(1, 0, 2, 3))
            new_si.append(fin[:, :, ns:].reshape(2, nbc, g, C_STATE).transpose(1, 0, 2, 3))
            new_kd.append(qkv_d[:, :, 4 * LANES:8 * LANES].reshape(nbc, tcx, 4, 2, HEAD_DIM))
            new_vd.append(qkv_d[:, :, 8 * LANES:].reshape(nbc, tcx, 4, 2 * HEAD_DIM))
            u, qkv_d = _inproj(ys, sh1l, sc1l, n1g, w_in, gq, gk, rope_tabs, CD_PLAN, widths, 256, "inproj_cd_lat")
            h0 = jnp.concatenate([state_ssm_re[:, j].reshape(nbl, 2, ns), state_ssm_im[:, j].reshape(nbl, 2, ns)],
                                 axis=-1).transpose(1, 0, 2)
            y2, _ = _s5(_time_major(u, nbl, tl), nbl, prm, h0)
            yf = _batch_major(y2[0], nbl, tl)
            yb = _batch_major(y2[1], nbl, tl)
            ckd = cache_k_d[:, j].reshape(nbl, -1, 4 * LANES)
            cvd = cache_v_d[:, j].reshape(nbl, -1, 4 * LANES)
            od = _diff_attn(qkv_d, lam, (ckd, cvd))
            ys = _out_call(kern, "out_cd_lat", ys, [yf, yb, u, od], consts, (g1l, sh2l, sc2l, g2l), ffn, 256)
    return (yp.reshape(nbc, tcx, d), ys,
            jnp.stack(new_ka, axis=1), jnp.stack(new_va, axis=1), jnp.stack(new_ret, axis=1),
            jnp.stack(new_sr, axis=1), jnp.stack(new_si, axis=1),
            jnp.stack(new_kd, axis=1), jnp.stack(new_vd, axis=1))
```

```python
import functools
import math

import jax
import jax.numpy as jnp
from jax import lax
from jax.experimental import pallas as pl
from jax.experimental.pallas import tpu as pltpu

F32 = jnp.float32
BF16 = jnp.bfloat16

LANES = 128
SUBLANES = 8
HEAD_DIM = 64
GRID_W = 64
ROPE_BASE = 10000.0
WINDOW = 128
RET_CHUNK = 128
C_GROUP = 16
C_STATE = 64
NEG_INF = -1e30
EPS = 1e-6
VMEM_LIMIT = 56 * 1024 * 1024


def _params(*sem):
    return pltpu.CompilerParams(dimension_semantics=sem, vmem_limit_bytes=VMEM_LIMIT)


def _mm(a, b):
    return jnp.dot(a.astype(BF16), b.astype(BF16), preferred_element_type=F32)


def _mm_nt(a, b):
    return lax.dot_general(a.astype(BF16), b.astype(BF16), (((1,), (1,)), ((), ())),
                           preferred_element_type=F32)


def _mm_tn(a, b):
    return lax.dot_general(a.astype(BF16), b.astype(BF16), (((0,), (0,)), ((), ())),
                           preferred_element_type=F32)


def _rms(x):
    return x * lax.rsqrt(jnp.mean(x * x, axis=-1, keepdims=True) + EPS)


def _silu(x):
    return x * jax.nn.sigmoid(x)


def _lane_lo():
    return lax.broadcasted_iota(jnp.int32, (1, LANES), 1) < HEAD_DIM


def _same_head():
    r = lax.broadcasted_iota(jnp.int32, (LANES, LANES), 0) < HEAD_DIM
    c = lax.broadcasted_iota(jnp.int32, (LANES, LANES), 1) < HEAD_DIM
    return r == c


def _head_ones():
    return _same_head().astype(BF16)


def _headnorm64(x, bd):
    sq = x * x
    hi = sq.astype(BF16)
    lo = (sq - hi.astype(F32)).astype(BF16)
    ss = (jnp.dot(hi, bd, preferred_element_type=F32) + jnp.dot(lo, bd, preferred_element_type=F32))
    return x * lax.rsqrt(ss * (1.0 / HEAD_DIM) + EPS)


def _rope128(x, cos, sin_signed):
    lane = lax.broadcasted_iota(jnp.int32, (1, LANES), 1)
    first = (lane & (HEAD_DIM // 2)) == 0
    partner = jnp.where(first, pltpu.roll(x, LANES - HEAD_DIM // 2, 1), pltpu.roll(x, HEAD_DIM // 2, 1))
    return x * cos + partner * sin_signed


def _ada_kernel(c_ref, w_ref, b_ref, o_ref):
    o_ref[0] = _mm(_silu(c_ref[...]), w_ref[0]) + b_ref[0]


def _ada(cond, ada_w, ada_b):
    depth, d, n = ada_w.shape
    tn = n // 4
    return pl.pallas_call(
        _ada_kernel,
        grid=(depth, n // tn),
        in_specs=[pl.BlockSpec(cond.shape, lambda l, j: (0, 0)),
                  pl.BlockSpec((1, d, tn), lambda l, j: (l, 0, j)),
                  pl.BlockSpec((1, 1, tn), lambda l, j: (l, 0, j))],
        out_specs=pl.BlockSpec((1, cond.shape[0], tn), lambda l, j: (l, 0, j)),
        out_shape=jax.ShapeDtypeStruct((depth, cond.shape[0], n), F32),
        compiler_params=_params("parallel", "parallel"),
        name="ada_modulation",
    )(cond, ada_w, ada_b.reshape(depth, 1, n))


RAW, NORM_Q, NORM_K, SCALE_Q = 0, 1, 2, 3


def _inproj_kernel(*refs, plan, rope, q_scale):
    x_ref, sh_ref, sc_ref, ng_ref, w_ref, gq_ref, gk_ref = refs[:7]
    pos = 7
    if rope:
        cos_ref, sin_ref = refs[7:9]
        pos = 9
    outs = refs[pos:]
    x = x_ref[0]
    h = _rms(x) * ng_ref[...] * (1.0 + sc_ref[0]) + sh_ref[0]
    p = _mm(h, w_ref[...])
    bd = _head_ones()
    for j, (op, oi, dj) in enumerate(plan):
        chunk = p[:, LANES * j:LANES * (j + 1)]
        if op in (NORM_Q, NORM_K):
            g = gq_ref[...] if op == NORM_Q else gk_ref[...]
            chunk = _headnorm64(chunk, bd) * g
            if rope:
                chunk = _rope128(chunk, cos_ref[...], sin_ref[...])
            if op == NORM_Q:
                chunk = chunk * q_scale
        elif op == SCALE_Q:
            chunk = chunk * (HEAD_DIM ** -0.5)
        outs[oi][0, :, LANES * dj:LANES * (dj + 1)] = chunk.astype(outs[oi].dtype)


def _inproj(x, sh, sc, ng, w, gq, gk, rope_tabs, plan, out_widths, out_dtypes, q_scale, tm, name):
    b, t, d = x.shape
    n = w.shape[1]
    rope = rope_tabs is not None
    in_specs = [pl.BlockSpec((1, tm, d), lambda i, j: (i, j, 0)),
                pl.BlockSpec((1, 1, d), lambda i, j: (i, 0, 0)),
                pl.BlockSpec((1, 1, d), lambda i, j: (i, 0, 0)),
                pl.BlockSpec((1, d), lambda i, j: (0, 0)),
                pl.BlockSpec((d, n), lambda i, j: (0, 0)),
                pl.BlockSpec((1, LANES), lambda i, j: (0, 0)),
                pl.BlockSpec((1, LANES), lambda i, j: (0, 0))]
    args = [x, sh, sc, ng, w, gq, gk]
    if rope:
        in_specs += [pl.BlockSpec((tm, LANES), lambda i, j: (j, 0))] * 2
        args += list(rope_tabs)
    return pl.pallas_call(
        functools.partial(_inproj_kernel, plan=plan, rope=rope, q_scale=q_scale),
        grid=(b, t // tm),
        in_specs=in_specs,
        out_specs=[pl.BlockSpec((1, tm, wd), lambda i, j: (i, j, 0)) for wd in out_widths],
        out_shape=[jax.ShapeDtypeStruct((b, t, wd), dt) for wd, dt in zip(out_widths, out_dtypes)],
        compiler_params=_params("parallel", "parallel"),
        name=name,
    )(*args)


def _attn_a_kernel(*refs, latent, lq):
    if latent:
        (sink_ref, q_ref, kp_ref, kc_ref, kn_ref, vp_ref, vc_ref, vn_ref, ck_ref, cv_ref, o_ref) = refs
        i = pl.program_id(1)
        last = pl.num_programs(1) - 1
        k_all = jnp.concatenate([ck_ref[0], kp_ref[0], kc_ref[0], kn_ref[0]], axis=0)
        v_all = jnp.concatenate([cv_ref[0], vp_ref[0], vc_ref[0], vn_ref[0]], axis=0)
        tc = ck_ref.shape[1]
        nk = k_all.shape[0]
        r = lax.broadcasted_iota(jnp.int32, (lq, nk), 0)
        c = lax.broadcasted_iota(jnp.int32, (lq, nk), 1) - tc
        mask = ((c < 0)
                | ((c >= 0) & (c < lq) & (c >= r) & (i > 0))
                | ((c >= lq) & (c < 2 * lq))
                | ((c >= 2 * lq) & (c - 2 * lq <= r) & (i < last)))
        mask2 = jnp.concatenate([mask, mask], axis=0)
    else:
        sink_ref, q_ref, k_ref, v_ref, o_ref = refs
        k_all = k_ref[0]
        v_all = v_ref[0]
        mask2 = None
    lo = _lane_lo()
    row = lax.broadcasted_iota(jnp.int32, (2 * lq, 1), 0)
    kr = pltpu.roll(k_all, HEAD_DIM, 1)
    vr = pltpu.roll(v_all, HEAD_DIM, 1)
    for g in range(2):
        k2 = (jnp.where(lo, k_all, kr) if g == 0 else jnp.where(lo, kr, k_all)).astype(BF16)
        v2 = (jnp.where(lo, v_all, vr) if g == 0 else jnp.where(lo, vr, v_all)).astype(BF16)
        for jp in range(2):
            j = 2 * g + jp
            q = q_ref[0, :, LANES * j:LANES * (j + 1)]
            q2 = jnp.concatenate([jnp.where(lo, q, 0.0), jnp.where(lo, 0.0, q)], axis=0)
            s = _mm_nt(q2, k2)
            if mask2 is not None:
                s = jnp.where(mask2, s, NEG_INF)
            sk = jnp.where(row < lq, sink_ref[2 * j], sink_ref[2 * j + 1])
            m = jnp.maximum(jnp.max(s, axis=1, keepdims=True), sk)
            p = jnp.exp(s - m)
            den = jnp.sum(p, axis=1, keepdims=True) + jnp.exp(sk - m)
            o2 = _mm(p, v2) / den
            o_ref[0, :, LANES * j:LANES * (j + 1)] = jnp.where(lo, o2[:lq], o2[lq:])


def _attn_a_ctx(qkv, sink):
    b, t, _ = qkv.shape
    return pl.pallas_call(
        functools.partial(_attn_a_kernel, latent=False, lq=t),
        grid=(b,),
        in_specs=[pl.BlockSpec(memory_space=pltpu.SMEM),
                  pl.BlockSpec((1, t, 4 * LANES), lambda i: (i, 0, 0)),
                  pl.BlockSpec((1, t, LANES), lambda i: (i, 0, 4)),
                  pl.BlockSpec((1, t, LANES), lambda i: (i, 0, 5))],
        out_specs=pl.BlockSpec((1, t, 4 * LANES), lambda i: (i, 0, 0)),
        out_shape=jax.ShapeDtypeStruct((b, t, 4 * LANES), F32),
        compiler_params=_params("parallel"),
        name="attn_a_ctx",
    )(sink, qkv, qkv, qkv)


def _attn_a_lat(qkv, ck, cv, sink):
    b, t, _ = qkv.shape
    lq = WINDOW
    nb = t // lq
    tc = ck.shape[1]

    def kv_spec(col, off):
        return pl.BlockSpec((1, lq, LANES), lambda i, j: (i, jnp.clip(j + off, 0, nb - 1), col))

    return pl.pallas_call(
        functools.partial(_attn_a_kernel, latent=True, lq=lq),
        grid=(b, nb),
        in_specs=[pl.BlockSpec(memory_space=pltpu.SMEM),
                  pl.BlockSpec((1, lq, 4 * LANES), lambda i, j: (i, j, 0)),
                  kv_spec(4, -1), kv_spec(4, 0), kv_spec(4, 1),
                  kv_spec(5, -1), kv_spec(5, 0), kv_spec(5, 1),
                  pl.BlockSpec((1, tc, LANES), lambda i, j: (i, 0, 0)),
                  pl.BlockSpec((1, tc, LANES), lambda i, j: (i, 0, 0))],
        out_specs=pl.BlockSpec((1, lq, 4 * LANES), lambda i, j: (i, j, 0)),
        out_shape=jax.ShapeDtypeStruct((b, t, 4 * LANES), F32),
        compiler_params=_params("parallel", "parallel"),
        name="attn_a_lat",
    )(sink, qkv, qkv, qkv, qkv, qkv, qkv, qkv, ck, cv)


def _retention_kernel(*refs, zero_init):
    if zero_init:
        lg_ref, q_ref, k_ref, v_ref, o_ref, sfin_ref = refs
    else:
        lg_ref, q_ref, k_ref, v_ref, s0_ref, o_ref, sfin_ref = refs
    pr = pl.program_id(1)
    L = RET_CHUNK
    n = q_ref.shape[1] // L
    lo = _lane_lo()
    lgf = jnp.where(lo, lg_ref[0, 2 * pr], lg_ref[0, 2 * pr + 1])
    lgb = jnp.where(lo, lg_ref[1, 2 * pr], lg_ref[1, 2 * pr + 1])
    idx = lax.broadcasted_iota(jnp.int32, (L, 1), 0).astype(F32)
    qdf = jnp.exp(lgf * (idx + 1.0))
    kdf = jnp.exp(lgf * (L - 1.0 - idx))
    cdf = jnp.exp(lgf * float(L))
    qdb = jnp.exp(lgb * (L - idx))
    kdb = jnp.exp(lgb * idx)
    cdb = jnp.exp(lgb * float(L))
    diff = (lax.broadcasted_iota(jnp.int32, (L, L), 0) - lax.broadcasted_iota(jnp.int32, (L, L), 1)).astype(F32)

    def decay(h):
        f = jnp.exp(lg_ref[0, 2 * pr + h] * jnp.maximum(diff, 0.0))
        bwd = jnp.exp(lg_ref[1, 2 * pr + h] * jnp.maximum(-diff, 0.0))
        return jnp.where(diff > 0, f, jnp.where(diff < 0, bwd, 2.0))

    d0 = decay(0)
    d1 = decay(1)
    blk = _same_head()

    if zero_init:
        sf0 = jnp.zeros((LANES, LANES), F32)
        sb0 = jnp.zeros((LANES, LANES), F32)
    else:
        sf0 = s0_ref[0, 0, 0]
        sb0 = s0_ref[0, 0, 1]

    def fwd(c, sf):
        rows = pl.ds(pl.multiple_of(c * L, L), L)
        q = q_ref[0, rows, :]
        k = k_ref[0, rows, :]
        v = v_ref[0, rows, :]
        s0 = _mm_nt(jnp.where(lo, q, 0.0), k) * d0
        s1 = _mm_nt(jnp.where(lo, 0.0, q), k) * d1
        inner = _mm(s0, jnp.where(lo, v, 0.0)) + _mm(s1, jnp.where(lo, 0.0, v))
        o_ref[0, rows, :] = inner + _mm(q * qdf, sf)
        return sf * cdf + jnp.where(blk, _mm_tn(k * kdf, v), 0.0)

    sf = lax.fori_loop(0, n, fwd, sf0)

    def bwd(c, sb):
        rows = pl.ds(pl.multiple_of((n - 1 - c) * L, L), L)
        q = q_ref[0, rows, :]
        k = k_ref[0, rows, :]
        v = v_ref[0, rows, :]
        o_ref[0, rows, :] += _mm(q * qdb, sb)
        return sb * cdb + jnp.where(blk, _mm_tn(k * kdb, v), 0.0)

    sb = lax.fori_loop(0, n, bwd, sb0)
    sfin_ref[0, 0, 0] = sf
    sfin_ref[0, 0, 1] = sb


def _retention(qkv, log_g, s0):
    b, t, _ = qkv.shape
    npair = 4
    in_specs = [pl.BlockSpec(memory_space=pltpu.SMEM),
                pl.BlockSpec((1, t, LANES), lambda i, p: (i, 0, p)),
                pl.BlockSpec((1, t, LANES), lambda i, p: (i, 0, npair + p)),
                pl.BlockSpec((1, t, LANES), lambda i, p: (i, 0, 2 * npair + p))]
    args = [log_g, qkv, qkv, qkv]
    if s0 is not None:
        in_specs.append(pl.BlockSpec((1, 1, 2, LANES, LANES), lambda i, p: (i, p, 0, 0, 0)))
        args.append(s0)
    return pl.pallas_call(
        functools.partial(_retention_kernel, zero_init=s0 is None),
        grid=(b, npair),
        in_specs=in_specs,
        out_specs=[pl.BlockSpec((1, t, LANES), lambda i, p: (i, 0, p)),
                   pl.BlockSpec((1, 1, 2, LANES, LANES), lambda i, p: (i, p, 0, 0, 0))],
        out_shape=[jax.ShapeDtypeStruct((b, t, npair * LANES), F32),
                   jax.ShapeDtypeStruct((b, npair, 2, LANES, LANES), F32)],
        compiler_params=_params("parallel", "parallel"),
        name="retention",
    )(*args)


S5_COLS = 512


def _s5_kernel(u_ref, bre_ref, bim_ref, are_ref, aim_ref, cre_ref, cim_ref, h0_ref,
               y_ref, fin_ref, xs_ref, hc_ref, *, nb, tt):
    d = pl.program_id(0)
    i = pl.program_id(1)
    ns = are_ref.shape[2]
    half = ns // 2

    @pl.when(i == 0)
    def _():
        hc_ref[...] = h0_ref[0]

    u = u_ref[...].astype(BF16)
    uw = u.shape[1] // 2
    for hf in range(2):
        uh = u[:, uw * hf:uw * (hf + 1)]
        xs_ref[:, half * hf:half * (hf + 1)] = jnp.dot(uh, bre_ref[0, hf], preferred_element_type=F32)
        xs_ref[:, ns + half * hf:ns + half * (hf + 1)] = jnp.dot(uh, bim_ref[0, hf], preferred_element_type=F32)

    for cc in range(ns // S5_COLS):
        cre = pl.ds(cc * S5_COLS, S5_COLS)
        cim = pl.ds(ns + cc * S5_COLS, S5_COLS)
        ar = jnp.broadcast_to(are_ref[0, :, cre], (SUBLANES, S5_COLS))
        ai = jnp.broadcast_to(aim_ref[0, :, cre], (SUBLANES, S5_COLS))
        for bg in range(nb // SUBLANES):
            brow = pl.ds(bg * SUBLANES, SUBLANES)

            def step(s, carry):
                hr, hi = carry
                t = jnp.where(d == 0, s, tt - 1 - s)
                rows = pl.ds(pl.multiple_of(t * nb + bg * SUBLANES, SUBLANES), SUBLANES)
                nhr = ar * hr - ai * hi + xs_ref[rows, cre]
                nhi = ar * hi + ai * hr + xs_ref[rows, cim]
                xs_ref[rows, cre] = nhr
                xs_ref[rows, cim] = nhi
                return nhr, nhi

            hr, hi = lax.fori_loop(0, tt, step, (hc_ref[brow, cre], hc_ref[brow, cim]), unroll=2)
            hc_ref[brow, cre] = hr
            hc_ref[brow, cim] = hi

    yw = y_ref.shape[2] // 2
    for hf in range(2):
        hre = xs_ref[:, half * hf:half * (hf + 1)].astype(BF16)
        him = xs_ref[:, ns + half * hf:ns + half * (hf + 1)].astype(BF16)
        y_ref[0, :, yw * hf:yw * (hf + 1)] = (jnp.dot(hre, cre_ref[hf], preferred_element_type=F32)
                                              + jnp.dot(him, cim_ref[hf], preferred_element_type=F32))

    @pl.when(i == pl.num_programs(1) - 1)
    def _():
        fin_ref[0] = hc_ref[...]


def _s5(u_tm, nb, prm, h0):
    bre, bim, are, aim, cre, cim = prm
    rows_total, cw = u_tm.shape
    ns = are.shape[2]
    tt = 512 // nb
    rt = tt * nb
    nt = rows_total // rt

    def tile(d, i):
        return i + d * (nt - 1 - 2 * i)

    return pl.pallas_call(
        functools.partial(_s5_kernel, nb=nb, tt=tt),
        grid=(2, nt),
        in_specs=[pl.BlockSpec((rt, cw), lambda d, i: (tile(d, i), 0)),
                  pl.BlockSpec((1,) + bre.shape[1:], lambda d, i: (d, 0, 0, 0)),
                  pl.BlockSpec((1,) + bim.shape[1:], lambda d, i: (d, 0, 0, 0)),
                  pl.BlockSpec((1, 1, ns), lambda d, i: (d, 0, 0)),
                  pl.BlockSpec((1, 1, ns), lambda d, i: (d, 0, 0)),
                  pl.BlockSpec(cre.shape, lambda d, i: (0, 0, 0)),
                  pl.BlockSpec(cim.shape, lambda d, i: (0, 0, 0)),
                  pl.BlockSpec((1, nb, 2 * ns), lambda d, i: (d, 0, 0))],
        out_specs=[pl.BlockSpec((1, rt, cw), lambda d, i: (d, tile(d, i), 0)),
                   pl.BlockSpec((1, nb, 2 * ns), lambda d, i: (d, 0, 0))],
        out_shape=[jax.ShapeDtypeStruct((2, rows_total, cw), F32),
                   jax.ShapeDtypeStruct((2, nb, 2 * ns), F32)],
        scratch_shapes=[pltpu.VMEM((rt, 2 * ns), F32), pltpu.VMEM((nb, 2 * ns), F32)],
        compiler_params=_params("arbitrary", "arbitrary"),
        name="s5_scan",
    )(u_tm, bre, bim, are, aim, cre, cim, h0)


def _s5_params(lam_re, lam_im, log_dt, b_re, b_im, c_re, c_im):
    dt = jnp.exp(log_dt)[..., None]
    mag = jnp.exp(lam_re * dt)
    a_re = mag * jnp.cos(lam_im * dt)
    a_im = mag * jnp.sin(lam_im * dt)
    den = lam_re * lam_re + lam_im * lam_im
    f_re = ((a_re - 1.0) * lam_re + a_im * lam_im) / den
    f_im = (a_im * lam_re - (a_re - 1.0) * lam_im) / den
    fb_re = f_re[..., None] * b_re[None] - f_im[..., None] * b_im[None]
    fb_im = f_re[..., None] * b_im[None] + f_im[..., None] * b_re[None]
    g = lam_re.shape[1]
    hg = g // 2
    eye = jnp.eye(hg, dtype=F32)

    def bmat(fb):
        fb = fb.reshape(2, 2, hg, C_STATE, C_GROUP)
        m = jnp.einsum('dhgpc,gk->dhgckp', fb, eye)
        return m.reshape(2, 2, hg * C_GROUP, hg * C_STATE).astype(BF16)

    def cmat(cm):
        cm = cm.reshape(2, hg, C_GROUP, C_STATE)
        m = jnp.einsum('hgcp,gk->hgpkc', cm, eye)
        return m.reshape(2, hg * C_STATE, hg * C_GROUP).astype(BF16)

    ns = g * C_STATE
    return (bmat(fb_re), bmat(fb_im), a_re.reshape(2, 1, ns), a_im.reshape(2, 1, ns),
            cmat(c_re), cmat(-c_im))


SHIFT_LIMIT = 50.0


def _diff_attn_kernel(*refs, latent, kc):
    if latent:
        lam_ref, q_ref, k_ref, v_ref, ck_ref, cv_ref, o_ref, kn_ref = refs
    else:
        lam_ref, q_ref, k_ref, v_ref, o_ref = refs
    lq = q_ref.shape[1]
    lo = _lane_lo()
    q = q_ref[0]
    qs = (jnp.where(lo, q, 0.0).astype(BF16), jnp.where(lo, 0.0, q).astype(BF16))

    def over_keys(step, carry):
        if latent:
            carry = step(ck_ref[0], cv_ref[0], carry)

            def body(c, carry):
                rows = pl.ds(pl.multiple_of(c * kc, kc), kc)
                return step(k_ref[0, rows, :], v_ref[0, rows, :], carry)

            return lax.fori_loop(0, k_ref.shape[1] // kc, body, carry)
        return step(k_ref[0], v_ref[0], carry)

    def finish(l0, acc0, l1, acc1):
        o_ref[0] = acc0 / l0 - lam_ref[0] * (acc1 / l1)

    def online_step(kb, vb, carry):
        kb = kb.astype(BF16)
        vb = vb.astype(BF16)
        out = []
        for qc, (m, l, acc) in zip(qs, carry):
            s = _mm_nt(qc, kb)
            m_new = jnp.maximum(m, jnp.max(s, axis=1, keepdims=True))
            alpha = jnp.exp2(m - m_new)
            p = jnp.exp2(s - m_new)
            l = alpha * l + jnp.sum(p, axis=1, keepdims=True)
            acc = alpha * acc + _mm(p, vb)
            out.append((m_new, l, acc))
        return tuple(out)

    def online():
        init = (jnp.full((lq, 1), NEG_INF, F32), jnp.zeros((lq, 1), F32), jnp.zeros((lq, LANES), F32))
        (_, l0, acc0), (_, l1, acc1) = over_keys(online_step, (init, init))
        finish(l0, acc0, l1, acc1)

    if not latent:
        online()
        return

    ones = _head_ones()

    def sq_norm_max(x):
        xf = x.astype(F32)
        return jnp.max(jnp.dot((xf * xf).astype(BF16), ones, preferred_element_type=F32), axis=0, keepdims=True)

    @pl.when(pl.program_id(2) == 0)
    def _():
        def body(c, m):
            rows = pl.ds(pl.multiple_of(c * kc, kc), kc)
            return jnp.maximum(m, sq_norm_max(k_ref[0, rows, :]))

        kn_ref[...] = lax.fori_loop(0, k_ref.shape[1] // kc, body, sq_norm_max(ck_ref[0]))

    bound_sq = jnp.max(sq_norm_max(q) * kn_ref[...], axis=1, keepdims=True)
    fixed = jnp.max(bound_sq) <= SHIFT_LIMIT * SHIFT_LIMIT

    @pl.when(fixed)
    def _():
        shift = jnp.sqrt(bound_sq)

        def step(kb, vb, carry):
            kb = kb.astype(BF16)
            vb = vb.astype(BF16)
            out = []
            for qc, (l, acc) in zip(qs, carry):
                p = jnp.exp2(_mm_nt(qc, kb) - shift)
                out.append((l + jnp.sum(p, axis=1, keepdims=True), acc + _mm(p, vb)))
            return tuple(out)

        init = (jnp.zeros((lq, 1), F32), jnp.zeros((lq, LANES), F32))
        (l0, acc0), (l1, acc1) = over_keys(step, (init, init))
        finish(l0, acc0, l1, acc1)

    @pl.when(jnp.logical_not(fixed))
    def _():
        online()


def _diff_attn(qkv, lam, cache=None, lq=512, kc=2048):
    b, t, _ = qkv.shape
    nh = 4
    latent = cache is not None
    kc = min(kc, t)
    assert t % kc == 0 and t % lq == 0
    in_specs = [pl.BlockSpec(memory_space=pltpu.SMEM),
                pl.BlockSpec((1, lq, LANES), lambda i, h, j: (i, j, h)),
                pl.BlockSpec((1, t, LANES), lambda i, h, j: (i, 0, nh + h)),
                pl.BlockSpec((1, t, LANES), lambda i, h, j: (i, 0, 2 * nh + h))]
    args = [lam, qkv, qkv, qkv]
    if latent:
        tc = cache[0].shape[1]
        in_specs += [pl.BlockSpec((1, tc, LANES), lambda i, h, j: (i, 0, h))] * 2
        args += list(cache)
    return pl.pallas_call(
        functools.partial(_diff_attn_kernel, latent=latent, kc=kc),
        grid=(b, nh, t // lq),
        in_specs=in_specs,
        out_specs=pl.BlockSpec((1, lq, LANES), lambda i, h, j: (i, j, h)),
        out_shape=jax.ShapeDtypeStruct((b, t, nh * LANES), F32),
        scratch_shapes=[pltpu.VMEM((1, LANES), F32)] if latent else [],
        compiler_params=_params("parallel", "parallel", "arbitrary"),
        name="diff_attn_lat" if latent else "diff_attn_ctx",
    )(*args)


def _resid_ffn(x, mix, g1, n2g, sh2, sc2, g2, w1_ref, w3_ref, w2_ref):
    x1 = x + g1 * mix
    h = (_rms(x1) * n2g * (1.0 + sc2) + sh2).astype(BF16)
    a = jnp.dot(h, w1_ref[...], preferred_element_type=F32)
    b = jnp.dot(h, w3_ref[...], preferred_element_type=F32)
    return x1 + g2 * _mm(_silu(a) * b, w2_ref[...])


def _out_ab_kernel(x_ref, oa_ref, ob_ref, gb_ref, rg_ref, wo_ref, g1_ref, n2g_ref, sh2_ref, sc2_ref,
                   g2_ref, w1_ref, w3_ref, w2_ref, y_ref):
    bd = _head_ones()
    half = wo_ref.shape[0] // 2
    mix = _mm(oa_ref[0], wo_ref[:half, :])
    for j in range(half // LANES):
        cols = slice(LANES * j, LANES * (j + 1))
        ob = _headnorm64(ob_ref[0, :, cols], bd) * rg_ref[...] * _silu(gb_ref[0, :, cols])
        mix += _mm(ob, wo_ref[half + LANES * j:half + LANES * (j + 1), :])
    y_ref[0] = _resid_ffn(x_ref[0], mix, g1_ref[0], n2g_ref[...], sh2_ref[0], sc2_ref[0], g2_ref[0],
                          w1_ref, w3_ref, w2_ref)


def _out_cd_kernel(x_ref, yf_ref, yb_ref, u_ref, od_ref, dsk_ref, gw_ref, gbias_ref, sg_ref, wo_ref,
                   g1_ref, n2g_ref, sh2_ref, sc2_ref, g2_ref, w1_ref, w3_ref, w2_ref, y_ref, *, od_scale):
    half = wo_ref.shape[0] // 2
    y = yf_ref[0] + yb_ref[0] + dsk_ref[...] * u_ref[0]
    g = jax.nn.gelu(y)
    oc = g * jax.nn.sigmoid(_mm(g, gw_ref[...]) + gbias_ref[...])
    mix = _mm(oc, wo_ref[:half, :])
    for j in range(half // LANES):
        od = _rms(od_ref[0, :, LANES * j:LANES * (j + 1)]) * sg_ref[...] * od_scale
        mix += _mm(od, wo_ref[half + LANES * j:half + LANES * (j + 1), :])
    y_ref[0] = _resid_ffn(x_ref[0], mix, g1_ref[0], n2g_ref[...], sh2_ref[0], sc2_ref[0], g2_ref[0],
                          w1_ref, w3_ref, w2_ref)


def _const_spec(shape):
    nd = len(shape)
    return pl.BlockSpec(shape, lambda i, j: (0,) * nd, pipeline_mode=pl.Buffered(1))


def _out_call(kern, name, x, tok_args, const_args, mod, ffn, tm):
    b, t, d = x.shape
    g1, sh2, sc2, g2 = mod
    n2g, w1, w3, w2 = ffn

    def tok(a):
        return pl.BlockSpec((1, tm, a.shape[2]), lambda i, j: (i, j, 0))

    def per_batch(a):
        return pl.BlockSpec((1, 1, d), lambda i, j: (i, 0, 0))

    args = [x] + list(tok_args) + list(const_args) + [g1, n2g, sh2, sc2, g2, w1, w3, w2]
    in_specs = ([tok(x)] + [tok(a) for a in tok_args] + [_const_spec(a.shape) for a in const_args]
                + [per_batch(g1), _const_spec(n2g.shape), per_batch(sh2), per_batch(sc2), per_batch(g2),
                   _const_spec(w1.shape), _const_spec(w3.shape), _const_spec(w2.shape)])
    return pl.pallas_call(
        kern,
        grid=(b, t // tm),
        in_specs=in_specs,
        out_specs=tok(x),
        out_shape=jax.ShapeDtypeStruct(x.shape, F32),
        compiler_params=_params("parallel", "parallel"),
        name=name,
    )(*args)


def _ab_plan():
    plan = [(NORM_Q, 0, j) for j in range(4)] + [(NORM_K, 0, 4), (RAW, 0, 5)]
    plan += [(SCALE_Q, 1, j) for j in range(4)] + [(RAW, 1, 4 + j) for j in range(8)]
    plan += [(RAW, 2, j) for j in range(4)]
    return tuple(plan)


def _cd_plan():
    plan = [(RAW, 0, j) for j in range(4)]
    plan += [(NORM_Q, 1, j) for j in range(4)] + [(NORM_K, 1, 4 + j) for j in range(4)]
    plan += [(RAW, 1, 8 + j) for j in range(4)]
    return tuple(plan)


AB_PLAN = _ab_plan()
CD_PLAN = _cd_plan()


def _rope_tables(t):
    rows = t // GRID_W
    r = jnp.repeat(jnp.arange(rows, dtype=F32), GRID_W)
    col = jnp.tile(jnp.arange(GRID_W, dtype=F32), rows)
    n_freq = HEAD_DIM // 4
    inv = ROPE_BASE ** (-jnp.arange(n_freq, dtype=F32) / n_freq)
    ang = jnp.concatenate([r[:, None] * inv, col[:, None] * inv], axis=-1)
    cos, sin = jnp.cos(ang), jnp.sin(ang)
    cos128 = jnp.tile(cos, (1, 4))
    sin128 = jnp.tile(jnp.concatenate([-sin, sin], axis=-1), (1, 2))
    return cos128, sin128


def _tile2(g):
    return jnp.tile(g.astype(F32), 2).reshape(1, LANES)


def _time_major(a, b, t):
    w = a.shape[-1]
    return a.reshape(b, t, w).transpose(1, 0, 2).reshape(t * b, w)


def _batch_major(a, b, t):
    w = a.shape[-1]
    return a.reshape(t, b, w).transpose(1, 0, 2)


def kernel(x_prompt, x_sample, c, c_ctx, cache_k_a, cache_v_a, state_ret, state_ssm_re, state_ssm_im,
           cache_k_d, cache_v_d, ada_w, ada_b, norm1_g, norm2_g, ffn_w1, ffn_w3, ffn_w2,
           ab_w_in, ab_w_out, a_q_norm, a_k_norm, a_sink, ret_decay, ret_norm,
           cd_w_in, cd_w_out, ssm_lambda_re, ssm_lambda_im, ssm_log_dt, ssm_b_re, ssm_b_im,
           ssm_c_re, ssm_c_im, ssm_d, ssm_glu_w, ssm_glu_b, d_q_norm, d_k_norm, d_lambda, d_subln):
    nbc, tcx, d = x_prompt.shape
    nbl, tl, _ = x_sample.shape
    depth = ada_w.shape[0]
    rope_tabs = _rope_tables(tl)

    ncond = -(-(nbl + 1) // SUBLANES) * SUBLANES
    cond = jnp.zeros((ncond, d), F32).at[:nbl].set(c).at[nbl].set(c_ctx)
    mods = _ada(cond, ada_w, ada_b)

    def mod_split(l, lo, hi):
        m = mods[l, lo:hi].reshape(hi - lo, 1, 6, d)
        return [m[:, :, k] for k in range(6)]

    yp = x_prompt.reshape(1, nbc * tcx, d)
    ys = x_sample
    new_ka, new_va, new_ret, new_sr, new_si, new_kd, new_vd = [], [], [], [], [], [], []
    for l in range(depth):
        j = l // 2
        ffn = (norm2_g[l].reshape(1, d), ffn_w1[l].astype(BF16), ffn_w3[l].astype(BF16), ffn_w2[l].astype(BF16))
        n1g = norm1_g[l].reshape(1, d)
        sh1c, sc1c, g1c, sh2c, sc2c, g2c = mod_split(l, nbl, nbl + 1)
        sh1l, sc1l, g1l, sh2l, sc2l, g2l = mod_split(l, 0, nbl)
        if l % 2 == 0:
            w_in = ab_w_in[j].astype(BF16)
            w_out = ab_w_out[j].astype(BF16)
            gq, gk = _tile2(a_q_norm[j]), _tile2(a_k_norm[j])
            rg = _tile2(ret_norm[j])
            log_g = jax.nn.log_sigmoid(ret_decay[j].astype(F32))
            widths = (6 * LANES, 12 * LANES, 4 * LANES)
            a_scale = HEAD_DIM ** -0.5
            qkv_a, qkv_b, gb = _inproj(yp, sh1c, sc1c, n1g, w_in, gq, gk, None, AB_PLAN, widths,
                                       (F32, F32, F32), a_scale, 256, "inproj_ab_ctx")
            qkv_a = qkv_a.reshape(nbc, tcx, -1)
            oa = _attn_a_ctx(qkv_a, a_sink[j])
            ob, s_ret = _retention(qkv_b.reshape(nbc, tcx, -1), log_g, None)
            yp = _out_call(_out_ab_kernel, "out_ab_ctx", yp,
                           [oa.reshape(1, nbc * tcx, -1), ob.reshape(1, nbc * tcx, -1), gb],
                           [rg, w_out], (g1c, sh2c, sc2c, g2c), ffn, 256)
            new_ka.append(qkv_a[:, :, 4 * LANES:5 * LANES].reshape(nbc, tcx, 2, HEAD_DIM))
            new_va.append(qkv_a[:, :, 5 * LANES:6 * LANES].reshape(nbc, tcx, 2, HEAD_DIM))
            s_ret = jnp.stack([s_ret[..., :HEAD_DIM, :HEAD_DIM], s_ret[..., HEAD_DIM:, HEAD_DIM:]], axis=2)
            new_ret.append(s_ret.transpose(0, 3, 1, 2, 4, 5).reshape(nbc, 2, 8, HEAD_DIM, HEAD_DIM))
            qkv_a, qkv_b, gb = _inproj(ys, sh1l, sc1l, n1g, w_in, gq, gk, rope_tabs, AB_PLAN, widths,
                                       (F32, F32, F32), a_scale, 256, "inproj_ab_lat")
            ck = cache_k_a[:, j].reshape(nbl, -1, LANES)
            cv = cache_v_a[:, j].reshape(nbl, -1, LANES)
            oa = _attn_a_lat(qkv_a, ck, cv, a_sink[j])
            s0 = state_ret[:, j].reshape(nbl, 2, 4, 2, HEAD_DIM, HEAD_DIM)
            z = jnp.zeros_like(s0[:, :, :, 0])
            s0 = jnp.concatenate([jnp.concatenate([s0[:, :, :, 0], z], axis=-1),
                                  jnp.concatenate([z, s0[:, :, :, 1]], axis=-1)], axis=-2)
            s0 = s0.transpose(0, 2, 1, 3, 4)
            ob, _ = _retention(qkv_b, log_g, s0)
            ys = _out_call(_out_ab_kernel, "out_ab_lat", ys, [oa, ob, gb], [rg, w_out],
                           (g1l, sh2l, sc2l, g2l), ffn, 256)
        else:
            lam_init = 0.8 - 0.6 * math.exp(-0.3 * l)
            w_in = cd_w_in[j].astype(BF16)
            w_out = cd_w_out[j].astype(BF16)
            gq, gk = _tile2(d_q_norm[j]), _tile2(d_k_norm[j])
            lp = d_lambda[j].astype(F32)
            lam = (jnp.exp(jnp.sum(lp[0] * lp[1])) - jnp.exp(jnp.sum(lp[2] * lp[3])) + lam_init).reshape(1)
            prm = _s5_params(ssm_lambda_re[j], ssm_lambda_im[j], ssm_log_dt[j], ssm_b_re[j], ssm_b_im[j],
                             ssm_c_re[j], ssm_c_im[j])
            ns = prm[2].shape[2]
            consts = [ssm_d[j].reshape(1, -1), ssm_glu_w[j].astype(BF16), ssm_glu_b[j].reshape(1, -1),
                      d_subln[j].reshape(1, LANES), w_out]
            kern = functools.partial(_out_cd_kernel, od_scale=1.0 - lam_init)
            widths = (4 * LANES, 12 * LANES)
            d_scale = HEAD_DIM ** -0.5 * math.log2(math.e)
            u, qkv_d = _inproj(yp, sh1c, sc1c, n1g, w_in, gq, gk, None, CD_PLAN, widths,
                               (F32, F32), d_scale, 256, "inproj_cd_ctx")
            h0 = jnp.zeros((2, nbc, 2 * ns), F32)
            y2, fin = _s5(_time_major(u, nbc, tcx), nbc, prm, h0)
            yf = _batch_major(y2[0], nbc, tcx).reshape(1, nbc * tcx, -1)
            yb = _batch_major(y2[1], nbc, tcx).reshape(1, nbc * tcx, -1)
            qkv_d = qkv_d.reshape(nbc, tcx, -1)
            od = _diff_attn(qkv_d, lam, None, lq=tcx)
            yp = _out_call(kern, "out_cd_ctx", yp, [yf, yb, u, od.reshape(1, nbc * tcx, -1)], consts,
                           (g1c, sh2c, sc2c, g2c), ffn, 256)
            g = ns // C_STATE
            new_sr.append(fin[:, :, :ns].reshape(2, nbc, g, C_STATE).transpose(1, 0, 2, 3))
            new_si.append(fin[:, :, ns:].reshape(2, nbc, g, C_STATE).transpose(1, 0, 2, 3))
            new_kd.append(qkv_d[:, :, 4 * LANES:8 * LANES].reshape(nbc, tcx, 4, 2, HEAD_DIM))
            new_vd.append(qkv_d[:, :, 8 * LANES:].reshape(nbc, tcx, 4, 2 * HEAD_DIM))
            u, qkv_d = _inproj(ys, sh1l, sc1l, n1g, w_in, gq, gk, rope_tabs, CD_PLAN, widths,
                               (F32, BF16), d_scale, 256, "inproj_cd_lat")
            h0 = jnp.concatenate([state_ssm_re[:, j].reshape(nbl, 2, ns), state_ssm_im[:, j].reshape(nbl, 2, ns)],
                                 axis=-1).transpose(1, 0, 2)
            y2, _ = _s5(_time_major(u, nbl, tl), nbl, prm, h0)
            yf = _batch_major(y2[0], nbl, tl)
            yb = _batch_major(y2[1], nbl, tl)
            ckd = cache_k_d[:, j].reshape(nbl, -1, 4 * LANES)
            cvd = cache_v_d[:, j].reshape(nbl, -1, 4 * LANES)
            od = _diff_attn(qkv_d, lam, (ckd, cvd))
            ys = _out_call(kern, "out_cd_lat", ys, [yf, yb, u, od], consts, (g1l, sh2l, sc2l, g2l), ffn, 256)
    return (yp.reshape(nbc, tcx, d), ys,
            jnp.stack(new_ka, axis=1), jnp.stack(new_va, axis=1), jnp.stack(new_ret, axis=1),
            jnp.stack(new_sr, axis=1), jnp.stack(new_si, axis=1),
            jnp.stack(new_kd, axis=1), jnp.stack(new_vd, axis=1))
```

```python
import functools
import math

import jax
import jax.numpy as jnp
from jax import lax
from jax.experimental import pallas as pl
from jax.experimental.pallas import tpu as pltpu

F32 = jnp.float32
BF16 = jnp.bfloat16

LANES = 128
SUBLANES = 8
HEAD_DIM = 64
GRID_W = 64
ROPE_BASE = 10000.0
WINDOW = 128
RET_CHUNK = 128
RET_UNROLL = 8
FFN_SPLIT = 2
TOKEN_TILE = 512
C_GROUP = 16
C_STATE = 64
NEG_INF = -1e30
EPS = 1e-6
LOG2E = math.log2(math.e)
Q_SCALE = HEAD_DIM ** -0.5 * LOG2E
SHIFT_LIMIT = 50.0
VMEM_LIMIT = 56 * 1024 * 1024


def _params(*sem):
    return pltpu.CompilerParams(dimension_semantics=sem, vmem_limit_bytes=VMEM_LIMIT)


def _mm(a, b):
    return jnp.dot(a.astype(BF16), b.astype(BF16), preferred_element_type=F32)


def _mm_nt(a, b):
    return lax.dot_general(a.astype(BF16), b.astype(BF16), (((1,), (1,)), ((), ())),
                           preferred_element_type=F32)


def _mm_tn(a, b):
    return lax.dot_general(a.astype(BF16), b.astype(BF16), (((0,), (0,)), ((), ())),
                           preferred_element_type=F32)


def _rms(x):
    return x * lax.rsqrt(jnp.mean(x * x, axis=-1, keepdims=True) + EPS)


def _silu(x):
    return x * jax.nn.sigmoid(x)


def _lane_lo():
    return lax.broadcasted_iota(jnp.int32, (1, LANES), 1) < HEAD_DIM


def _same_head():
    r = lax.broadcasted_iota(jnp.int32, (LANES, LANES), 0) < HEAD_DIM
    c = lax.broadcasted_iota(jnp.int32, (LANES, LANES), 1) < HEAD_DIM
    return r == c


def _head_ones():
    return _same_head().astype(BF16)


def _headnorm64(x, bd):
    sq = x * x
    hi = sq.astype(BF16)
    lo = (sq - hi.astype(F32)).astype(BF16)
    ss = (jnp.dot(hi, bd, preferred_element_type=F32) + jnp.dot(lo, bd, preferred_element_type=F32))
    return x * lax.rsqrt(ss * (1.0 / HEAD_DIM) + EPS)


def _max_sq_head_norm(x):
    bd = _head_ones()
    best = None
    for j in range(x.shape[1] // LANES):
        xf = x[:, LANES * j:LANES * (j + 1)].astype(F32)
        m = jnp.max(jnp.dot((xf * xf).astype(BF16), bd, preferred_element_type=F32))
        best = m if best is None else jnp.maximum(best, m)
    return best


def _norm_bounds(gq, gk):
    q_sq = HEAD_DIM * Q_SCALE * Q_SCALE * jnp.max(gq.astype(F32) ** 2)
    k_sq = HEAD_DIM * jnp.max(gk.astype(F32) ** 2)
    return jnp.stack([q_sq, k_sq])


def _rope128(x, cos, sin_signed):
    lane = lax.broadcasted_iota(jnp.int32, (1, LANES), 1)
    first = (lane & (HEAD_DIM // 2)) == 0
    partner = jnp.where(first, pltpu.roll(x, LANES - HEAD_DIM // 2, 1), pltpu.roll(x, HEAD_DIM // 2, 1))
    return x * cos + partner * sin_signed


def _ada_kernel(c_ref, w_ref, b_ref, o_ref):
    o_ref[0] = _mm(_silu(c_ref[...]), w_ref[0]) + b_ref[0]


def _ada(cond, ada_w, ada_b):
    depth, d, n = ada_w.shape
    tn = n // 4
    return pl.pallas_call(
        _ada_kernel,
        grid=(depth, n // tn),
        in_specs=[pl.BlockSpec(cond.shape, lambda l, j: (0, 0)),
                  pl.BlockSpec((1, d, tn), lambda l, j: (l, 0, j)),
                  pl.BlockSpec((1, 1, tn), lambda l, j: (l, 0, j))],
        out_specs=pl.BlockSpec((1, cond.shape[0], tn), lambda l, j: (l, 0, j)),
        out_shape=jax.ShapeDtypeStruct((depth, cond.shape[0], n), F32),
        compiler_params=_params("parallel", "parallel"),
        name="ada_modulation",
    )(cond, ada_w, ada_b.reshape(depth, 1, n))


RAW, NORM_Q, NORM_K, SCALE_Q = 0, 1, 2, 3


def _inproj_kernel(*refs, plan, rope, q_scale):
    x_ref, sh_ref, sc_ref, ng_ref, w_ref, gq_ref, gk_ref = refs[:7]
    pos = 7
    if rope:
        cos_ref, sin_ref = refs[7:9]
        pos = 9
    outs = refs[pos:]
    x = x_ref[0]
    h = _rms(x) * ng_ref[...] * (1.0 + sc_ref[0]) + sh_ref[0]
    p = _mm(h, w_ref[...])
    bd = _head_ones()
    for j, (op, oi, dj) in enumerate(plan):
        chunk = p[:, LANES * j:LANES * (j + 1)]
        if op in (NORM_Q, NORM_K):
            g = gq_ref[...] if op == NORM_Q else gk_ref[...]
            chunk = _headnorm64(chunk, bd) * g
            if rope:
                chunk = _rope128(chunk, cos_ref[...], sin_ref[...])
            if op == NORM_Q:
                chunk = chunk * q_scale
        elif op == SCALE_Q:
            chunk = chunk * (HEAD_DIM ** -0.5)
        outs[oi][0, :, LANES * dj:LANES * (dj + 1)] = chunk.astype(outs[oi].dtype)


def _inproj(x, sh, sc, ng, w, gq, gk, rope_tabs, plan, out_widths, out_dtypes, q_scale, tm, name):
    b, t, d = x.shape
    n = w.shape[1]
    rope = rope_tabs is not None
    in_specs = [pl.BlockSpec((1, tm, d), lambda i, j: (i, j, 0)),
                pl.BlockSpec((1, 1, d), lambda i, j: (i, 0, 0)),
                pl.BlockSpec((1, 1, d), lambda i, j: (i, 0, 0)),
                pl.BlockSpec((1, d), lambda i, j: (0, 0)),
                pl.BlockSpec((d, n), lambda i, j: (0, 0)),
                pl.BlockSpec((1, LANES), lambda i, j: (0, 0)),
                pl.BlockSpec((1, LANES), lambda i, j: (0, 0))]
    args = [x, sh, sc, ng, w, gq, gk]
    if rope:
        in_specs += [pl.BlockSpec((tm, LANES), lambda i, j: (j, 0))] * 2
        args += list(rope_tabs)
    return pl.pallas_call(
        functools.partial(_inproj_kernel, plan=plan, rope=rope, q_scale=q_scale),
        grid=(b, t // tm),
        in_specs=in_specs,
        out_specs=[pl.BlockSpec((1, tm, wd), lambda i, j: (i, j, 0)) for wd in out_widths],
        out_shape=[jax.ShapeDtypeStruct((b, t, wd), dt) for wd, dt in zip(out_widths, out_dtypes)],
        compiler_params=_params("parallel", "parallel"),
        name=name,
    )(*args)


def _attn_a_kernel(*refs, latent, lq):
    if latent:
        (bnd_ref, sink_ref, q_ref, kp_ref, kc_ref, kn_ref, vp_ref, vc_ref, vn_ref, ck_ref, cv_ref, o_ref,
         cn_ref) = refs
        i = pl.program_id(1)

        @pl.when(i == 0)
        def _():
            cn_ref[0] = _max_sq_head_norm(ck_ref[0])

        k_sq = jnp.maximum(bnd_ref[1], cn_ref[0])
        last = pl.num_programs(1) - 1
        k_all = jnp.concatenate([ck_ref[0], kp_ref[0], kc_ref[0], kn_ref[0]], axis=0)
        v_all = jnp.concatenate([cv_ref[0], vp_ref[0], vc_ref[0], vn_ref[0]], axis=0)
        tc = ck_ref.shape[1]
        assert lq & (lq - 1) == 0
        r = lax.broadcasted_iota(jnp.int32, (2 * lq, lq), 0) & (lq - 1)
        c = lax.broadcasted_iota(jnp.int32, (2 * lq, lq), 1)
        piece_masks = {tc: (c >= r) & (i > 0), tc + 2 * lq: (c <= r) & (i < last)}
    else:
        bnd_ref, sink_ref, q_ref, k_ref, v_ref, o_ref = refs
        k_all = k_ref[0]
        v_all = v_ref[0]
        piece_masks = {}
        k_sq = bnd_ref[1]
    nk = k_all.shape[0]
    lo = _lane_lo()
    row = lax.broadcasted_iota(jnp.int32, (2 * lq, 1), 0)

    def masked(x, fill):
        if not piece_masks:
            return x
        pieces = [jnp.where(piece_masks[o], x[:, o:o + lq], fill) if o in piece_masks else x[:, o:o + lq]
                  for o in range(0, nk, lq)]
        return jnp.concatenate(pieces, axis=1)

    sinks = [sink_ref[h] * LOG2E for h in range(8)]
    sink_max = functools.reduce(jnp.maximum, sinks)
    bound_sq = bnd_ref[0] * k_sq
    fixed = (bound_sq <= SHIFT_LIMIT * SHIFT_LIMIT) & (sink_max <= SHIFT_LIMIT)

    def attend(softmax):
        for g in range(2):
            k2 = k_all[:, LANES * g:LANES * (g + 1)].astype(BF16)
            v2 = v_all[:, LANES * g:LANES * (g + 1)].astype(BF16)
            for jp in range(2):
                j = 2 * g + jp
                q = q_ref[0, :, LANES * j:LANES * (j + 1)]
                q2 = jnp.concatenate([jnp.where(lo, q, 0.0), jnp.where(lo, 0.0, q)], axis=0)
                sk = jnp.where(row < lq, sinks[2 * j], sinks[2 * j + 1])
                p, den = softmax(_mm_nt(q2, k2), sk)
                o2 = _mm(p, v2) / den
                o_ref[0, :, LANES * j:LANES * (j + 1)] = jnp.where(lo, o2[:lq], o2[lq:])

    @pl.when(fixed)
    def _():
        shift = jnp.maximum(jnp.sqrt(jnp.full((1, 1), bound_sq, F32)), sink_max)

        def softmax(s, sk):
            p = masked(jnp.exp2(s - shift), 0.0)
            return p, jnp.sum(p, axis=1, keepdims=True) + jnp.exp2(sk - shift)

        attend(softmax)

    @pl.when(jnp.logical_not(fixed))
    def _():
        def softmax(s, sk):
            s = masked(s, NEG_INF)
            m = jnp.maximum(jnp.max(s, axis=1, keepdims=True), sk)
            p = jnp.exp2(s - m)
            return p, jnp.sum(p, axis=1, keepdims=True) + jnp.exp2(sk - m)

        attend(softmax)


def _attn_a_ctx(qkv, sink, bounds):
    b, t, _ = qkv.shape
    return pl.pallas_call(
        functools.partial(_attn_a_kernel, latent=False, lq=t),
        grid=(b,),
        in_specs=[pl.BlockSpec(memory_space=pltpu.SMEM),
                  pl.BlockSpec(memory_space=pltpu.SMEM),
                  pl.BlockSpec((1, t, 4 * LANES), lambda i: (i, 0, 0)),
                  pl.BlockSpec((1, t, 2 * LANES), lambda i: (i, 0, 2)),
                  pl.BlockSpec((1, t, 2 * LANES), lambda i: (i, 0, 3))],
        out_specs=pl.BlockSpec((1, t, 4 * LANES), lambda i: (i, 0, 0)),
        out_shape=jax.ShapeDtypeStruct((b, t, 4 * LANES), F32),
        compiler_params=_params("parallel"),
        name="attn_a_ctx",
    )(bounds, sink, qkv, qkv, qkv)


def _attn_a_lat(qkv, ck, cv, sink, bounds):
    b, t, _ = qkv.shape
    lq = WINDOW
    nb = t // lq
    tc = ck.shape[1]

    def kv_spec(col, off):
        return pl.BlockSpec((1, lq, 2 * LANES), lambda i, j: (i, jnp.clip(j + off, 0, nb - 1), col))

    return pl.pallas_call(
        functools.partial(_attn_a_kernel, latent=True, lq=lq),
        grid=(b, nb),
        in_specs=[pl.BlockSpec(memory_space=pltpu.SMEM),
                  pl.BlockSpec(memory_space=pltpu.SMEM),
                  pl.BlockSpec((1, lq, 4 * LANES), lambda i, j: (i, j, 0)),
                  kv_spec(2, -1), kv_spec(2, 0), kv_spec(2, 1),
                  kv_spec(3, -1), kv_spec(3, 0), kv_spec(3, 1),
                  pl.BlockSpec((1, tc, 2 * LANES), lambda i, j: (i, 0, 0)),
                  pl.BlockSpec((1, tc, 2 * LANES), lambda i, j: (i, 0, 0))],
        out_specs=pl.BlockSpec((1, lq, 4 * LANES), lambda i, j: (i, j, 0)),
        out_shape=jax.ShapeDtypeStruct((b, t, 4 * LANES), F32),
        scratch_shapes=[pltpu.SMEM((1,), F32)],
        compiler_params=_params("parallel", "arbitrary"),
        name="attn_a_lat",
    )(bounds, sink, qkv, qkv, qkv, qkv, qkv, qkv, qkv, ck, cv)


def _retention_kernel(*refs, zero_init):
    if zero_init:
        lg_ref, q_ref, k_ref, v_ref, o_ref, sfin_ref, sp_ref = refs
    else:
        lg_ref, q_ref, k_ref, v_ref, s0_ref, o_ref, sfin_ref, sp_ref = refs
    pr = pl.program_id(1)
    L = RET_CHUNK
    n = q_ref.shape[1] // L
    lo = _lane_lo()
    lgf = jnp.where(lo, lg_ref[0, 2 * pr], lg_ref[0, 2 * pr + 1])
    lgb = jnp.where(lo, lg_ref[1, 2 * pr], lg_ref[1, 2 * pr + 1])
    idx = lax.broadcasted_iota(jnp.int32, (L, 1), 0).astype(F32)
    qdf = jnp.exp(lgf * (idx + 1.0))
    kdf = jnp.exp(lgf * (L - 1.0 - idx))
    cdf = jnp.exp(lgf * float(L))
    qdb = jnp.exp(lgb * (L - idx))
    kdb = jnp.exp(lgb * idx)
    cdb = jnp.exp(lgb * float(L))
    diff = (lax.broadcasted_iota(jnp.int32, (L, L), 0) - lax.broadcasted_iota(jnp.int32, (L, L), 1)).astype(F32)

    def decay(h):
        f = jnp.exp(lg_ref[0, 2 * pr + h] * jnp.maximum(diff, 0.0))
        bwd = jnp.exp(lg_ref[1, 2 * pr + h] * jnp.maximum(-diff, 0.0))
        return jnp.where(diff > 0, f, jnp.where(diff < 0, bwd, 2.0))

    d0 = decay(0)
    d1 = decay(1)
    blk = _same_head()

    if zero_init:
        sf0 = jnp.zeros((LANES, LANES), F32)
        sb0 = jnp.zeros((LANES, LANES), F32)
    else:
        sf0 = s0_ref[0, 0, 0]
        sb0 = s0_ref[0, 0, 1]

    qd = jnp.concatenate([qdf, qdb], axis=1)
    kd = jnp.concatenate([kdf, kdb], axis=1)
    blk2 = jnp.concatenate([blk, blk], axis=0)
    unroll = math.gcd(n, RET_UNROLL)

    def chunk(c):
        return pl.ds(pl.multiple_of(c * L, L), L)

    def kv_body(c, carry):
        k = k_ref[0, chunk(c), :]
        kv = _mm_tn(jnp.concatenate([k, k], axis=1) * kd, v_ref[0, chunk(c), :])
        sp_ref[c] = jnp.where(blk2, kv, 0.0)
        return carry

    lax.fori_loop(0, n, kv_body, 0, unroll=unroll)

    def f_body(c, sf):
        kv = sp_ref[c, :LANES, :]
        sp_ref[c, :LANES, :] = sf
        return sf * cdf + kv

    def b_body(i, sb):
        c = n - 1 - i
        kv = sp_ref[c, LANES:, :]
        sp_ref[c, LANES:, :] = sb
        return sb * cdb + kv

    sfin_ref[0, 0, 0] = lax.fori_loop(0, n, f_body, sf0)
    sfin_ref[0, 0, 1] = lax.fori_loop(0, n, b_body, sb0)

    def o_body(c, carry):
        q = q_ref[0, chunk(c), :]
        v = v_ref[0, chunk(c), :]
        s = _mm_nt(jnp.concatenate([jnp.where(lo, q, 0.0), jnp.where(lo, 0.0, q)], axis=0), k_ref[0, chunk(c), :])
        sd = jnp.concatenate([s[:L] * d0, s[L:] * d1], axis=1)
        v2 = jnp.concatenate([jnp.where(lo, v, 0.0), jnp.where(lo, 0.0, v)], axis=0)
        o_ref[0, chunk(c), :] = _mm(sd, v2) + _mm(jnp.concatenate([q, q], axis=1) * qd, sp_ref[c])
        return carry

    lax.fori_loop(0, n, o_body, 0, unroll=unroll)


def _retention(qkv, log_g, s0):
    b, t, _ = qkv.shape
    npair = 4
    in_specs = [pl.BlockSpec(memory_space=pltpu.SMEM),
                pl.BlockSpec((1, t, LANES), lambda i, p: (i, 0, p)),
                pl.BlockSpec((1, t, LANES), lambda i, p: (i, 0, npair + p)),
                pl.BlockSpec((1, t, LANES), lambda i, p: (i, 0, 2 * npair + p))]
    args = [log_g, qkv, qkv, qkv]
    if s0 is not None:
        in_specs.append(pl.BlockSpec((1, 1, 2, LANES, LANES), lambda i, p: (i, p, 0, 0, 0)))
        args.append(s0)
    return pl.pallas_call(
        functools.partial(_retention_kernel, zero_init=s0 is None),
        grid=(b, npair),
        in_specs=in_specs,
        out_specs=[pl.BlockSpec((1, t, LANES), lambda i, p: (i, 0, p)),
                   pl.BlockSpec((1, 1, 2, LANES, LANES), lambda i, p: (i, p, 0, 0, 0))],
        out_shape=[jax.ShapeDtypeStruct((b, t, npair * LANES), F32),
                   jax.ShapeDtypeStruct((b, npair, 2, LANES, LANES), F32)],
        scratch_shapes=[pltpu.VMEM((t // RET_CHUNK, 2 * LANES, LANES), F32)],
        compiler_params=_params("parallel", "parallel"),
        name="retention",
    )(*args)


S5_COLS = 512


def _s5_kernel(u_ref, bre_ref, bim_ref, are_ref, aim_ref, cre_ref, cim_ref, h0_ref,
               y_ref, fin_ref, xs_ref, hc_ref, *, nb, tt):
    d = pl.program_id(0)
    i = pl.program_id(1)
    ns = are_ref.shape[2]
    half = ns // 2

    @pl.when(i == 0)
    def _():
        hc_ref[...] = h0_ref[0]

    u = u_ref[...].astype(BF16)
    uw = u.shape[1] // 2
    for hf in range(2):
        uh = u[:, uw * hf:uw * (hf + 1)]
        xs_ref[:, half * hf:half * (hf + 1)] = jnp.dot(uh, bre_ref[0, hf], preferred_element_type=F32)
        xs_ref[:, ns + half * hf:ns + half * (hf + 1)] = jnp.dot(uh, bim_ref[0, hf], preferred_element_type=F32)

    for cc in range(ns // S5_COLS):
        cre = pl.ds(cc * S5_COLS, S5_COLS)
        cim = pl.ds(ns + cc * S5_COLS, S5_COLS)
        ar = jnp.broadcast_to(are_ref[0, :, cre], (SUBLANES, S5_COLS))
        ai = jnp.broadcast_to(aim_ref[0, :, cre], (SUBLANES, S5_COLS))
        for bg in range(nb // SUBLANES):
            brow = pl.ds(bg * SUBLANES, SUBLANES)

            def step(s, carry):
                hr, hi = carry
                t = jnp.where(d == 0, s, tt - 1 - s)
                rows = pl.ds(pl.multiple_of(t * nb + bg * SUBLANES, SUBLANES), SUBLANES)
                nhr = ar * hr - ai * hi + xs_ref[rows, cre]
                nhi = ar * hi + ai * hr + xs_ref[rows, cim]
                xs_ref[rows, cre] = nhr
                xs_ref[rows, cim] = nhi
                return nhr, nhi

            hr, hi = lax.fori_loop(0, tt, step, (hc_ref[brow, cre], hc_ref[brow, cim]), unroll=2)
            hc_ref[brow, cre] = hr
            hc_ref[brow, cim] = hi

    yw = y_ref.shape[2] // 2
    for hf in range(2):
        hre = xs_ref[:, half * hf:half * (hf + 1)].astype(BF16)
        him = xs_ref[:, ns + half * hf:ns + half * (hf + 1)].astype(BF16)
        y_ref[0, :, yw * hf:yw * (hf + 1)] = (jnp.dot(hre, cre_ref[hf], preferred_element_type=F32)
                                              + jnp.dot(him, cim_ref[hf], preferred_element_type=F32))

    @pl.when(i == pl.num_programs(1) - 1)
    def _():
        fin_ref[0] = hc_ref[...]


def _s5(u_tm, nb, prm, h0):
    bre, bim, are, aim, cre, cim = prm
    rows_total, cw = u_tm.shape
    ns = are.shape[2]
    tt = 512 // nb
    rt = tt * nb
    nt = rows_total // rt

    def tile(d, i):
        return i + d * (nt - 1 - 2 * i)

    return pl.pallas_call(
        functools.partial(_s5_kernel, nb=nb, tt=tt),
        grid=(2, nt),
        in_specs=[pl.BlockSpec((rt, cw), lambda d, i: (tile(d, i), 0)),
                  pl.BlockSpec((1,) + bre.shape[1:], lambda d, i: (d, 0, 0, 0)),
                  pl.BlockSpec((1,) + bim.shape[1:], lambda d, i: (d, 0, 0, 0)),
                  pl.BlockSpec((1, 1, ns), lambda d, i: (d, 0, 0)),
                  pl.BlockSpec((1, 1, ns), lambda d, i: (d, 0, 0)),
                  pl.BlockSpec(cre.shape, lambda d, i: (0, 0, 0)),
                  pl.BlockSpec(cim.shape, lambda d, i: (0, 0, 0)),
                  pl.BlockSpec((1, nb, 2 * ns), lambda d, i: (d, 0, 0))],
        out_specs=[pl.BlockSpec((1, rt, cw), lambda d, i: (d, tile(d, i), 0)),
                   pl.BlockSpec((1, nb, 2 * ns), lambda d, i: (d, 0, 0))],
        out_shape=[jax.ShapeDtypeStruct((2, rows_total, cw), F32),
                   jax.ShapeDtypeStruct((2, nb, 2 * ns), F32)],
        scratch_shapes=[pltpu.VMEM((rt, 2 * ns), F32), pltpu.VMEM((nb, 2 * ns), F32)],
        compiler_params=_params("arbitrary", "arbitrary"),
        name="s5_scan",
    )(u_tm, bre, bim, are, aim, cre, cim, h0)


def _s5_params(lam_re, lam_im, log_dt, b_re, b_im, c_re, c_im):
    dt = jnp.exp(log_dt)[..., None]
    mag = jnp.exp(lam_re * dt)
    a_re = mag * jnp.cos(lam_im * dt)
    a_im = mag * jnp.sin(lam_im * dt)
    den = lam_re * lam_re + lam_im * lam_im
    f_re = ((a_re - 1.0) * lam_re + a_im * lam_im) / den
    f_im = (a_im * lam_re - (a_re - 1.0) * lam_im) / den
    fb_re = f_re[..., None] * b_re[None] - f_im[..., None] * b_im[None]
    fb_im = f_re[..., None] * b_im[None] + f_im[..., None] * b_re[None]
    g = lam_re.shape[1]
    hg = g // 2
    eye = jnp.eye(hg, dtype=F32)

    def bmat(fb):
        fb = fb.reshape(2, 2, hg, C_STATE, C_GROUP)
        m = jnp.einsum('dhgpc,gk->dhgckp', fb, eye)
        return m.reshape(2, 2, hg * C_GROUP, hg * C_STATE).astype(BF16)

    def cmat(cm):
        cm = cm.reshape(2, hg, C_GROUP, C_STATE)
        m = jnp.einsum('hgcp,gk->hgpkc', cm, eye)
        return m.reshape(2, hg * C_STATE, hg * C_GROUP).astype(BF16)

    ns = g * C_STATE
    return (bmat(fb_re), bmat(fb_im), a_re.reshape(2, 1, ns), a_im.reshape(2, 1, ns),
            cmat(c_re), cmat(-c_im))


def _diff_attn_kernel(*refs, latent, kc):
    if latent:
        bnd_ref, lam_ref, q_ref, k_ref, v_ref, ck_ref, cv_ref, o_ref, cn_ref = refs
    else:
        lam_ref, q_ref, k_ref, v_ref, o_ref = refs
    lq = q_ref.shape[1]
    lo = _lane_lo()
    q = q_ref[0]
    qs = (jnp.where(lo, q, 0.0).astype(BF16), jnp.where(lo, 0.0, q).astype(BF16))

    def over_keys(step, carry):
        if latent:
            carry = step(ck_ref[0], cv_ref[0], carry)

            def body(c, carry):
                rows = pl.ds(pl.multiple_of(c * kc, kc), kc)
                return step(k_ref[0, rows, :], v_ref[0, rows, :], carry)

            return lax.fori_loop(0, k_ref.shape[1] // kc, body, carry)
        return step(k_ref[0], v_ref[0], carry)

    def finish(l0, acc0, l1, acc1):
        o_ref[0] = acc0 / l0 - lam_ref[0] * (acc1 / l1)

    def online_step(kb, vb, carry):
        kb = kb.astype(BF16)
        vb = vb.astype(BF16)
        out = []
        for qc, (m, l, acc) in zip(qs, carry):
            s = _mm_nt(qc, kb)
            m_new = jnp.maximum(m, jnp.max(s, axis=1, keepdims=True))
            alpha = jnp.exp2(m - m_new)
            p = jnp.exp2(s - m_new)
            l = alpha * l + jnp.sum(p, axis=1, keepdims=True)
            acc = alpha * acc + _mm(p, vb)
            out.append((m_new, l, acc))
        return tuple(out)

    def online():
        init = (jnp.full((lq, 1), NEG_INF, F32), jnp.zeros((lq, 1), F32), jnp.zeros((lq, LANES), F32))
        (_, l0, acc0), (_, l1, acc1) = over_keys(online_step, (init, init))
        finish(l0, acc0, l1, acc1)

    if not latent:
        online()
        return

    @pl.when(pl.program_id(2) == 0)
    def _():
        cn_ref[0] = _max_sq_head_norm(ck_ref[0])

    bound_sq = bnd_ref[0] * jnp.maximum(bnd_ref[1], cn_ref[0])
    fixed = bound_sq <= SHIFT_LIMIT * SHIFT_LIMIT

    @pl.when(fixed)
    def _():
        shift = jnp.sqrt(jnp.full((1, 1), bound_sq, F32))

        def step(kb, vb, carry):
            kb = kb.astype(BF16)
            vb = vb.astype(BF16)
            out = []
            for qc, (l, acc) in zip(qs, carry):
                p = jnp.exp2(_mm_nt(qc, kb) - shift)
                out.append((l + jnp.sum(p, axis=1, keepdims=True), acc + _mm(p, vb)))
            return tuple(out)

        init = (jnp.zeros((lq, 1), F32), jnp.zeros((lq, LANES), F32))
        (l0, acc0), (l1, acc1) = over_keys(step, (init, init))
        finish(l0, acc0, l1, acc1)

    @pl.when(jnp.logical_not(fixed))
    def _():
        online()


def _diff_attn(qkv, lam, cache=None, bounds=None, lq=512, kc=2048):
    b, t, _ = qkv.shape
    nh = 4
    latent = cache is not None
    kc = min(kc, t)
    assert t % kc == 0 and t % lq == 0
    in_specs = [pl.BlockSpec(memory_space=pltpu.SMEM),
                pl.BlockSpec((1, lq, LANES), lambda i, h, j: (i, j, h)),
                pl.BlockSpec((1, t, LANES), lambda i, h, j: (i, 0, nh + h)),
                pl.BlockSpec((1, t, LANES), lambda i, h, j: (i, 0, 2 * nh + h))]
    args = [lam, qkv, qkv, qkv]
    if latent:
        tc = cache[0].shape[1]
        in_specs = [pl.BlockSpec(memory_space=pltpu.SMEM)] + in_specs
        in_specs += [pl.BlockSpec((1, tc, LANES), lambda i, h, j: (i, 0, h))] * 2
        args = [bounds] + args + list(cache)
    return pl.pallas_call(
        functools.partial(_diff_attn_kernel, latent=latent, kc=kc),
        grid=(b, nh, t // lq),
        in_specs=in_specs,
        out_specs=pl.BlockSpec((1, lq, LANES), lambda i, h, j: (i, j, h)),
        out_shape=jax.ShapeDtypeStruct((b, t, nh * LANES), F32),
        scratch_shapes=[pltpu.SMEM((1,), F32)] if latent else [],
        compiler_params=_params("parallel", "parallel", "arbitrary"),
        name="diff_attn_lat" if latent else "diff_attn_ctx",
    )(*args)


def _resid_ffn(x, mix, g1, n2g, sh2, sc2, g2, w1_ref, w3_ref, w2_ref):
    x1 = x + g1 * mix
    h = (_rms(x1) * n2g * (1.0 + sc2) + sh2).astype(BF16)
    nf = w1_ref.shape[1]
    step = nf // FFN_SPLIT
    f = None
    for c0 in range(0, nf, step):
        a = jnp.dot(h, w1_ref[:, c0:c0 + step], preferred_element_type=F32)
        b = jnp.dot(h, w3_ref[:, c0:c0 + step], preferred_element_type=F32)
        part = _mm(_silu(a) * b, w2_ref[c0:c0 + step, :])
        f = part if f is None else f + part
    return x1 + g2 * f


def _out_ab_kernel(x_ref, oa_ref, ob_ref, gb_ref, rg_ref, wo_ref, g1_ref, n2g_ref, sh2_ref, sc2_ref,
                   g2_ref, w1_ref, w3_ref, w2_ref, y_ref):
    bd = _head_ones()
    half = wo_ref.shape[0] // 2
    mix = _mm(oa_ref[0], wo_ref[:half, :])
    for j in range(half // LANES):
        cols = slice(LANES * j, LANES * (j + 1))
        ob = _headnorm64(ob_ref[0, :, cols], bd) * rg_ref[...] * _silu(gb_ref[0, :, cols])
        mix += _mm(ob, wo_ref[half + LANES * j:half + LANES * (j + 1), :])
    y_ref[0] = _resid_ffn(x_ref[0], mix, g1_ref[0], n2g_ref[...], sh2_ref[0], sc2_ref[0], g2_ref[0],
                          w1_ref, w3_ref, w2_ref)


def _out_cd_kernel(x_ref, yf_ref, yb_ref, u_ref, od_ref, dsk_ref, gw_ref, gbias_ref, sg_ref, wo_ref,
                   g1_ref, n2g_ref, sh2_ref, sc2_ref, g2_ref, w1_ref, w3_ref, w2_ref, y_ref, *, od_scale):
    half = wo_ref.shape[0] // 2
    y = yf_ref[0] + yb_ref[0] + dsk_ref[...] * u_ref[0]
    g = jax.nn.gelu(y)
    oc = g * jax.nn.sigmoid(_mm(g, gw_ref[...]) + gbias_ref[...])
    mix = _mm(oc, wo_ref[:half, :])
    for j in range(half // LANES):
        od = _rms(od_ref[0, :, LANES * j:LANES * (j + 1)]) * sg_ref[...] * od_scale
        mix += _mm(od, wo_ref[half + LANES * j:half + LANES * (j + 1), :])
    y_ref[0] = _resid_ffn(x_ref[0], mix, g1_ref[0], n2g_ref[...], sh2_ref[0], sc2_ref[0], g2_ref[0],
                          w1_ref, w3_ref, w2_ref)


def _const_spec(shape):
    nd = len(shape)
    return pl.BlockSpec(shape, lambda i, j: (0,) * nd, pipeline_mode=pl.Buffered(1))


def _out_call(kern, name, x, tok_args, const_args, mod, ffn, tm):
    b, t, d = x.shape
    g1, sh2, sc2, g2 = mod
    n2g, w1, w3, w2 = ffn

    def tok(a):
        return pl.BlockSpec((1, tm, a.shape[2]), lambda i, j: (i, j, 0))

    def per_batch(a):
        return pl.BlockSpec((1, 1, d), lambda i, j: (i, 0, 0))

    args = [x] + list(tok_args) + list(const_args) + [g1, n2g, sh2, sc2, g2, w1, w3, w2]
    in_specs = ([tok(x)] + [tok(a) for a in tok_args] + [_const_spec(a.shape) for a in const_args]
                + [per_batch(g1), _const_spec(n2g.shape), per_batch(sh2), per_batch(sc2), per_batch(g2),
                   _const_spec(w1.shape), _const_spec(w3.shape), _const_spec(w2.shape)])
    return pl.pallas_call(
        kern,
        grid=(b, t // tm),
        in_specs=in_specs,
        out_specs=tok(x),
        out_shape=jax.ShapeDtypeStruct(x.shape, F32),
        compiler_params=_params("parallel", "parallel"),
        name=name,
    )(*args)


def _ab_plan():
    plan = [(NORM_Q, 0, j) for j in range(4)] + [(NORM_K, 0, 4), (NORM_K, 0, 5), (RAW, 0, 6), (RAW, 0, 7)]
    plan += [(SCALE_Q, 1, j) for j in range(4)] + [(RAW, 1, 4 + j) for j in range(8)]
    plan += [(RAW, 2, j) for j in range(4)]
    return tuple(plan)


def _dup_kv_cols(w):
    q, kv, rest = w[:, :4 * LANES], w[:, 4 * LANES:6 * LANES], w[:, 6 * LANES:]
    kv = jnp.repeat(kv.reshape(w.shape[0], 4, 1, HEAD_DIM), 2, axis=2).reshape(w.shape[0], 4 * LANES)
    return jnp.concatenate([q, kv, rest], axis=1)


def _dup_heads(x):
    return jnp.repeat(x[..., :, None, :], 2, axis=-2).reshape(x.shape[:-2] + (4 * HEAD_DIM,))


def _cd_plan():
    plan = [(RAW, 0, j) for j in range(4)]
    plan += [(NORM_Q, 1, j) for j in range(4)] + [(NORM_K, 1, 4 + j) for j in range(4)]
    plan += [(RAW, 1, 8 + j) for j in range(4)]
    return tuple(plan)


AB_PLAN = _ab_plan()
CD_PLAN = _cd_plan()


def _rope_tables(t):
    rows = t // GRID_W
    r = jnp.repeat(jnp.arange(rows, dtype=F32), GRID_W)
    col = jnp.tile(jnp.arange(GRID_W, dtype=F32), rows)
    n_freq = HEAD_DIM // 4
    inv = ROPE_BASE ** (-jnp.arange(n_freq, dtype=F32) / n_freq)
    ang = jnp.concatenate([r[:, None] * inv, col[:, None] * inv], axis=-1)
    cos, sin = jnp.cos(ang), jnp.sin(ang)
    cos128 = jnp.tile(cos, (1, 4))
    sin128 = jnp.tile(jnp.concatenate([-sin, sin], axis=-1), (1, 2))
    return cos128, sin128


def _tile2(g):
    return jnp.tile(g.astype(F32), 2).reshape(1, LANES)


def _time_major(a, b, t):
    w = a.shape[-1]
    return a.reshape(b, t, w).transpose(1, 0, 2).reshape(t * b, w)


def _batch_major(a, b, t):
    w = a.shape[-1]
    return a.reshape(t, b, w).transpose(1, 0, 2)


def kernel(x_prompt, x_sample, c, c_ctx, cache_k_a, cache_v_a, state_ret, state_ssm_re, state_ssm_im,
           cache_k_d, cache_v_d, ada_w, ada_b, norm1_g, norm2_g, ffn_w1, ffn_w3, ffn_w2,
           ab_w_in, ab_w_out, a_q_norm, a_k_norm, a_sink, ret_decay, ret_norm,
           cd_w_in, cd_w_out, ssm_lambda_re, ssm_lambda_im, ssm_log_dt, ssm_b_re, ssm_b_im,
           ssm_c_re, ssm_c_im, ssm_d, ssm_glu_w, ssm_glu_b, d_q_norm, d_k_norm, d_lambda, d_subln):
    nbc, tcx, d = x_prompt.shape
    nbl, tl, _ = x_sample.shape
    depth = ada_w.shape[0]
    rope_tabs = _rope_tables(tl)

    ncond = -(-(nbl + 1) // SUBLANES) * SUBLANES
    cond = jnp.zeros((ncond, d), F32).at[:nbl].set(c).at[nbl].set(c_ctx)
    mods = _ada(cond, ada_w, ada_b)

    def mod_split(l, lo, hi):
        m = mods[l, lo:hi].reshape(hi - lo, 1, 6, d)
        return [m[:, :, k] for k in range(6)]

    yp = x_prompt.reshape(1, nbc * tcx, d)
    ys = x_sample
    new_ka, new_va, new_ret, new_sr, new_si, new_kd, new_vd = [], [], [], [], [], [], []
    for l in range(depth):
        j = l // 2
        ffn = (norm2_g[l].reshape(1, d), ffn_w1[l].astype(BF16), ffn_w3[l].astype(BF16), ffn_w2[l].astype(BF16))
        n1g = norm1_g[l].reshape(1, d)
        sh1c, sc1c, g1c, sh2c, sc2c, g2c = mod_split(l, nbl, nbl + 1)
        sh1l, sc1l, g1l, sh2l, sc2l, g2l = mod_split(l, 0, nbl)
        if l % 2 == 0:
            w_in = _dup_kv_cols(ab_w_in[j]).astype(BF16)
            w_out = ab_w_out[j].astype(BF16)
            gq, gk = _tile2(a_q_norm[j]), _tile2(a_k_norm[j])
            rg = _tile2(ret_norm[j])
            log_g = jax.nn.log_sigmoid(ret_decay[j].astype(F32))
            widths = (8 * LANES, 12 * LANES, 4 * LANES)
            a_scale = Q_SCALE
            qkv_a, qkv_b, gb = _inproj(yp, sh1c, sc1c, n1g, w_in, gq, gk, None, AB_PLAN, widths,
                                       (F32, F32, F32), a_scale, TOKEN_TILE, "inproj_ab_ctx")
            qkv_a = qkv_a.reshape(nbc, tcx, -1)
            bounds = _norm_bounds(a_q_norm[j], a_k_norm[j])
            oa = _attn_a_ctx(qkv_a, a_sink[j], bounds)
            ob, s_ret = _retention(qkv_b.reshape(nbc, tcx, -1), log_g, None)
            yp = _out_call(_out_ab_kernel, "out_ab_ctx", yp,
                           [oa.reshape(1, nbc * tcx, -1), ob.reshape(1, nbc * tcx, -1), gb],
                           [rg, w_out], (g1c, sh2c, sc2c, g2c), ffn, TOKEN_TILE)
            new_ka.append(qkv_a[:, :, 4 * LANES:6 * LANES].reshape(nbc, tcx, 2, 2, HEAD_DIM)[:, :, :, 0])
            new_va.append(qkv_a[:, :, 6 * LANES:8 * LANES].reshape(nbc, tcx, 2, 2, HEAD_DIM)[:, :, :, 0])
            s_ret = jnp.stack([s_ret[..., :HEAD_DIM, :HEAD_DIM], s_ret[..., HEAD_DIM:, HEAD_DIM:]], axis=2)
            new_ret.append(s_ret.transpose(0, 3, 1, 2, 4, 5).reshape(nbc, 2, 8, HEAD_DIM, HEAD_DIM))
            qkv_a, qkv_b, gb = _inproj(ys, sh1l, sc1l, n1g, w_in, gq, gk, rope_tabs, AB_PLAN, widths,
                                       (BF16, F32, F32), a_scale, TOKEN_TILE, "inproj_ab_lat")
            ck = _dup_heads(cache_k_a[:, j]).astype(BF16)
            cv = _dup_heads(cache_v_a[:, j]).astype(BF16)
            oa = _attn_a_lat(qkv_a, ck, cv, a_sink[j], bounds)
            s0 = state_ret[:, j].reshape(nbl, 2, 4, 2, HEAD_DIM, HEAD_DIM)
            z = jnp.zeros_like(s0[:, :, :, 0])
            s0 = jnp.concatenate([jnp.concatenate([s0[:, :, :, 0], z], axis=-1),
                                  jnp.concatenate([z, s0[:, :, :, 1]], axis=-1)], axis=-2)
            s0 = s0.transpose(0, 2, 1, 3, 4)
            ob, _ = _retention(qkv_b, log_g, s0)
            ys = _out_call(_out_ab_kernel, "out_ab_lat", ys, [oa, ob, gb], [rg, w_out],
                           (g1l, sh2l, sc2l, g2l), ffn, TOKEN_TILE)
        else:
            lam_init = 0.8 - 0.6 * math.exp(-0.3 * l)
            w_in = cd_w_in[j].astype(BF16)
            w_out = cd_w_out[j].astype(BF16)
            gq, gk = _tile2(d_q_norm[j]), _tile2(d_k_norm[j])
            lp = d_lambda[j].astype(F32)
            lam = (jnp.exp(jnp.sum(lp[0] * lp[1])) - jnp.exp(jnp.sum(lp[2] * lp[3])) + lam_init).reshape(1)
            prm = _s5_params(ssm_lambda_re[j], ssm_lambda_im[j], ssm_log_dt[j], ssm_b_re[j], ssm_b_im[j],
                             ssm_c_re[j], ssm_c_im[j])
            ns = prm[2].shape[2]
            consts = [ssm_d[j].reshape(1, -1), ssm_glu_w[j].astype(BF16), ssm_glu_b[j].reshape(1, -1),
                      d_subln[j].reshape(1, LANES), w_out]
            kern = functools.partial(_out_cd_kernel, od_scale=1.0 - lam_init)
            widths = (4 * LANES, 12 * LANES)
            d_scale = Q_SCALE
            u, qkv_d = _inproj(yp, sh1c, sc1c, n1g, w_in, gq, gk, None, CD_PLAN, widths,
                               (F32, F32), d_scale, TOKEN_TILE, "inproj_cd_ctx")
            h0 = jnp.zeros((2, nbc, 2 * ns), F32)
            y2, fin = _s5(_time_major(u, nbc, tcx), nbc, prm, h0)
            yf = _batch_major(y2[0], nbc, tcx).reshape(1, nbc * tcx, -1)
            yb = _batch_major(y2[1], nbc, tcx).reshape(1, nbc * tcx, -1)
            qkv_d = qkv_d.reshape(nbc, tcx, -1)
            od = _diff_attn(qkv_d, lam, None, lq=tcx)
            yp = _out_call(kern, "out_cd_ctx", yp, [yf, yb, u, od.reshape(1, nbc * tcx, -1)], consts,
                           (g1c, sh2c, sc2c, g2c), ffn, TOKEN_TILE)
            g = ns // C_STATE
            new_sr.append(fin[:, :, :ns].reshape(2, nbc, g, C_STATE).transpose(1, 0, 2, 3))
            new_si.append(fin[:, :, ns:].reshape(2, nbc, g, C_STATE).transpose(1, 0, 2, 3))
            new_kd.append(qkv_d[:, :, 4 * LANES:8 * LANES].reshape(nbc, tcx, 4, 2, HEAD_DIM))
            new_vd.append(qkv_d[:, :, 8 * LANES:].reshape(nbc, tcx, 4, 2 * HEAD_DIM))
            u, qkv_d = _inproj(ys, sh1l, sc1l, n1g, w_in, gq, gk, rope_tabs, CD_PLAN, widths,
                               (F32, BF16), d_scale, TOKEN_TILE, "inproj_cd_lat")
            h0 = jnp.concatenate([state_ssm_re[:, j].reshape(nbl, 2, ns), state_ssm_im[:, j].reshape(nbl, 2, ns)],
                                 axis=-1).transpose(1, 0, 2)
            y2, _ = _s5(_time_major(u, nbl, tl), nbl, prm, h0)
            yf = _batch_major(y2[0], nbl, tl)
            yb = _batch_major(y2[1], nbl, tl)
            ckd = cache_k_d[:, j].reshape(nbl, -1, 4 * LANES)
            cvd = cache_v_d[:, j].reshape(nbl, -1, 4 * LANES)
            od = _diff_attn(qkv_d, lam, (ckd, cvd), _norm_bounds(d_q_norm[j], d_k_norm[j]))
            ys = _out_call(kern, "out_cd_lat", ys, [yf, yb, u, od], consts, (g1l, sh2l, sc2l, g2l), ffn, TOKEN_TILE)
    return (yp.reshape(nbc, tcx, d), ys,
            jnp.stack(new_ka, axis=1), jnp.stack(new_va, axis=1), jnp.stack(new_ret, axis=1),
            jnp.stack(new_sr, axis=1), jnp.stack(new_si, axis=1),
            jnp.stack(new_kd, axis=1), jnp.stack(new_vd, axis=1))
```

```python
import functools
import math

import jax
import jax.numpy as jnp
from jax import lax
from jax.experimental import pallas as pl
from jax.experimental.pallas import tpu as pltpu

F32 = jnp.float32
BF16 = jnp.bfloat16

LANES = 128
SUBLANES = 8
HEAD_DIM = 64
GRID_W = 64
ROPE_BASE = 10000.0
WINDOW = 128
RET_CHUNK = 128
RET_UNROLL = 8
FFN_SPLIT = 2
TOKEN_TILE = 512
C_GROUP = 16
C_STATE = 64
NEG_INF = -1e30
EPS = 1e-6
LOG2E = math.log2(math.e)
Q_SCALE = HEAD_DIM ** -0.5 * LOG2E
SHIFT_LIMIT = 50.0
VMEM_LIMIT = 56 * 1024 * 1024


def _params(*sem):
    return pltpu.CompilerParams(dimension_semantics=sem, vmem_limit_bytes=VMEM_LIMIT)


def _mm(a, b):
    return jnp.dot(a.astype(BF16), b.astype(BF16), preferred_element_type=F32)


def _mm_nt(a, b):
    return lax.dot_general(a.astype(BF16), b.astype(BF16), (((1,), (1,)), ((), ())),
                           preferred_element_type=F32)


def _mm_tn(a, b):
    return lax.dot_general(a.astype(BF16), b.astype(BF16), (((0,), (0,)), ((), ())),
                           preferred_element_type=F32)


def _rms(x):
    return x * lax.rsqrt(jnp.mean(x * x, axis=-1, keepdims=True) + EPS)


def _silu(x):
    return x * jax.nn.sigmoid(x)


def _lane_lo():
    return lax.broadcasted_iota(jnp.int32, (1, LANES), 1) < HEAD_DIM


def _same_head():
    r = lax.broadcasted_iota(jnp.int32, (LANES, LANES), 0) < HEAD_DIM
    c = lax.broadcasted_iota(jnp.int32, (LANES, LANES), 1) < HEAD_DIM
    return r == c


def _head_ones():
    return _same_head().astype(BF16)


def _headnorm64(x, bd):
    sq = x * x
    hi = sq.astype(BF16)
    lo = (sq - hi.astype(F32)).astype(BF16)
    ss = (jnp.dot(hi, bd, preferred_element_type=F32) + jnp.dot(lo, bd, preferred_element_type=F32))
    return x * lax.rsqrt(ss * (1.0 / HEAD_DIM) + EPS)


def _max_sq_head_norm(x):
    bd = _head_ones()
    best = None
    for j in range(x.shape[1] // LANES):
        xf = x[:, LANES * j:LANES * (j + 1)].astype(F32)
        m = jnp.max(jnp.dot((xf * xf).astype(BF16), bd, preferred_element_type=F32))
        best = m if best is None else jnp.maximum(best, m)
    return best


def _norm_bounds(gq, gk):
    q_sq = HEAD_DIM * Q_SCALE * Q_SCALE * jnp.max(gq.astype(F32) ** 2)
    k_sq = HEAD_DIM * jnp.max(gk.astype(F32) ** 2)
    return jnp.stack([q_sq, k_sq])


def _rope128(x, cos, sin_signed):
    lane = lax.broadcasted_iota(jnp.int32, (1, LANES), 1)
    first = (lane & (HEAD_DIM // 2)) == 0
    partner = jnp.where(first, pltpu.roll(x, LANES - HEAD_DIM // 2, 1), pltpu.roll(x, HEAD_DIM // 2, 1))
    return x * cos + partner * sin_signed


def _ada_kernel(c_ref, w_ref, b_ref, o_ref):
    o_ref[0] = _mm(_silu(c_ref[...]), w_ref[0]) + b_ref[0]


def _ada(cond, ada_w, ada_b):
    depth, d, n = ada_w.shape
    tn = n // 4
    return pl.pallas_call(
        _ada_kernel,
        grid=(depth, n // tn),
        in_specs=[pl.BlockSpec(cond.shape, lambda l, j: (0, 0)),
                  pl.BlockSpec((1, d, tn), lambda l, j: (l, 0, j)),
                  pl.BlockSpec((1, 1, tn), lambda l, j: (l, 0, j))],
        out_specs=pl.BlockSpec((1, cond.shape[0], tn), lambda l, j: (l, 0, j)),
        out_shape=jax.ShapeDtypeStruct((depth, cond.shape[0], n), F32),
        compiler_params=_params("parallel", "parallel"),
        name="ada_modulation",
    )(cond, ada_w, ada_b.reshape(depth, 1, n))


RAW, NORM_Q, NORM_K, SCALE_Q = 0, 1, 2, 3


def _inproj_kernel(*refs, plan, rope, q_scale):
    x_ref, sh_ref, sc_ref, ng_ref, w_ref, gq_ref, gk_ref = refs[:7]
    pos = 7
    if rope:
        cos_ref, sin_ref = refs[7:9]
        pos = 9
    outs = refs[pos:]
    x = x_ref[0]
    h = _rms(x) * ng_ref[...] * (1.0 + sc_ref[0]) + sh_ref[0]
    p = _mm(h, w_ref[...])
    bd = _head_ones()
    for j, (op, oi, dj) in enumerate(plan):
        chunk = p[:, LANES * j:LANES * (j + 1)]
        if op in (NORM_Q, NORM_K):
            g = gq_ref[...] if op == NORM_Q else gk_ref[...]
            chunk = _headnorm64(chunk, bd) * g
            if rope:
                chunk = _rope128(chunk, cos_ref[...], sin_ref[...])
            if op == NORM_Q:
                chunk = chunk * q_scale
        elif op == SCALE_Q:
            chunk = chunk * (HEAD_DIM ** -0.5)
        outs[oi][0, :, LANES * dj:LANES * (dj + 1)] = chunk.astype(outs[oi].dtype)


def _inproj(x, sh, sc, ng, w, gq, gk, rope_tabs, plan, out_widths, out_dtypes, q_scale, tm, name):
    b, t, d = x.shape
    n = w.shape[1]
    rope = rope_tabs is not None
    in_specs = [pl.BlockSpec((1, tm, d), lambda i, j: (i, j, 0)),
                pl.BlockSpec((1, 1, d), lambda i, j: (i, 0, 0)),
                pl.BlockSpec((1, 1, d), lambda i, j: (i, 0, 0)),
                pl.BlockSpec((1, d), lambda i, j: (0, 0)),
                pl.BlockSpec((d, n), lambda i, j: (0, 0)),
                pl.BlockSpec((1, LANES), lambda i, j: (0, 0)),
                pl.BlockSpec((1, LANES), lambda i, j: (0, 0))]
    args = [x, sh, sc, ng, w, gq, gk]
    if rope:
        in_specs += [pl.BlockSpec((tm, LANES), lambda i, j: (j, 0))] * 2
        args += list(rope_tabs)
    return pl.pallas_call(
        functools.partial(_inproj_kernel, plan=plan, rope=rope, q_scale=q_scale),
        grid=(b, t // tm),
        in_specs=in_specs,
        out_specs=[pl.BlockSpec((1, tm, wd), lambda i, j: (i, j, 0)) for wd in out_widths],
        out_shape=[jax.ShapeDtypeStruct((b, t, wd), dt) for wd, dt in zip(out_widths, out_dtypes)],
        compiler_params=_params("parallel", "parallel"),
        name=name,
    )(*args)


def _attn_a_kernel(*refs, latent, lq):
    if latent:
        (bnd_ref, sink_ref, q_ref, kp_ref, kc_ref, kn_ref, vp_ref, vc_ref, vn_ref, ck_ref, cv_ref, o_ref,
         cn_ref) = refs
        i = pl.program_id(1)

        @pl.when(i == 0)
        def _():
            cn_ref[0] = _max_sq_head_norm(ck_ref[0])

        k_sq = jnp.maximum(bnd_ref[1], cn_ref[0])
        last = pl.num_programs(1) - 1
        k_all = jnp.concatenate([ck_ref[0], kp_ref[0], kc_ref[0], kn_ref[0]], axis=0)
        v_all = jnp.concatenate([cv_ref[0], vp_ref[0], vc_ref[0], vn_ref[0]], axis=0)
        tc = ck_ref.shape[1]
        assert lq & (lq - 1) == 0
        r = lax.broadcasted_iota(jnp.int32, (2 * lq, lq), 0) & (lq - 1)
        c = lax.broadcasted_iota(jnp.int32, (2 * lq, lq), 1)
        piece_masks = {tc: (c >= r) & (i > 0), tc + 2 * lq: (c <= r) & (i < last)}
    else:
        bnd_ref, sink_ref, q_ref, k_ref, v_ref, o_ref = refs
        k_all = k_ref[0]
        v_all = v_ref[0]
        piece_masks = {}
        k_sq = bnd_ref[1]
    nk = k_all.shape[0]
    lo = _lane_lo()
    row = lax.broadcasted_iota(jnp.int32, (2 * lq, 1), 0)

    def masked(x, fill):
        if not piece_masks:
            return x
        pieces = [jnp.where(piece_masks[o], x[:, o:o + lq], fill) if o in piece_masks else x[:, o:o + lq]
                  for o in range(0, nk, lq)]
        return jnp.concatenate(pieces, axis=1)

    sinks = [sink_ref[h] * LOG2E for h in range(8)]
    sink_max = functools.reduce(jnp.maximum, sinks)
    bound_sq = bnd_ref[0] * k_sq
    fixed = (bound_sq <= SHIFT_LIMIT * SHIFT_LIMIT) & (sink_max <= SHIFT_LIMIT)

    def attend(softmax):
        for g in range(2):
            k2 = k_all[:, LANES * g:LANES * (g + 1)].astype(BF16)
            v2 = v_all[:, LANES * g:LANES * (g + 1)].astype(BF16)
            for jp in range(2):
                j = 2 * g + jp
                q = q_ref[0, :, LANES * j:LANES * (j + 1)]
                q2 = jnp.concatenate([jnp.where(lo, q, 0.0), jnp.where(lo, 0.0, q)], axis=0)
                sk = jnp.where(row < lq, sinks[2 * j], sinks[2 * j + 1])
                p, den = softmax(_mm_nt(q2, k2), sk)
                o2 = _mm(p, v2) / den
                o_ref[0, :, LANES * j:LANES * (j + 1)] = jnp.where(lo, o2[:lq], o2[lq:])

    @pl.when(fixed)
    def _():
        shift = jnp.maximum(jnp.sqrt(jnp.full((1, 1), bound_sq, F32)), sink_max)

        def softmax(s, sk):
            p = masked(jnp.exp2(s - shift), 0.0)
            return p, jnp.sum(p, axis=1, keepdims=True) + jnp.exp2(sk - shift)

        attend(softmax)

    @pl.when(jnp.logical_not(fixed))
    def _():
        def softmax(s, sk):
            s = masked(s, NEG_INF)
            m = jnp.maximum(jnp.max(s, axis=1, keepdims=True), sk)
            p = jnp.exp2(s - m)
            return p, jnp.sum(p, axis=1, keepdims=True) + jnp.exp2(sk - m)

        attend(softmax)


def _attn_a_ctx(qkv, sink, bounds):
    b, t, _ = qkv.shape
    return pl.pallas_call(
        functools.partial(_attn_a_kernel, latent=False, lq=t),
        grid=(b,),
        in_specs=[pl.BlockSpec(memory_space=pltpu.SMEM),
                  pl.BlockSpec(memory_space=pltpu.SMEM),
                  pl.BlockSpec((1, t, 4 * LANES), lambda i: (i, 0, 0)),
                  pl.BlockSpec((1, t, 2 * LANES), lambda i: (i, 0, 2)),
                  pl.BlockSpec((1, t, 2 * LANES), lambda i: (i, 0, 3))],
        out_specs=pl.BlockSpec((1, t, 4 * LANES), lambda i: (i, 0, 0)),
        out_shape=jax.ShapeDtypeStruct((b, t, 4 * LANES), F32),
        compiler_params=_params("parallel"),
        name="attn_a_ctx",
    )(bounds, sink, qkv, qkv, qkv)


def _attn_a_lat(qkv, ck, cv, sink, bounds):
    b, t, _ = qkv.shape
    lq = WINDOW
    nb = t // lq
    tc = ck.shape[1]

    def kv_spec(col, off):
        return pl.BlockSpec((1, lq, 2 * LANES), lambda i, j: (i, jnp.clip(j + off, 0, nb - 1), col))

    return pl.pallas_call(
        functools.partial(_attn_a_kernel, latent=True, lq=lq),
        grid=(b, nb),
        in_specs=[pl.BlockSpec(memory_space=pltpu.SMEM),
                  pl.BlockSpec(memory_space=pltpu.SMEM),
                  pl.BlockSpec((1, lq, 4 * LANES), lambda i, j: (i, j, 0)),
                  kv_spec(2, -1), kv_spec(2, 0), kv_spec(2, 1),
                  kv_spec(3, -1), kv_spec(3, 0), kv_spec(3, 1),
                  pl.BlockSpec((1, tc, 2 * LANES), lambda i, j: (i, 0, 0)),
                  pl.BlockSpec((1, tc, 2 * LANES), lambda i, j: (i, 0, 0))],
        out_specs=pl.BlockSpec((1, lq, 4 * LANES), lambda i, j: (i, j, 0)),
        out_shape=jax.ShapeDtypeStruct((b, t, 4 * LANES), F32),
        scratch_shapes=[pltpu.SMEM((1,), F32)],
        compiler_params=_params("parallel", "arbitrary"),
        name="attn_a_lat",
    )(bounds, sink, qkv, qkv, qkv, qkv, qkv, qkv, qkv, ck, cv)


def _retention_kernel(*refs, zero_init):
    if zero_init:
        lg_ref, q_ref, k_ref, v_ref, o_ref, sfin_ref, sp_ref = refs
    else:
        lg_ref, q_ref, k_ref, v_ref, s0_ref, o_ref, sfin_ref, sp_ref = refs
    pr = pl.program_id(1)
    L = RET_CHUNK
    n = q_ref.shape[1] // L
    lo = _lane_lo()
    lgf = jnp.where(lo, lg_ref[0, 2 * pr], lg_ref[0, 2 * pr + 1])
    lgb = jnp.where(lo, lg_ref[1, 2 * pr], lg_ref[1, 2 * pr + 1])
    idx = lax.broadcasted_iota(jnp.int32, (L, 1), 0).astype(F32)
    qdf = jnp.exp(lgf * (idx + 1.0))
    kdf = jnp.exp(lgf * (L - 1.0 - idx))
    cdf = jnp.exp(lgf * float(L))
    qdb = jnp.exp(lgb * (L - idx))
    kdb = jnp.exp(lgb * idx)
    cdb = jnp.exp(lgb * float(L))
    diff = (lax.broadcasted_iota(jnp.int32, (L, L), 0) - lax.broadcasted_iota(jnp.int32, (L, L), 1)).astype(F32)

    def decay(h):
        f = jnp.exp(lg_ref[0, 2 * pr + h] * jnp.maximum(diff, 0.0))
        bwd = jnp.exp(lg_ref[1, 2 * pr + h] * jnp.maximum(-diff, 0.0))
        return jnp.where(diff > 0, f, jnp.where(diff < 0, bwd, 2.0))

    d0 = decay(0)
    d1 = decay(1)
    blk = _same_head()

    if zero_init:
        sf0 = jnp.zeros((LANES, LANES), F32)
        sb0 = jnp.zeros((LANES, LANES), F32)
    else:
        sf0 = s0_ref[0, 0, 0]
        sb0 = s0_ref[0, 0, 1]

    qd = jnp.concatenate([qdf, qdb], axis=1)
    kd = jnp.concatenate([kdf, kdb], axis=1)
    blk2 = jnp.concatenate([blk, blk], axis=0)
    unroll = math.gcd(n, RET_UNROLL)

    def chunk(c):
        return pl.ds(pl.multiple_of(c * L, L), L)

    def kv_body(c, carry):
        k = k_ref[0, chunk(c), :]
        kv = _mm_tn(jnp.concatenate([k, k], axis=1) * kd, v_ref[0, chunk(c), :])
        sp_ref[c] = jnp.where(blk2, kv, 0.0)
        return carry

    lax.fori_loop(0, n, kv_body, 0, unroll=unroll)

    def f_body(c, sf):
        kv = sp_ref[c, :LANES, :]
        sp_ref[c, :LANES, :] = sf
        return sf * cdf + kv

    def b_body(i, sb):
        c = n - 1 - i
        kv = sp_ref[c, LANES:, :]
        sp_ref[c, LANES:, :] = sb
        return sb * cdb + kv

    sfin_ref[0, 0, 0] = lax.fori_loop(0, n, f_body, sf0)
    sfin_ref[0, 0, 1] = lax.fori_loop(0, n, b_body, sb0)

    def o_body(c, carry):
        q = q_ref[0, chunk(c), :]
        v = v_ref[0, chunk(c), :]
        s = _mm_nt(jnp.concatenate([jnp.where(lo, q, 0.0), jnp.where(lo, 0.0, q)], axis=0), k_ref[0, chunk(c), :])
        sd = jnp.concatenate([s[:L] * d0, s[L:] * d1], axis=1)
        v2 = jnp.concatenate([jnp.where(lo, v, 0.0), jnp.where(lo, 0.0, v)], axis=0)
        o_ref[0, chunk(c), :] = _mm(sd, v2) + _mm(jnp.concatenate([q, q], axis=1) * qd, sp_ref[c])
        return carry

    lax.fori_loop(0, n, o_body, 0, unroll=unroll)


def _retention(qkv, log_g, s0):
    b, t, _ = qkv.shape
    npair = 4
    in_specs = [pl.BlockSpec(memory_space=pltpu.SMEM),
                pl.BlockSpec((1, t, LANES), lambda i, p: (i, 0, p)),
                pl.BlockSpec((1, t, LANES), lambda i, p: (i, 0, npair + p)),
                pl.BlockSpec((1, t, LANES), lambda i, p: (i, 0, 2 * npair + p))]
    args = [log_g, qkv, qkv, qkv]
    if s0 is not None:
        in_specs.append(pl.BlockSpec((1, 1, 2, LANES, LANES), lambda i, p: (i, p, 0, 0, 0)))
        args.append(s0)
    return pl.pallas_call(
        functools.partial(_retention_kernel, zero_init=s0 is None),
        grid=(b, npair),
        in_specs=in_specs,
        out_specs=[pl.BlockSpec((1, t, LANES), lambda i, p: (i, 0, p)),
                   pl.BlockSpec((1, 1, 2, LANES, LANES), lambda i, p: (i, p, 0, 0, 0))],
        out_shape=[jax.ShapeDtypeStruct((b, t, npair * LANES), F32),
                   jax.ShapeDtypeStruct((b, npair, 2, LANES, LANES), F32)],
        scratch_shapes=[pltpu.VMEM((t // RET_CHUNK, 2 * LANES, LANES), F32)],
        compiler_params=_params("parallel", "parallel"),
        name="retention",
    )(*args)


S5_CHUNK = 16
S5_GROUPS = 4


def _s5_kernel(u_ref, m_ref, w_ref, v_ref, a_ref, h0_ref, y_ref, fin_ref, z_ref, sf_ref, sb_ref, *, nb):
    rows = u_ref.shape[1]
    n = rows // nb
    lo = _lane_lo()
    nbg = nb // SUBLANES
    for gi in range(u_ref.shape[0]):
        u = u_ref[gi]
        z_ref[...] = jnp.dot(u, w_ref[gi], preferred_element_type=F32)
        are = jnp.broadcast_to(a_ref[gi, 0:1, :], (SUBLANES, LANES))
        aim = jnp.broadcast_to(a_ref[gi, 1:2, :], (SUBLANES, LANES))

        def step(i, carry):
            out = []
            for bg in range(nbg):
                hre, him = carry[bg]
                up = pl.ds(pl.multiple_of(i * nb + bg * SUBLANES, SUBLANES), SUBLANES)
                dn = pl.ds(pl.multiple_of((n - 1 - i) * nb + bg * SUBLANES, SUBLANES), SUBLANES)
                sf_ref[up, :LANES] = hre
                sf_ref[up, LANES:] = him
                sb_ref[dn, :LANES] = hre
                sb_ref[dn, LANES:] = him
                zre = jnp.where(lo, z_ref[up, :LANES], z_ref[dn, :LANES])
                zim = jnp.where(lo, z_ref[up, LANES:], z_ref[dn, LANES:])
                out.append((are * hre - aim * him + zre, are * him + aim * hre + zim))
            return tuple(out)

        init = tuple((h0_ref[gi, bg * SUBLANES:(bg + 1) * SUBLANES, :LANES],
                      h0_ref[gi, bg * SUBLANES:(bg + 1) * SUBLANES, LANES:]) for bg in range(nbg))
        last = lax.fori_loop(0, n, step, init, unroll=2)
        for bg in range(nbg):
            fin_ref[gi, bg * SUBLANES:(bg + 1) * SUBLANES, :LANES] = last[bg][0]
            fin_ref[gi, bg * SUBLANES:(bg + 1) * SUBLANES, LANES:] = last[bg][1]
        y_ref[gi] = (jnp.dot(u, m_ref[gi], preferred_element_type=F32)
                     + _mm(sf_ref[...], v_ref[gi, 0]) + _mm(sb_ref[...], v_ref[gi, 1]))


def _s5(u, prm, h0_re, h0_im):
    m, w, v, a = prm
    b, t, cw = u.shape
    g = cw // C_GROUP
    n = t // S5_CHUNK
    rows = n * b
    width = S5_CHUNK * C_GROUP
    ug = u.astype(BF16).reshape(b, n, S5_CHUNK, g, C_GROUP).transpose(3, 1, 0, 2, 4).reshape(g, rows, width)
    if h0_re is None:
        h0 = jnp.zeros((g, b, 4 * C_STATE), F32)
    else:
        h0 = jnp.concatenate([h0_re[:, 0], h0_re[:, 1], h0_im[:, 0], h0_im[:, 1]], axis=-1).transpose(1, 0, 2)
    gs = S5_GROUPS

    def grp(shape):
        nd = len(shape)
        return pl.BlockSpec((gs,) + shape[1:], lambda i: (i,) + (0,) * (nd - 1))

    y, fin = pl.pallas_call(
        functools.partial(_s5_kernel, nb=b),
        grid=(g // gs,),
        in_specs=[grp(ug.shape), grp(m.shape), grp(w.shape), grp(v.shape), grp(a.shape), grp(h0.shape)],
        out_specs=[grp(ug.shape), grp(h0.shape)],
        out_shape=[jax.ShapeDtypeStruct(ug.shape, F32), jax.ShapeDtypeStruct(h0.shape, F32)],
        scratch_shapes=[pltpu.VMEM((rows, width), F32)] * 3,
        compiler_params=_params("parallel"),
        name="s5_scan",
    )(ug, m, w, v, a, h0)
    y = y.reshape(g, n, b, S5_CHUNK, C_GROUP).transpose(2, 1, 3, 0, 4).reshape(b, t, cw)
    fin = fin.reshape(g, b, 2, 2, C_STATE).transpose(2, 1, 3, 0, 4)
    return y, fin[0], fin[1]


def _s5_params(lam_re, lam_im, log_dt, b_re, b_im, c_re, c_im):
    dt = jnp.exp(log_dt)[..., None]
    mag = jnp.exp(lam_re * dt)
    a_re = mag * jnp.cos(lam_im * dt)
    a_im = mag * jnp.sin(lam_im * dt)
    den = lam_re * lam_re + lam_im * lam_im
    f_re = ((a_re - 1.0) * lam_re + a_im * lam_im) / den
    f_im = (a_im * lam_re - (a_re - 1.0) * lam_im) / den
    fb_re = f_re[..., None] * b_re[None] - f_im[..., None] * b_im[None]
    fb_im = f_re[..., None] * b_im[None] + f_im[..., None] * b_re[None]
    g = lam_re.shape[1]
    L = S5_CHUNK
    k = jnp.arange(L + 1, dtype=F32)[:, None, None, None]
    pmag = jnp.exp(lam_re * dt * k)
    p_re = pmag * jnp.cos(lam_im * dt * k)
    p_im = pmag * jnp.sin(lam_im * dt * k)
    hp = lax.Precision.HIGHEST

    ca_re = c_re[None, None] * p_re[:, :, :, None, :] - c_im[None, None] * p_im[:, :, :, None, :]
    ca_im = c_re[None, None] * p_im[:, :, :, None, :] + c_im[None, None] * p_re[:, :, :, None, :]
    kern = (jnp.einsum('kdgcp,dgpe->kdgce', ca_re, fb_re, precision=hp)
            - jnp.einsum('kdgcp,dgpe->kdgce', ca_im, fb_im, precision=hp))
    sig = jnp.arange(L)[:, None]
    tau = jnp.arange(L)[None, :]
    kf = jnp.where((tau >= sig)[:, :, None, None, None], kern[jnp.clip(tau - sig, 0, L), 0], 0.0)
    kb = jnp.where((sig >= tau)[:, :, None, None, None], kern[jnp.clip(sig - tau, 0, L), 1], 0.0)
    m = (kf + kb).transpose(2, 0, 4, 1, 3).reshape(g, L * C_GROUP, L * C_GROUP)

    def to_state(pw_re, pw_im, d):
        re = pw_re[..., None] * fb_re[d][None] - pw_im[..., None] * fb_im[d][None]
        im = pw_re[..., None] * fb_im[d][None] + pw_im[..., None] * fb_re[d][None]
        return re, im

    wf_re, wf_im = to_state(p_re[L - 1::-1, 0][:L], p_im[L - 1::-1, 0][:L], 0)
    wb_re, wb_im = to_state(p_re[:L, 1], p_im[:L, 1], 1)
    w = jnp.concatenate([wf_re, wb_re, wf_im, wb_im], axis=2)
    w = w.transpose(1, 0, 3, 2).reshape(g, L * C_GROUP, 4 * C_STATE)

    def from_state(car, cai):
        return car.transpose(1, 3, 0, 2), -cai.transpose(1, 3, 0, 2)

    vf_re, vf_im = from_state(ca_re[1:, 0], ca_im[1:, 0])
    vb_re, vb_im = from_state(ca_re[L:0:-1, 1], ca_im[L:0:-1, 1])
    zero = jnp.zeros_like(vf_re)
    v = jnp.stack([jnp.concatenate([vf_re, zero, vf_im, zero], axis=1),
                   jnp.concatenate([zero, vb_re, zero, vb_im], axis=1)], axis=1)
    v = v.reshape(g, 2, 4 * C_STATE, L * C_GROUP)

    a = jnp.stack([jnp.concatenate([p_re[L, 0], p_re[L, 1]], axis=-1),
                   jnp.concatenate([p_im[L, 0], p_im[L, 1]], axis=-1)], axis=1)
    return m.astype(BF16), w.astype(BF16), v.astype(BF16), a


def _diff_attn_kernel(*refs, latent, kc):
    if latent:
        bnd_ref, lam_ref, q_ref, k_ref, v_ref, ck_ref, cv_ref, o_ref, cn_ref = refs
    else:
        lam_ref, q_ref, k_ref, v_ref, o_ref = refs
    lq = q_ref.shape[1]
    lo = _lane_lo()
    q = q_ref[0]
    qs = (jnp.where(lo, q, 0.0).astype(BF16), jnp.where(lo, 0.0, q).astype(BF16))

    def over_keys(step, carry):
        if latent:
            carry = step(ck_ref[0], cv_ref[0], carry)

            def body(c, carry):
                rows = pl.ds(pl.multiple_of(c * kc, kc), kc)
                return step(k_ref[0, rows, :], v_ref[0, rows, :], carry)

            return lax.fori_loop(0, k_ref.shape[1] // kc, body, carry)
        return step(k_ref[0], v_ref[0], carry)

    def finish(l0, acc0, l1, acc1):
        o_ref[0] = acc0 / l0 - lam_ref[0] * (acc1 / l1)

    def online_step(kb, vb, carry):
        kb = kb.astype(BF16)
        vb = vb.astype(BF16)
        out = []
        for qc, (m, l, acc) in zip(qs, carry):
            s = _mm_nt(qc, kb)
            m_new = jnp.maximum(m, jnp.max(s, axis=1, keepdims=True))
            alpha = jnp.exp2(m - m_new)
            p = jnp.exp2(s - m_new)
            l = alpha * l + jnp.sum(p, axis=1, keepdims=True)
            acc = alpha * acc + _mm(p, vb)
            out.append((m_new, l, acc))
        return tuple(out)

    def online():
        init = (jnp.full((lq, 1), NEG_INF, F32), jnp.zeros((lq, 1), F32), jnp.zeros((lq, LANES), F32))
        (_, l0, acc0), (_, l1, acc1) = over_keys(online_step, (init, init))
        finish(l0, acc0, l1, acc1)

    if not latent:
        online()
        return

    @pl.when(pl.program_id(2) == 0)
    def _():
        cn_ref[0] = _max_sq_head_norm(ck_ref[0])

    bound_sq = bnd_ref[0] * jnp.maximum(bnd_ref[1], cn_ref[0])
    fixed = bound_sq <= SHIFT_LIMIT * SHIFT_LIMIT

    @pl.when(fixed)
    def _():
        shift = jnp.sqrt(jnp.full((1, 1), bound_sq, F32))

        def step(kb, vb, carry):
            kb = kb.astype(BF16)
            vb = vb.astype(BF16)
            out = []
            for qc, (l, acc) in zip(qs, carry):
                p = jnp.exp2(_mm_nt(qc, kb) - shift)
                out.append((l + jnp.sum(p, axis=1, keepdims=True), acc + _mm(p, vb)))
            return tuple(out)

        init = (jnp.zeros((lq, 1), F32), jnp.zeros((lq, LANES), F32))
        (l0, acc0), (l1, acc1) = over_keys(step, (init, init))
        finish(l0, acc0, l1, acc1)

    @pl.when(jnp.logical_not(fixed))
    def _():
        online()


def _diff_attn(qkv, lam, cache=None, bounds=None, lq=512, kc=2048):
    b, t, _ = qkv.shape
    nh = 4
    latent = cache is not None
    kc = min(kc, t)
    assert t % kc == 0 and t % lq == 0
    in_specs = [pl.BlockSpec(memory_space=pltpu.SMEM),
                pl.BlockSpec((1, lq, LANES), lambda i, h, j: (i, j, h)),
                pl.BlockSpec((1, t, LANES), lambda i, h, j: (i, 0, nh + h)),
                pl.BlockSpec((1, t, LANES), lambda i, h, j: (i, 0, 2 * nh + h))]
    args = [lam, qkv, qkv, qkv]
    if latent:
        tc = cache[0].shape[1]
        in_specs = [pl.BlockSpec(memory_space=pltpu.SMEM)] + in_specs
        in_specs += [pl.BlockSpec((1, tc, LANES), lambda i, h, j: (i, 0, h))] * 2
        args = [bounds] + args + list(cache)
    return pl.pallas_call(
        functools.partial(_diff_attn_kernel, latent=latent, kc=kc),
        grid=(b, nh, t // lq),
        in_specs=in_specs,
        out_specs=pl.BlockSpec((1, lq, LANES), lambda i, h, j: (i, j, h)),
        out_shape=jax.ShapeDtypeStruct((b, t, nh * LANES), F32),
        scratch_shapes=[pltpu.SMEM((1,), F32)] if latent else [],
        compiler_params=_params("parallel", "parallel", "arbitrary"),
        name="diff_attn_lat" if latent else "diff_attn_ctx",
    )(*args)


def _resid_ffn(x, mix, g1, n2g, sh2, sc2, g2, w1_ref, w3_ref, w2_ref):
    x1 = x + g1 * mix
    h = (_rms(x1) * n2g * (1.0 + sc2) + sh2).astype(BF16)
    nf = w1_ref.shape[1]
    step = nf // FFN_SPLIT
    f = None
    for c0 in range(0, nf, step):
        a = jnp.dot(h, w1_ref[:, c0:c0 + step], preferred_element_type=F32)
        b = jnp.dot(h, w3_ref[:, c0:c0 + step], preferred_element_type=F32)
        part = _mm(_silu(a) * b, w2_ref[c0:c0 + step, :])
        f = part if f is None else f + part
    return x1 + g2 * f


def _out_ab_kernel(x_ref, oa_ref, ob_ref, gb_ref, rg_ref, wo_ref, g1_ref, n2g_ref, sh2_ref, sc2_ref,
                   g2_ref, w1_ref, w3_ref, w2_ref, y_ref):
    bd = _head_ones()
    half = wo_ref.shape[0] // 2
    mix = _mm(oa_ref[0], wo_ref[:half, :])
    for j in range(half // LANES):
        cols = slice(LANES * j, LANES * (j + 1))
        ob = _headnorm64(ob_ref[0, :, cols], bd) * rg_ref[...] * _silu(gb_ref[0, :, cols])
        mix += _mm(ob, wo_ref[half + LANES * j:half + LANES * (j + 1), :])
    y_ref[0] = _resid_ffn(x_ref[0], mix, g1_ref[0], n2g_ref[...], sh2_ref[0], sc2_ref[0], g2_ref[0],
                          w1_ref, w3_ref, w2_ref)


def _out_cd_kernel(x_ref, ys_ref, u_ref, od_ref, dsk_ref, gw_ref, gbias_ref, sg_ref, wo_ref,
                   g1_ref, n2g_ref, sh2_ref, sc2_ref, g2_ref, w1_ref, w3_ref, w2_ref, y_ref, *, od_scale):
    half = wo_ref.shape[0] // 2
    y = ys_ref[0] + dsk_ref[...] * u_ref[0]
    g = jax.nn.gelu(y)
    oc = g * jax.nn.sigmoid(_mm(g, gw_ref[...]) + gbias_ref[...])
    mix = _mm(oc, wo_ref[:half, :])
    for j in range(half // LANES):
        od = _rms(od_ref[0, :, LANES * j:LANES * (j + 1)]) * sg_ref[...] * od_scale
        mix += _mm(od, wo_ref[half + LANES * j:half + LANES * (j + 1), :])
    y_ref[0] = _resid_ffn(x_ref[0], mix, g1_ref[0], n2g_ref[...], sh2_ref[0], sc2_ref[0], g2_ref[0],
                          w1_ref, w3_ref, w2_ref)


def _const_spec(shape):
    nd = len(shape)
    return pl.BlockSpec(shape, lambda i, j: (0,) * nd, pipeline_mode=pl.Buffered(1))


def _out_call(kern, name, x, tok_args, const_args, mod, ffn, tm):
    b, t, d = x.shape
    g1, sh2, sc2, g2 = mod
    n2g, w1, w3, w2 = ffn

    def tok(a):
        return pl.BlockSpec((1, tm, a.shape[2]), lambda i, j: (i, j, 0))

    def per_batch(a):
        return pl.BlockSpec((1, 1, d), lambda i, j: (i, 0, 0))

    args = [x] + list(tok_args) + list(const_args) + [g1, n2g, sh2, sc2, g2, w1, w3, w2]
    in_specs = ([tok(x)] + [tok(a) for a in tok_args] + [_const_spec(a.shape) for a in const_args]
                + [per_batch(g1), _const_spec(n2g.shape), per_batch(sh2), per_batch(sc2), per_batch(g2),
                   _const_spec(w1.shape), _const_spec(w3.shape), _const_spec(w2.shape)])
    return pl.pallas_call(
        kern,
        grid=(b, t // tm),
        in_specs=in_specs,
        out_specs=tok(x),
        out_shape=jax.ShapeDtypeStruct(x.shape, F32),
        compiler_params=_params("parallel", "parallel"),
        name=name,
    )(*args)


def _ab_plan():
    plan = [(NORM_Q, 0, j) for j in range(4)] + [(NORM_K, 0, 4), (NORM_K, 0, 5), (RAW, 0, 6), (RAW, 0, 7)]
    plan += [(SCALE_Q, 1, j) for j in range(4)] + [(RAW, 1, 4 + j) for j in range(8)]
    plan += [(RAW, 2, j) for j in range(4)]
    return tuple(plan)


def _dup_kv_cols(w):
    q, kv, rest = w[:, :4 * LANES], w[:, 4 * LANES:6 * LANES], w[:, 6 * LANES:]
    kv = jnp.repeat(kv.reshape(w.shape[0], 4, 1, HEAD_DIM), 2, axis=2).reshape(w.shape[0], 4 * LANES)
    return jnp.concatenate([q, kv, rest], axis=1)


def _dup_heads(x):
    return jnp.repeat(x[..., :, None, :], 2, axis=-2).reshape(x.shape[:-2] + (4 * HEAD_DIM,))


def _cd_plan():
    plan = [(RAW, 0, j) for j in range(4)]
    plan += [(NORM_Q, 1, j) for j in range(4)] + [(NORM_K, 1, 4 + j) for j in range(4)]
    plan += [(RAW, 1, 8 + j) for j in range(4)]
    return tuple(plan)


AB_PLAN = _ab_plan()
CD_PLAN = _cd_plan()


def _rope_tables(t):
    rows = t // GRID_W
    r = jnp.repeat(jnp.arange(rows, dtype=F32), GRID_W)
    col = jnp.tile(jnp.arange(GRID_W, dtype=F32), rows)
    n_freq = HEAD_DIM // 4
    inv = ROPE_BASE ** (-jnp.arange(n_freq, dtype=F32) / n_freq)
    ang = jnp.concatenate([r[:, None] * inv, col[:, None] * inv], axis=-1)
    cos, sin = jnp.cos(ang), jnp.sin(ang)
    cos128 = jnp.tile(cos, (1, 4))
    sin128 = jnp.tile(jnp.concatenate([-sin, sin], axis=-1), (1, 2))
    return cos128, sin128


def _tile2(g):
    return jnp.tile(g.astype(F32), 2).reshape(1, LANES)


def kernel(x_prompt, x_sample, c, c_ctx, cache_k_a, cache_v_a, state_ret, state_ssm_re, state_ssm_im,
           cache_k_d, cache_v_d, ada_w, ada_b, norm1_g, norm2_g, ffn_w1, ffn_w3, ffn_w2,
           ab_w_in, ab_w_out, a_q_norm, a_k_norm, a_sink, ret_decay, ret_norm,
           cd_w_in, cd_w_out, ssm_lambda_re, ssm_lambda_im, ssm_log_dt, ssm_b_re, ssm_b_im,
           ssm_c_re, ssm_c_im, ssm_d, ssm_glu_w, ssm_glu_b, d_q_norm, d_k_norm, d_lambda, d_subln):
    nbc, tcx, d = x_prompt.shape
    nbl, tl, _ = x_sample.shape
    depth = ada_w.shape[0]
    rope_tabs = _rope_tables(tl)

    ncond = -(-(nbl + 1) // SUBLANES) * SUBLANES
    cond = jnp.zeros((ncond, d), F32).at[:nbl].set(c).at[nbl].set(c_ctx)
    mods = _ada(cond, ada_w, ada_b)

    def mod_split(l, lo, hi):
        m = mods[l, lo:hi].reshape(hi - lo, 1, 6, d)
        return [m[:, :, k] for k in range(6)]

    yp = x_prompt.reshape(1, nbc * tcx, d)
    ys = x_sample
    new_ka, new_va, new_ret, new_sr, new_si, new_kd, new_vd = [], [], [], [], [], [], []
    for l in range(depth):
        j = l // 2
        ffn = (norm2_g[l].reshape(1, d), ffn_w1[l].astype(BF16), ffn_w3[l].astype(BF16), ffn_w2[l].astype(BF16))
        n1g = norm1_g[l].reshape(1, d)
        sh1c, sc1c, g1c, sh2c, sc2c, g2c = mod_split(l, nbl, nbl + 1)
        sh1l, sc1l, g1l, sh2l, sc2l, g2l = mod_split(l, 0, nbl)
        if l % 2 == 0:
            w_in = _dup_kv_cols(ab_w_in[j]).astype(BF16)
            w_out = ab_w_out[j].astype(BF16)
            gq, gk = _tile2(a_q_norm[j]), _tile2(a_k_norm[j])
            rg = _tile2(ret_norm[j])
            log_g = jax.nn.log_sigmoid(ret_decay[j].astype(F32))
            widths = (8 * LANES, 12 * LANES, 4 * LANES)
            a_scale = Q_SCALE
            qkv_a, qkv_b, gb = _inproj(yp, sh1c, sc1c, n1g, w_in, gq, gk, None, AB_PLAN, widths,
                                       (F32, F32, F32), a_scale, TOKEN_TILE, "inproj_ab_ctx")
            qkv_a = qkv_a.reshape(nbc, tcx, -1)
            bounds = _norm_bounds(a_q_norm[j], a_k_norm[j])
            oa = _attn_a_ctx(qkv_a, a_sink[j], bounds)
            ob, s_ret = _retention(qkv_b.reshape(nbc, tcx, -1), log_g, None)
            yp = _out_call(_out_ab_kernel, "out_ab_ctx", yp,
                           [oa.reshape(1, nbc * tcx, -1), ob.reshape(1, nbc * tcx, -1), gb],
                           [rg, w_out], (g1c, sh2c, sc2c, g2c), ffn, TOKEN_TILE)
            new_ka.append(qkv_a[:, :, 4 * LANES:6 * LANES].reshape(nbc, tcx, 2, 2, HEAD_DIM)[:, :, :, 0])
            new_va.append(qkv_a[:, :, 6 * LANES:8 * LANES].reshape(nbc, tcx, 2, 2, HEAD_DIM)[:, :, :, 0])
            s_ret = jnp.stack([s_ret[..., :HEAD_DIM, :HEAD_DIM], s_ret[..., HEAD_DIM:, HEAD_DIM:]], axis=2)
            new_ret.append(s_ret.transpose(0, 3, 1, 2, 4, 5).reshape(nbc, 2, 8, HEAD_DIM, HEAD_DIM))
            qkv_a, qkv_b, gb = _inproj(ys, sh1l, sc1l, n1g, w_in, gq, gk, rope_tabs, AB_PLAN, widths,
                                       (BF16, F32, F32), a_scale, TOKEN_TILE, "inproj_ab_lat")
            ck = _dup_heads(cache_k_a[:, j]).astype(BF16)
            cv = _dup_heads(cache_v_a[:, j]).astype(BF16)
            oa = _attn_a_lat(qkv_a, ck, cv, a_sink[j], bounds)
            s0 = state_ret[:, j].reshape(nbl, 2, 4, 2, HEAD_DIM, HEAD_DIM)
            z = jnp.zeros_like(s0[:, :, :, 0])
            s0 = jnp.concatenate([jnp.concatenate([s0[:, :, :, 0], z], axis=-1),
                                  jnp.concatenate([z, s0[:, :, :, 1]], axis=-1)], axis=-2)
            s0 = s0.transpose(0, 2, 1, 3, 4)
            ob, _ = _retention(qkv_b, log_g, s0)
            ys = _out_call(_out_ab_kernel, "out_ab_lat", ys, [oa, ob, gb], [rg, w_out],
                           (g1l, sh2l, sc2l, g2l), ffn, TOKEN_TILE)
        else:
            lam_init = 0.8 - 0.6 * math.exp(-0.3 * l)
            w_in = cd_w_in[j].astype(BF16)
            w_out = cd_w_out[j].astype(BF16)
            gq, gk = _tile2(d_q_norm[j]), _tile2(d_k_norm[j])
            lp = d_lambda[j].astype(F32)
            lam = (jnp.exp(jnp.sum(lp[0] * lp[1])) - jnp.exp(jnp.sum(lp[2] * lp[3])) + lam_init).reshape(1)
            prm = _s5_params(ssm_lambda_re[j], ssm_lambda_im[j], ssm_log_dt[j], ssm_b_re[j], ssm_b_im[j],
                             ssm_c_re[j], ssm_c_im[j])
            consts =[ssm_d[j].reshape(1, -1), ssm_glu_w[j].astype(BF16), ssm_glu_b[j].reshape(1, -1),
                      d_subln[j].reshape(1, LANES), w_out]
            kern = functools.partial(_out_cd_kernel, od_scale=1.0 - lam_init)
            widths = (4 * LANES, 12 * LANES)
            d_scale = Q_SCALE
            u, qkv_d = _inproj(yp, sh1c, sc1c, n1g, w_in, gq, gk, None, CD_PLAN, widths,
                               (F32, F32), d_scale, TOKEN_TILE, "inproj_cd_ctx")
            y, fin_re, fin_im = _s5(u.reshape(nbc, tcx, -1), prm, None, None)
            qkv_d = qkv_d.reshape(nbc, tcx, -1)
            od = _diff_attn(qkv_d, lam, None, lq=tcx)
            yp = _out_call(kern, "out_cd_ctx", yp, [y.reshape(1, nbc * tcx, -1), u, od.reshape(1, nbc * tcx, -1)],
                           consts, (g1c, sh2c, sc2c, g2c), ffn, TOKEN_TILE)
            new_sr.append(fin_re)
            new_si.append(fin_im)
            new_kd.append(qkv_d[:, :, 4 * LANES:8 * LANES].reshape(nbc, tcx, 4, 2, HEAD_DIM))
            new_vd.append(qkv_d[:, :, 8 * LANES:].reshape(nbc, tcx, 4, 2 * HEAD_DIM))
            u, qkv_d = _inproj(ys, sh1l, sc1l, n1g, w_in, gq, gk, rope_tabs, CD_PLAN, widths,
                               (F32, BF16), d_scale, TOKEN_TILE, "inproj_cd_lat")
            y, _, _ = _s5(u, prm, state_ssm_re[:, j], state_ssm_im[:, j])
            ckd = cache_k_d[:, j].reshape(nbl, -1, 4 * LANES)
            cvd = cache_v_d[:, j].reshape(nbl, -1, 4 * LANES)
            od = _diff_attn(qkv_d, lam, (ckd, cvd), _norm_bounds(d_q_norm[j], d_k_norm[j]))
            ys = _out_call(kern, "out_cd_lat", ys, [y, u, od], consts, (g1l, sh2l, sc2l, g2l), ffn, TOKEN_TILE)
    return (yp.reshape(nbc, tcx, d), ys,
            jnp.stack(new_ka, axis=1), jnp.stack(new_va, axis=1), jnp.stack(new_ret, axis=1),
            jnp.stack(new_sr, axis=1), jnp.stack(new_si, axis=1),
            jnp.stack(new_kd, axis=1), jnp.stack(new_vd, axis=1))
```

```python
import functools
import math

import jax
import jax.numpy as jnp
from jax import lax
from jax.experimental import pallas as pl
from jax.experimental.pallas import tpu as pltpu

F32 = jnp.float32
BF16 = jnp.bfloat16

LANES = 128
SUBLANES = 8
HEAD_DIM = 64
GRID_W = 64
ROPE_BASE = 10000.0
WINDOW = 128
RET_CHUNK = 128
RET_UNROLL = 8
FFN_SPLIT = 2
TOKEN_TILE = 512
C_GROUP = 16
C_STATE = 64
NEG_INF = -1e30
EPS = 1e-6
LOG2E = math.log2(math.e)
Q_SCALE = HEAD_DIM ** -0.5 * LOG2E
SHIFT_LIMIT = 50.0
VMEM_LIMIT = 56 * 1024 * 1024


def _params(*sem):
    return pltpu.CompilerParams(dimension_semantics=sem, vmem_limit_bytes=VMEM_LIMIT)


def _mm(a, b):
    return jnp.dot(a.astype(BF16), b.astype(BF16), preferred_element_type=F32)


def _mm_nt(a, b):
    return lax.dot_general(a.astype(BF16), b.astype(BF16), (((1,), (1,)), ((), ())),
                           preferred_element_type=F32)


def _mm_tn(a, b):
    return lax.dot_general(a.astype(BF16), b.astype(BF16), (((0,), (0,)), ((), ())),
                           preferred_element_type=F32)


def _rms(x):
    return x * lax.rsqrt(jnp.mean(x * x, axis=-1, keepdims=True) + EPS)


def _silu(x):
    return x * jax.nn.sigmoid(x)


def _lane_lo():
    return lax.broadcasted_iota(jnp.int32, (1, LANES), 1) < HEAD_DIM


def _same_head():
    r = lax.broadcasted_iota(jnp.int32, (LANES, LANES), 0) < HEAD_DIM
    c = lax.broadcasted_iota(jnp.int32, (LANES, LANES), 1) < HEAD_DIM
    return r == c


def _head_ones():
    return _same_head().astype(BF16)


def _headnorm64(x, bd):
    sq = x * x
    hi = sq.astype(BF16)
    lo = (sq - hi.astype(F32)).astype(BF16)
    ss = (jnp.dot(hi, bd, preferred_element_type=F32) + jnp.dot(lo, bd, preferred_element_type=F32))
    return x * lax.rsqrt(ss * (1.0 / HEAD_DIM) + EPS)


def _max_sq_head_norm(x):
    bd = _head_ones()
    best = None
    for j in range(x.shape[1] // LANES):
        xf = x[:, LANES * j:LANES * (j + 1)].astype(F32)
        m = jnp.max(jnp.dot((xf * xf).astype(BF16), bd, preferred_element_type=F32))
        best = m if best is None else jnp.maximum(best, m)
    return best


def _norm_bounds(gq, gk):
    q_sq = HEAD_DIM * Q_SCALE * Q_SCALE * jnp.max(gq.astype(F32) ** 2)
    k_sq = HEAD_DIM * jnp.max(gk.astype(F32) ** 2)
    return jnp.stack([q_sq, k_sq])


def _rope128(x, cos, sin_signed):
    lane = lax.broadcasted_iota(jnp.int32, (1, LANES), 1)
    first = (lane & (HEAD_DIM // 2)) == 0
    partner = jnp.where(first, pltpu.roll(x, LANES - HEAD_DIM // 2, 1), pltpu.roll(x, HEAD_DIM // 2, 1))
    return x * cos + partner * sin_signed


def _ada_kernel(c_ref, w_ref, b_ref, o_ref):
    o_ref[0] = _mm(_silu(c_ref[...]), w_ref[0]) + b_ref[0]


def _ada(cond, ada_w, ada_b):
    depth, d, n = ada_w.shape
    tn = n // 4
    return pl.pallas_call(
        _ada_kernel,
        grid=(depth, n // tn),
        in_specs=[pl.BlockSpec(cond.shape, lambda l, j: (0, 0)),
                  pl.BlockSpec((1, d, tn), lambda l, j: (l, 0, j)),
                  pl.BlockSpec((1, 1, tn), lambda l, j: (l, 0, j))],
        out_specs=pl.BlockSpec((1, cond.shape[0], tn), lambda l, j: (l, 0, j)),
        out_shape=jax.ShapeDtypeStruct((depth, cond.shape[0], n), F32),
        compiler_params=_params("parallel", "parallel"),
        name="ada_modulation",
    )(cond, ada_w, ada_b.reshape(depth, 1, n))


RAW, NORM_Q, NORM_K, SCALE_Q = 0, 1, 2, 3


def _inproj_kernel(*refs, plan, rope, q_scale):
    x_ref, sh_ref, sc_ref, ng_ref, w_ref, gq_ref, gk_ref = refs[:7]
    pos = 7
    if rope:
        cos_ref, sin_ref = refs[7:9]
        pos = 9
    outs = refs[pos:]
    x = x_ref[0]
    h = _rms(x) * ng_ref[...] * (1.0 + sc_ref[0]) + sh_ref[0]
    p = _mm(h, w_ref[...])
    bd = _head_ones()
    for j, (op, oi, dj) in enumerate(plan):
        chunk = p[:, LANES * j:LANES * (j + 1)]
        if op in (NORM_Q, NORM_K):
            g = gq_ref[...] if op == NORM_Q else gk_ref[...]
            chunk = _headnorm64(chunk, bd) * g
            if rope:
                chunk = _rope128(chunk, cos_ref[...], sin_ref[...])
            if op == NORM_Q:
                chunk = chunk * q_scale
        elif op == SCALE_Q:
            chunk = chunk * (HEAD_DIM ** -0.5)
        if len(outs[oi].shape) == 4:
            outs[oi][0, dj] = chunk.astype(outs[oi].dtype)
        else:
            outs[oi][0, :, LANES * dj:LANES * (dj + 1)] = chunk.astype(outs[oi].dtype)


def _inproj(x, sh, sc, ng, w, gq, gk, rope_tabs, plan, out_widths, out_dtypes, q_scale, tm, name):
    b, t, d = x.shape
    n = w.shape[1]
    rope = rope_tabs is not None
    in_specs = [pl.BlockSpec((1, tm, d), lambda i, j: (i, j, 0)),
                pl.BlockSpec((1, 1, d), lambda i, j: (i, 0, 0)),
                pl.BlockSpec((1, 1, d), lambda i, j: (i, 0, 0)),
                pl.BlockSpec((1, d), lambda i, j: (0, 0)),
                pl.BlockSpec((d, n), lambda i, j: (0, 0)),
                pl.BlockSpec((1, LANES), lambda i, j: (0, 0)),
                pl.BlockSpec((1, LANES), lambda i, j: (0, 0))]
    args = [x, sh, sc, ng, w, gq, gk]
    if rope:
        in_specs += [pl.BlockSpec((tm, LANES), lambda i, j: (j, 0))] * 2
        args += list(rope_tabs)
    return pl.pallas_call(
        functools.partial(_inproj_kernel, plan=plan, rope=rope, q_scale=q_scale),
        grid=(b, t // tm),
        in_specs=in_specs,
        out_specs=[pl.BlockSpec((1, tm, wd), lambda i, j: (i, j, 0)) if wd > 0 else
                   pl.BlockSpec((1, -wd // LANES, tm, LANES), lambda i, j: (i, 0, j, 0)) for wd in out_widths],
        out_shape=[jax.ShapeDtypeStruct((b, t, wd) if wd > 0 else (b, -wd // LANES, t, LANES), dt)
                   for wd, dt in zip(out_widths, out_dtypes)],
        compiler_params=_params("parallel", "parallel"),
        name=name,
    )(*args)


def _attn_a_kernel(*refs, latent, lq):
    if latent:
        (bnd_ref, sink_ref, q_ref, kp_ref, kc_ref, kn_ref, vp_ref, vc_ref, vn_ref, ck_ref, cv_ref, o_ref,
         cn_ref) = refs
        i = pl.program_id(1)

        @pl.when(i == 0)
        def _():
            cn_ref[0] = _max_sq_head_norm(ck_ref[0])

        k_sq = jnp.maximum(bnd_ref[1], cn_ref[0])
        last = pl.num_programs(1) - 1
        k_all = jnp.concatenate([ck_ref[0], kp_ref[0], kc_ref[0], kn_ref[0]], axis=0)
        v_all = jnp.concatenate([cv_ref[0], vp_ref[0], vc_ref[0], vn_ref[0]], axis=0)
        tc = ck_ref.shape[1]
        assert lq & (lq - 1) == 0
        r = lax.broadcasted_iota(jnp.int32, (2 * lq, lq), 0) & (lq - 1)
        c = lax.broadcasted_iota(jnp.int32, (2 * lq, lq), 1)
        piece_masks = {tc: (c >= r) & (i > 0), tc + 2 * lq: (c <= r) & (i < last)}
    else:
        bnd_ref, sink_ref, q_ref, k_ref, v_ref, o_ref = refs
        k_all = k_ref[0]
        v_all = v_ref[0]
        piece_masks = {}
        k_sq = bnd_ref[1]
    nk = k_all.shape[0]
    lo = _lane_lo()
    row = lax.broadcasted_iota(jnp.int32, (2 * lq, 1), 0)

    def masked(x, fill):
        if not piece_masks:
            return x
        pieces = [jnp.where(piece_masks[o], x[:, o:o + lq], fill) if o in piece_masks else x[:, o:o + lq]
                  for o in range(0, nk, lq)]
        return jnp.concatenate(pieces, axis=1)

    sinks = [sink_ref[h] * LOG2E for h in range(8)]
    sink_max = functools.reduce(jnp.maximum, sinks)
    bound_sq = bnd_ref[0] * k_sq
    fixed = (bound_sq <= SHIFT_LIMIT * SHIFT_LIMIT) & (sink_max <= SHIFT_LIMIT)

    def attend(softmax):
        for g in range(2):
            k2 = k_all[:, LANES * g:LANES * (g + 1)].astype(BF16)
            v2 = v_all[:, LANES * g:LANES * (g + 1)].astype(BF16)
            for jp in range(2):
                j = 2 * g + jp
                q = q_ref[0, :, LANES * j:LANES * (j + 1)]
                q2 = jnp.concatenate([jnp.where(lo, q, 0.0), jnp.where(lo, 0.0, q)], axis=0)
                sk = jnp.where(row < lq, sinks[2 * j], sinks[2 * j + 1])
                p, den = softmax(_mm_nt(q2, k2), sk)
                o2 = _mm(p, v2) / den
                o_ref[0, :, LANES * j:LANES * (j + 1)] = jnp.where(lo, o2[:lq], o2[lq:])

    @pl.when(fixed)
    def _():
        shift = jnp.maximum(jnp.sqrt(jnp.full((1, 1), bound_sq, F32)), sink_max)

        def softmax(s, sk):
            p = masked(jnp.exp2(s - shift), 0.0)
            return p, jnp.sum(p, axis=1, keepdims=True) + jnp.exp2(sk - shift)

        attend(softmax)

    @pl.when(jnp.logical_not(fixed))
    def _():
        def softmax(s, sk):
            s = masked(s, NEG_INF)
            m = jnp.maximum(jnp.max(s, axis=1, keepdims=True), sk)
            p = jnp.exp2(s - m)
            return p, jnp.sum(p, axis=1, keepdims=True) + jnp.exp2(sk - m)

        attend(softmax)


def _attn_a_ctx(qkv, sink, bounds):
    b, t, _ = qkv.shape
    return pl.pallas_call(
        functools.partial(_attn_a_kernel, latent=False, lq=t),
        grid=(b,),
        in_specs=[pl.BlockSpec(memory_space=pltpu.SMEM),
                  pl.BlockSpec(memory_space=pltpu.SMEM),
                  pl.BlockSpec((1, t, 4 * LANES), lambda i: (i, 0, 0)),
                  pl.BlockSpec((1, t, 2 * LANES), lambda i: (i, 0, 2)),
                  pl.BlockSpec((1, t, 2 * LANES), lambda i: (i, 0, 3))],
        out_specs=pl.BlockSpec((1, t, 4 * LANES), lambda i: (i, 0, 0)),
        out_shape=jax.ShapeDtypeStruct((b, t, 4 * LANES), F32),
        compiler_params=_params("parallel"),
        name="attn_a_ctx",
    )(bounds, sink, qkv, qkv, qkv)


def _attn_a_lat(qkv, ck, cv, sink, bounds):
    b, t, _ = qkv.shape
    lq = WINDOW
    nb = t // lq
    tc = ck.shape[1]

    def kv_spec(col, off):
        return pl.BlockSpec((1, lq, 2 * LANES), lambda i, j: (i, jnp.clip(j + off, 0, nb - 1), col))

    return pl.pallas_call(
        functools.partial(_attn_a_kernel, latent=True, lq=lq),
        grid=(b, nb),
        in_specs=[pl.BlockSpec(memory_space=pltpu.SMEM),
                  pl.BlockSpec(memory_space=pltpu.SMEM),
                  pl.BlockSpec((1, lq, 4 * LANES), lambda i, j: (i, j, 0)),
                  kv_spec(2, -1), kv_spec(2, 0), kv_spec(2, 1),
                  kv_spec(3, -1), kv_spec(3, 0), kv_spec(3, 1),
                  pl.BlockSpec((1, tc, 2 * LANES), lambda i, j: (i, 0, 0)),
                  pl.BlockSpec((1, tc, 2 * LANES), lambda i, j: (i, 0, 0))],
        out_specs=pl.BlockSpec((1, lq, 4 * LANES), lambda i, j: (i, j, 0)),
        out_shape=jax.ShapeDtypeStruct((b, t, 4 * LANES), F32),
        scratch_shapes=[pltpu.SMEM((1,), F32)],
        compiler_params=_params("parallel", "arbitrary"),
        name="attn_a_lat",
    )(bounds, sink, qkv, qkv, qkv, qkv, qkv, qkv, qkv, ck, cv)


def _retention_kernel(*refs, zero_init):
    if zero_init:
        lg_ref, q_ref, k_ref, v_ref, o_ref, sfin_ref, sp_ref = refs
    else:
        lg_ref, q_ref, k_ref, v_ref, s0_ref, o_ref, sfin_ref, sp_ref = refs
    pr = pl.program_id(1)
    L = RET_CHUNK
    n = q_ref.shape[1] // L
    lo = _lane_lo()
    lgf = jnp.where(lo, lg_ref[0, 2 * pr], lg_ref[0, 2 * pr + 1])
    lgb = jnp.where(lo, lg_ref[1, 2 * pr], lg_ref[1, 2 * pr + 1])
    idx = lax.broadcasted_iota(jnp.int32, (L, 1), 0).astype(F32)
    qdf = jnp.exp(lgf * (idx + 1.0))
    kdf = jnp.exp(lgf * (L - 1.0 - idx))
    cdf = jnp.exp(lgf * float(L))
    qdb = jnp.exp(lgb * (L - idx))
    kdb = jnp.exp(lgb * idx)
    cdb = jnp.exp(lgb * float(L))
    diff = (lax.broadcasted_iota(jnp.int32, (L, L), 0) - lax.broadcasted_iota(jnp.int32, (L, L), 1)).astype(F32)

    def decay(h):
        f = jnp.exp(lg_ref[0, 2 * pr + h] * jnp.maximum(diff, 0.0))
        bwd = jnp.exp(lg_ref[1, 2 * pr + h] * jnp.maximum(-diff, 0.0))
        return jnp.where(diff > 0, f, jnp.where(diff < 0, bwd, 2.0))

    d0 = decay(0)
    d1 = decay(1)
    blk = _same_head()

    if zero_init:
        sf0 = jnp.zeros((LANES, LANES), F32)
        sb0 = jnp.zeros((LANES, LANES), F32)
    else:
        sf0 = s0_ref[0, 0, 0]
        sb0 = s0_ref[0, 0, 1]

    qd = jnp.concatenate([qdf, qdb], axis=1)
    kd = jnp.concatenate([kdf, kdb], axis=1)
    blk2 = jnp.concatenate([blk, blk], axis=0)
    unroll = math.gcd(n, RET_UNROLL)

    def chunk(c):
        return pl.ds(pl.multiple_of(c * L, L), L)

    def kv_body(c, carry):
        k = k_ref[0, chunk(c), :]
        kv = _mm_tn(jnp.concatenate([k, k], axis=1) * kd, v_ref[0, chunk(c), :])
        sp_ref[c] = jnp.where(blk2, kv, 0.0)
        return carry

    lax.fori_loop(0, n, kv_body, 0, unroll=unroll)

    def f_body(c, sf):
        kv = sp_ref[c, :LANES, :]
        sp_ref[c, :LANES, :] = sf
        return sf * cdf + kv

    def b_body(i, sb):
        c = n - 1 - i
        kv = sp_ref[c, LANES:, :]
        sp_ref[c, LANES:, :] = sb
        return sb * cdb + kv

    sfin_ref[0, 0, 0] = lax.fori_loop(0, n, f_body, sf0)
    sfin_ref[0, 0, 1] = lax.fori_loop(0, n, b_body, sb0)

    def o_body(c, carry):
        q = q_ref[0, chunk(c), :]
        v = v_ref[0, chunk(c), :]
        s = _mm_nt(jnp.concatenate([jnp.where(lo, q, 0.0), jnp.where(lo, 0.0, q)], axis=0), k_ref[0, chunk(c), :])
        sd = jnp.concatenate([s[:L] * d0, s[L:] * d1], axis=1)
        v2 = jnp.concatenate([jnp.where(lo, v, 0.0), jnp.where(lo, 0.0, v)], axis=0)
        o_ref[0, chunk(c), :] = _mm(sd, v2) + _mm(jnp.concatenate([q, q], axis=1) * qd, sp_ref[c])
        return carry

    lax.fori_loop(0, n, o_body, 0, unroll=unroll)


def _retention(qkv, log_g, s0):
    b, t, _ = qkv.shape
    npair = 4
    in_specs = [pl.BlockSpec(memory_space=pltpu.SMEM),
                pl.BlockSpec((1, t, LANES), lambda i, p: (i, 0, p)),
                pl.BlockSpec((1, t, LANES), lambda i, p: (i, 0, npair + p)),
                pl.BlockSpec((1, t, LANES), lambda i, p: (i, 0, 2 * npair + p))]
    args = [log_g, qkv, qkv, qkv]
    if s0 is not None:
        in_specs.append(pl.BlockSpec((1, 1, 2, LANES, LANES), lambda i, p: (i, p, 0, 0, 0)))
        args.append(s0)
    return pl.pallas_call(
        functools.partial(_retention_kernel, zero_init=s0 is None),
        grid=(b, npair),
        in_specs=in_specs,
        out_specs=[pl.BlockSpec((1, t, LANES), lambda i, p: (i, 0, p)),
                   pl.BlockSpec((1, 1, 2, LANES, LANES), lambda i, p: (i, p, 0, 0, 0))],
        out_shape=[jax.ShapeDtypeStruct((b, t, npair * LANES), F32),
                   jax.ShapeDtypeStruct((b, npair, 2, LANES, LANES), F32)],
        scratch_shapes=[pltpu.VMEM((t // RET_CHUNK, 2 * LANES, LANES), F32)],
        compiler_params=_params("parallel", "parallel"),
        name="retention",
    )(*args)


S5_CHUNK = 16
S5_GROUPS = 4


def _s5_kernel(u_ref, m_ref, w_ref, v_ref, a_ref, h0_ref, y_ref, fin_ref, z_ref, sf_ref, sb_ref, *, nb):
    rows = u_ref.shape[2]
    n = rows // nb
    lo = _lane_lo()
    nbg = nb // SUBLANES
    for gi in range(u_ref.shape[0]):
        u = jnp.concatenate([u_ref[gi, t2] for t2 in range(u_ref.shape[1])], axis=1).astype(BF16)
        z_ref[...] = jnp.dot(u, w_ref[gi], preferred_element_type=F32)
        are = jnp.broadcast_to(a_ref[gi, 0:1, :], (SUBLANES, LANES))
        aim = jnp.broadcast_to(a_ref[gi, 1:2, :], (SUBLANES, LANES))

        def step(i, carry):
            out = []
            for bg in range(nbg):
                hre, him = carry[bg]
                up = pl.ds(pl.multiple_of(i * nb + bg * SUBLANES, SUBLANES), SUBLANES)
                dn = pl.ds(pl.multiple_of((n - 1 - i) * nb + bg * SUBLANES, SUBLANES), SUBLANES)
                sf_ref[up, :LANES] = hre
                sf_ref[up, LANES:] = him
                sb_ref[dn, :LANES] = hre
                sb_ref[dn, LANES:] = him
                zre = jnp.where(lo, z_ref[up, :LANES], z_ref[dn, :LANES])
                zim = jnp.where(lo, z_ref[up, LANES:], z_ref[dn, LANES:])
                out.append((are * hre - aim * him + zre, are * him + aim * hre + zim))
            return tuple(out)

        init = tuple((h0_ref[gi, bg * SUBLANES:(bg + 1) * SUBLANES, :LANES],
                      h0_ref[gi, bg * SUBLANES:(bg + 1) * SUBLANES, LANES:]) for bg in range(nbg))
        last = lax.fori_loop(0, n, step, init, unroll=2)
        for bg in range(nbg):
            fin_ref[gi, bg * SUBLANES:(bg + 1) * SUBLANES, :LANES] = last[bg][0]
            fin_ref[gi, bg * SUBLANES:(bg + 1) * SUBLANES, LANES:] = last[bg][1]
        y = (jnp.dot(u, m_ref[gi], preferred_element_type=F32)
             + _mm(sf_ref[...], v_ref[gi, 0]) + _mm(sb_ref[...], v_ref[gi, 1]))
        for t2 in range(y_ref.shape[1]):
            y_ref[gi, t2] = y[:, LANES * t2:LANES * (t2 + 1)]


GROUPS_PER_TILE = LANES // C_GROUP


def _lane_group():
    return lax.broadcasted_iota(jnp.int32, (1, LANES), 1) >> (C_GROUP.bit_length() - 1)


S5_TILES = S5_CHUNK * C_GROUP // LANES


def _swap_blocks(xs, grp):
    xs = list(xs)
    d = GROUPS_PER_TILE // 2
    while d >= 1:
        keep = (grp & d) == 0
        for i in range(GROUPS_PER_TILE):
            if i & d:
                continue
            lo_x, hi_x = xs[i], xs[i + d]
            xs[i] = jnp.where(keep, lo_x, pltpu.roll(hi_x, C_GROUP * d, 1))
            xs[i + d] = jnp.where(keep, pltpu.roll(lo_x, LANES - C_GROUP * d, 1), hi_x)
        d //= 2
    return xs


def _s5_pack_kernel(u_ref, o_ref):
    nb, tiles, tb, _ = u_ref.shape
    n = tb // S5_CHUNK
    grp = _lane_group()

    def per_batch(b, carry):
        for j in range(tiles):
            for t2 in range(S5_TILES):
                steps = [u_ref[b, j, pl.ds(GROUPS_PER_TILE * t2 + tl, n, stride=S5_CHUNK), :]
                         for tl in range(GROUPS_PER_TILE)]
                for q, x in enumerate(_swap_blocks(steps, grp)):
                    o_ref[GROUPS_PER_TILE * j + q, t2, pl.ds(b, n, stride=nb), :] = x
        return carry

    lax.fori_loop(0, nb, per_batch, 0)


def _s5_unpack_kernel(y_ref, o_ref):
    nb, tiles, tb, _ = o_ref.shape
    n = tb // S5_CHUNK
    grp = _lane_group()

    def per_batch(b, carry):
        for j in range(tiles):
            for t2 in range(S5_TILES):
                src = [y_ref[GROUPS_PER_TILE * j + q, t2, pl.ds(b, n, stride=nb), :] for q in range(GROUPS_PER_TILE)]
                for tl, x in enumerate(_swap_blocks(src, grp)):
                    o_ref[b, j, pl.ds(GROUPS_PER_TILE * t2 + tl, n, stride=S5_CHUNK), :] = x
        return carry

    lax.fori_loop(0, nb, per_batch, 0)


def _s5_relayout(x, b, t, cw, pack):
    g = cw // C_GROUP
    tb = max(S5_CHUNK * SUBLANES, 2048 // b)
    tok = pl.BlockSpec((b, cw // LANES, tb, LANES), lambda i: (0, 0, i, 0))
    grp = pl.BlockSpec((g, S5_TILES, tb // S5_CHUNK * b, LANES), lambda i: (0, 0, i, 0))
    tok_shape = jax.ShapeDtypeStruct((b, cw // LANES, t, LANES), F32)
    grp_shape = jax.ShapeDtypeStruct((g, S5_TILES, t // S5_CHUNK * b, LANES), F32)
    return pl.pallas_call(
        _s5_pack_kernel if pack else _s5_unpack_kernel,
        grid=(t // tb,),
        in_specs=[tok if pack else grp],
        out_specs=grp if pack else tok,
        out_shape=grp_shape if pack else tok_shape,
        compiler_params=_params("parallel"),
        name="s5_pack" if pack else "s5_unpack",
    )(x)


def _s5(u, prm, h0_re, h0_im):
    m, w, v, a = prm
    b, tiles, t, _ = u.shape
    cw = tiles * LANES
    g = cw // C_GROUP
    n = t // S5_CHUNK
    rows = n * b
    width = S5_CHUNK * C_GROUP
    ug = _s5_relayout(u, b, t, cw, True)
    if h0_re is None:
        h0 = jnp.zeros((g, b, 4 * C_STATE), F32)
    else:
        h0 = jnp.concatenate([h0_re[:, 0], h0_re[:, 1], h0_im[:, 0], h0_im[:, 1]], axis=-1).transpose(1, 0, 2)
    gs = S5_GROUPS

    def grp(shape):
        nd = len(shape)
        return pl.BlockSpec((gs,) + shape[1:], lambda i: (i,) + (0,) * (nd - 1))

    y, fin = pl.pallas_call(
        functools.partial(_s5_kernel, nb=b),
        grid=(g // gs,),
        in_specs=[grp(ug.shape), grp(m.shape), grp(w.shape), grp(v.shape), grp(a.shape), grp(h0.shape)],
        out_specs=[grp(ug.shape), grp(h0.shape)],
        out_shape=[jax.ShapeDtypeStruct(ug.shape, F32), jax.ShapeDtypeStruct(h0.shape, F32)],
        scratch_shapes=[pltpu.VMEM((rows, width), F32)] * 3,
        compiler_params=_params("parallel"),
        name="s5_scan",
    )(ug, m, w, v, a, h0)
    y = _s5_relayout(y, b, t, cw, False)
    fin = fin.reshape(g, b, 2, 2, C_STATE).transpose(2, 1, 3, 0, 4)
    return y, fin[0], fin[1]


def _s5_params(lam_re, lam_im, log_dt, b_re, b_im, c_re, c_im):
    dt = jnp.exp(log_dt)[..., None]
    mag = jnp.exp(lam_re * dt)
    a_re = mag * jnp.cos(lam_im * dt)
    a_im = mag * jnp.sin(lam_im * dt)
    den = lam_re * lam_re + lam_im * lam_im
    f_re = ((a_re - 1.0) * lam_re + a_im * lam_im) / den
    f_im = (a_im * lam_re - (a_re - 1.0) * lam_im) / den
    fb_re = f_re[..., None] * b_re[None] - f_im[..., None] * b_im[None]
    fb_im = f_re[..., None] * b_im[None] + f_im[..., None] * b_re[None]
    g = lam_re.shape[1]
    L = S5_CHUNK
    k = jnp.arange(L + 1, dtype=F32)[:, None, None, None]
    pmag = jnp.exp(lam_re * dt * k)
    p_re = pmag * jnp.cos(lam_im * dt * k)
    p_im = pmag * jnp.sin(lam_im * dt * k)
    hp = lax.Precision.HIGHEST

    ca_re = c_re[None, None] * p_re[:, :, :, None, :] - c_im[None, None] * p_im[:, :, :, None, :]
    ca_im = c_re[None, None] * p_im[:, :, :, None, :] + c_im[None, None] * p_re[:, :, :, None, :]
    kern = (jnp.einsum('kdgcp,dgpe->kdgce', ca_re, fb_re, precision=hp)
            - jnp.einsum('kdgcp,dgpe->kdgce', ca_im, fb_im, precision=hp))
    sig = jnp.arange(L)[:, None]
    tau = jnp.arange(L)[None, :]
    kf = jnp.where((tau >= sig)[:, :, None, None, None], kern[jnp.clip(tau - sig, 0, L), 0], 0.0)
    kb = jnp.where((sig >= tau)[:, :, None, None, None], kern[jnp.clip(sig - tau, 0, L), 1], 0.0)
    m = (kf + kb).transpose(2, 0, 4, 1, 3).reshape(g, L * C_GROUP, L * C_GROUP)

    def to_state(pw_re, pw_im, d):
        re = pw_re[..., None] * fb_re[d][None] - pw_im[..., None] * fb_im[d][None]
        im = pw_re[..., None] * fb_im[d][None] + pw_im[..., None] * fb_re[d][None]
        return re, im

    wf_re, wf_im = to_state(p_re[L - 1::-1, 0][:L], p_im[L - 1::-1, 0][:L], 0)
    wb_re, wb_im = to_state(p_re[:L, 1], p_im[:L, 1], 1)
    w = jnp.concatenate([wf_re, wb_re, wf_im, wb_im], axis=2)
    w = w.transpose(1, 0, 3, 2).reshape(g, L * C_GROUP, 4 * C_STATE)

    def from_state(car, cai):
        return car.transpose(1, 3, 0, 2), -cai.transpose(1, 3, 0, 2)

    vf_re, vf_im = from_state(ca_re[1:, 0], ca_im[1:, 0])
    vb_re, vb_im = from_state(ca_re[L:0:-1, 1], ca_im[L:0:-1, 1])
    zero = jnp.zeros_like(vf_re)
    v = jnp.stack([jnp.concatenate([vf_re, zero, vf_im, zero], axis=1),
                   jnp.concatenate([zero, vb_re, zero, vb_im], axis=1)], axis=1)
    v = v.reshape(g, 2, 4 * C_STATE, L * C_GROUP)

    a = jnp.stack([jnp.concatenate([p_re[L, 0], p_re[L, 1]], axis=-1),
                   jnp.concatenate([p_im[L, 0], p_im[L, 1]], axis=-1)], axis=1)
    return m.astype(BF16), w.astype(BF16), v.astype(BF16), a


def _diff_attn_kernel(*refs, latent, kc):
    if latent:
        bnd_ref, lam_ref, q_ref, k_ref, v_ref, ck_ref, cv_ref, o_ref, cn_ref = refs
    else:
        bnd_ref, lam_ref, q_ref, k_ref, v_ref, o_ref = refs
    lq = q_ref.shape[1]
    lo = _lane_lo()
    q = q_ref[0]
    qs = (jnp.where(lo, q, 0.0).astype(BF16), jnp.where(lo, 0.0, q).astype(BF16))

    def over_keys(step, carry):
        if latent:
            carry = step(ck_ref[0], cv_ref[0], carry)

            def body(c, carry):
                rows = pl.ds(pl.multiple_of(c * kc, kc), kc)
                return step(k_ref[0, rows, :], v_ref[0, rows, :], carry)

            return lax.fori_loop(0, k_ref.shape[1] // kc, body, carry)
        return step(k_ref[0], v_ref[0], carry)

    def finish(l0, acc0, l1, acc1):
        o_ref[0] = acc0 / l0 - lam_ref[0] * (acc1 / l1)

    def online_step(kb, vb, carry):
        kb = kb.astype(BF16)
        vb = vb.astype(BF16)
        out = []
        for qc, (m, l, acc) in zip(qs, carry):
            s = _mm_nt(qc, kb)
            m_new = jnp.maximum(m, jnp.max(s, axis=1, keepdims=True))
            alpha = jnp.exp2(m - m_new)
            p = jnp.exp2(s - m_new)
            l = alpha * l + jnp.sum(p, axis=1, keepdims=True)
            acc = alpha * acc + _mm(p, vb)
            out.append((m_new, l, acc))
        return tuple(out)

    def online():
        init = (jnp.full((lq, 1), NEG_INF, F32), jnp.zeros((lq, 1), F32), jnp.zeros((lq, LANES), F32))
        (_, l0, acc0), (_, l1, acc1) = over_keys(online_step, (init, init))
        finish(l0, acc0, l1, acc1)

    if latent:
        @pl.when(pl.program_id(2) == 0)
        def _():
            cn_ref[0] = _max_sq_head_norm(ck_ref[0])

        bound_sq = bnd_ref[0] * jnp.maximum(bnd_ref[1], cn_ref[0])
    else:
        bound_sq = bnd_ref[0] * bnd_ref[1]
    fixed = bound_sq <= SHIFT_LIMIT * SHIFT_LIMIT

    @pl.when(fixed)
    def _():
        shift = jnp.sqrt(jnp.full((1, 1), bound_sq, F32))

        def step(kb, vb, carry):
            kb = kb.astype(BF16)
            vb = vb.astype(BF16)
            out = []
            for qc, (l, acc) in zip(qs, carry):
                p = jnp.exp2(_mm_nt(qc, kb) - shift)
                out.append((l + jnp.sum(p, axis=1, keepdims=True), acc + _mm(p, vb)))
            return tuple(out)

        init = (jnp.zeros((lq, 1), F32), jnp.zeros((lq, LANES), F32))
        (l0, acc0), (l1, acc1) = over_keys(step, (init, init))
        finish(l0, acc0, l1, acc1)

    @pl.when(jnp.logical_not(fixed))
    def _():
        online()


def _diff_attn(qkv, lam, cache=None, bounds=None, lq=512, kc=2048):
    b, t, _ = qkv.shape
    nh = 4
    latent = cache is not None
    kc = min(kc, t)
    assert t % kc == 0 and t % lq == 0
    in_specs = [pl.BlockSpec(memory_space=pltpu.SMEM),
                pl.BlockSpec(memory_space=pltpu.SMEM),
                pl.BlockSpec((1, lq, LANES), lambda i, h, j: (i, j, h)),
                pl.BlockSpec((1, t, LANES), lambda i, h, j: (i, 0, nh + h)),
                pl.BlockSpec((1, t, LANES), lambda i, h, j: (i, 0, 2 * nh + h))]
    args = [bounds, lam, qkv, qkv, qkv]
    if latent:
        tc = cache[0].shape[1]
        in_specs += [pl.BlockSpec((1, tc, LANES), lambda i, h, j: (i, 0, h))] * 2
        args += list(cache)
    return pl.pallas_call(
        functools.partial(_diff_attn_kernel, latent=latent, kc=kc),
        grid=(b, nh, t // lq),
        in_specs=in_specs,
        out_specs=pl.BlockSpec((1, lq, LANES), lambda i, h, j: (i, j, h)),
        out_shape=jax.ShapeDtypeStruct((b, t, nh * LANES), F32),
        scratch_shapes=[pltpu.SMEM((1,), F32)] if latent else [],
        compiler_params=_params("parallel", "parallel", "arbitrary"),
        name="diff_attn_lat" if latent else "diff_attn_ctx",
    )(*args)


def _resid_ffn(x, mix, g1, n2g, sh2, sc2, g2, w1_ref, w3_ref, w2_ref):
    x1 = x + g1 * mix
    h = (_rms(x1) * n2g * (1.0 + sc2) + sh2).astype(BF16)
    nf = w1_ref.shape[1]
    step = nf // FFN_SPLIT
    f = None
    for c0 in range(0, nf, step):
        a = jnp.dot(h, w1_ref[:, c0:c0 + step], preferred_element_type=F32)
        b = jnp.dot(h, w3_ref[:, c0:c0 + step], preferred_element_type=F32)
        part = _mm(_silu(a) * b, w2_ref[c0:c0 + step, :])
        f = part if f is None else f + part
    return x1 + g2 * f


def _out_ab_kernel(x_ref, oa_ref, ob_ref, gb_ref, rg_ref, wo_ref, g1_ref, n2g_ref, sh2_ref, sc2_ref,
                   g2_ref, w1_ref, w3_ref, w2_ref, y_ref):
    bd = _head_ones()
    half = wo_ref.shape[0] // 2
    mix = _mm(oa_ref[0], wo_ref[:half, :])
    for j in range(half // LANES):
        cols = slice(LANES * j, LANES * (j + 1))
        ob = _headnorm64(ob_ref[0, :, cols], bd) * rg_ref[...] * _silu(gb_ref[0, :, cols])
        mix += _mm(ob, wo_ref[half + LANES * j:half + LANES * (j + 1), :])
    y_ref[0] = _resid_ffn(x_ref[0], mix, g1_ref[0], n2g_ref[...], sh2_ref[0], sc2_ref[0], g2_ref[0],
                          w1_ref, w3_ref, w2_ref)


def _out_cd_kernel(x_ref, ys_ref, u_ref, od_ref, dsk_ref, gw_ref, gbias_ref, sg_ref, wo_ref,
                   g1_ref, n2g_ref, sh2_ref, sc2_ref, g2_ref, w1_ref, w3_ref, w2_ref, y_ref, *, od_scale):
    half = wo_ref.shape[0] // 2
    tiles = range(ys_ref.shape[1])
    y = (jnp.concatenate([ys_ref[0, j] for j in tiles], axis=1)
         + dsk_ref[...] * jnp.concatenate([u_ref[0, j] for j in tiles], axis=1))
    g = jax.nn.gelu(y)
    oc = g * jax.nn.sigmoid(_mm(g, gw_ref[...]) + gbias_ref[...])
    mix = _mm(oc, wo_ref[:half, :])
    for j in range(half // LANES):
        od = _rms(od_ref[0, :, LANES * j:LANES * (j + 1)]) * sg_ref[...] * od_scale
        mix += _mm(od, wo_ref[half + LANES * j:half + LANES * (j + 1), :])
    y_ref[0] = _resid_ffn(x_ref[0], mix, g1_ref[0], n2g_ref[...], sh2_ref[0], sc2_ref[0], g2_ref[0],
                          w1_ref, w3_ref, w2_ref)


def _const_spec(shape):
    nd = len(shape)
    return pl.BlockSpec(shape, lambda i, j: (0,) * nd, pipeline_mode=pl.Buffered(1))


def _out_call(kern, name, x, tok_args, const_args, mod, ffn, tm):
    b, t, d = x.shape
    g1, sh2, sc2, g2 = mod
    n2g, w1, w3, w2 = ffn

    def tok(a):
        if a.ndim == 4:
            return pl.BlockSpec((1, a.shape[1], tm, LANES), lambda i, j: (i, 0, j, 0))
        return pl.BlockSpec((1, tm, a.shape[2]), lambda i, j: (i, j, 0))

    def per_batch(a):
        return pl.BlockSpec((1, 1, d), lambda i, j: (i, 0, 0))

    args = [x] + list(tok_args) + list(const_args) + [g1, n2g, sh2, sc2, g2, w1, w3, w2]
    in_specs = ([tok(x)] + [tok(a) for a in tok_args] + [_const_spec(a.shape) for a in const_args]
                + [per_batch(g1), _const_spec(n2g.shape), per_batch(sh2), per_batch(sc2), per_batch(g2),
                   _const_spec(w1.shape), _const_spec(w3.shape), _const_spec(w2.shape)])
    return pl.pallas_call(
        kern,
        grid=(b, t // tm),
        in_specs=in_specs,
        out_specs=tok(x),
        out_shape=jax.ShapeDtypeStruct(x.shape, F32),
        compiler_params=_params("parallel", "parallel"),
        name=name,
    )(*args)


def _ab_plan():
    plan = [(NORM_Q, 0, j) for j in range(4)] + [(NORM_K, 0, 4), (NORM_K, 0, 5), (RAW, 0, 6), (RAW, 0, 7)]
    plan += [(SCALE_Q, 1, j) for j in range(4)] + [(RAW, 1, 4 + j) for j in range(8)]
    plan += [(RAW, 2, j) for j in range(4)]
    return tuple(plan)


def _dup_kv_cols(w):
    q, kv, rest = w[:, :4 * LANES], w[:, 4 * LANES:6 * LANES], w[:, 6 * LANES:]
    kv = jnp.repeat(kv.reshape(w.shape[0], 4, 1, HEAD_DIM), 2, axis=2).reshape(w.shape[0], 4 * LANES)
    return jnp.concatenate([q, kv, rest], axis=1)


def _dup_heads(x):
    return jnp.repeat(x[..., :, None, :], 2, axis=-2).reshape(x.shape[:-2] + (4 * HEAD_DIM,))


def _cd_plan():
    plan = [(RAW, 0, j) for j in range(4)]
    plan += [(NORM_Q, 1, j) for j in range(4)] + [(NORM_K, 1, 4 + j) for j in range(4)]
    plan += [(RAW, 1, 8 + j) for j in range(4)]
    return tuple(plan)


AB_PLAN = _ab_plan()
CD_PLAN = _cd_plan()


def _rope_tables(t):
    rows = t // GRID_W
    r = jnp.repeat(jnp.arange(rows, dtype=F32), GRID_W)
    col = jnp.tile(jnp.arange(GRID_W, dtype=F32), rows)
    n_freq = HEAD_DIM // 4
    inv = ROPE_BASE ** (-jnp.arange(n_freq, dtype=F32) / n_freq)
    ang = jnp.concatenate([r[:, None] * inv, col[:, None] * inv], axis=-1)
    cos, sin = jnp.cos(ang), jnp.sin(ang)
    cos128 = jnp.tile(cos, (1, 4))
    sin128 = jnp.tile(jnp.concatenate([-sin, sin], axis=-1), (1, 2))
    return cos128, sin128


def _tile2(g):
    return jnp.tile(g.astype(F32), 2).reshape(1, LANES)


def kernel(x_prompt, x_sample, c, c_ctx, cache_k_a, cache_v_a, state_ret, state_ssm_re, state_ssm_im,
           cache_k_d, cache_v_d, ada_w, ada_b, norm1_g, norm2_g, ffn_w1, ffn_w3, ffn_w2,
           ab_w_in, ab_w_out, a_q_norm, a_k_norm, a_sink, ret_decay, ret_norm,
           cd_w_in, cd_w_out, ssm_lambda_re, ssm_lambda_im, ssm_log_dt, ssm_b_re, ssm_b_im,
           ssm_c_re, ssm_c_im, ssm_d, ssm_glu_w, ssm_glu_b, d_q_norm, d_k_norm, d_lambda, d_subln):
    nbc, tcx, d = x_prompt.shape
    nbl, tl, _ = x_sample.shape
    depth = ada_w.shape[0]
    rope_tabs = _rope_tables(tl)

    ncond = -(-(nbl + 1) // SUBLANES) * SUBLANES
    cond = jnp.zeros((ncond, d), F32).at[:nbl].set(c).at[nbl].set(c_ctx)
    mods = _ada(cond, ada_w, ada_b)

    def mod_split(l, lo, hi):
        m = mods[l, lo:hi].reshape(hi - lo, 1, 6, d)
        return [m[:, :, k] for k in range(6)]

    yp = x_prompt.reshape(1, nbc * tcx, d)
    ys = x_sample
    new_ka, new_va, new_ret, new_sr, new_si, new_kd, new_vd = [], [], [], [], [], [], []
    for l in range(depth):
        j = l // 2
        ffn = (norm2_g[l].reshape(1, d), ffn_w1[l].astype(BF16), ffn_w3[l].astype(BF16), ffn_w2[l].astype(BF16))
        n1g = norm1_g[l].reshape(1, d)
        sh1c, sc1c, g1c, sh2c, sc2c, g2c = mod_split(l, nbl, nbl + 1)
        sh1l, sc1l, g1l, sh2l, sc2l, g2l = mod_split(l, 0, nbl)
        if l % 2 == 0:
            w_in = _dup_kv_cols(ab_w_in[j]).astype(BF16)
            w_out = ab_w_out[j].astype(BF16)
            gq, gk = _tile2(a_q_norm[j]), _tile2(a_k_norm[j])
            rg = _tile2(ret_norm[j])
            log_g = jax.nn.log_sigmoid(ret_decay[j].astype(F32))
            widths = (8 * LANES, 12 * LANES, 4 * LANES)
            a_scale = Q_SCALE
            qkv_a, qkv_b, gb = _inproj(yp, sh1c, sc1c, n1g, w_in, gq, gk, None, AB_PLAN, widths,
                                       (F32, F32, F32), a_scale, TOKEN_TILE, "inproj_ab_ctx")
            qkv_a = qkv_a.reshape(nbc, tcx, -1)
            bounds = _norm_bounds(a_q_norm[j], a_k_norm[j])
            oa = _attn_a_ctx(qkv_a, a_sink[j], bounds)
            ob, s_ret = _retention(qkv_b.reshape(nbc, tcx, -1), log_g, None)
            yp = _out_call(_out_ab_kernel, "out_ab_ctx", yp,
                           [oa.reshape(1, nbc * tcx, -1), ob.reshape(1, nbc * tcx, -1), gb],
                           [rg, w_out], (g1c, sh2c, sc2c, g2c), ffn, TOKEN_TILE)
            new_ka.append(qkv_a[:, :, 4 * LANES:6 * LANES].reshape(nbc, tcx, 2, 2, HEAD_DIM)[:, :, :, 0])
            new_va.append(qkv_a[:, :, 6 * LANES:8 * LANES].reshape(nbc, tcx, 2, 2, HEAD_DIM)[:, :, :, 0])
            s_ret = jnp.stack([s_ret[..., :HEAD_DIM, :HEAD_DIM], s_ret[..., HEAD_DIM:, HEAD_DIM:]], axis=2)
            new_ret.append(s_ret.transpose(0, 3, 1, 2, 4, 5).reshape(nbc, 2, 8, HEAD_DIM, HEAD_DIM))
            qkv_a, qkv_b, gb = _inproj(ys, sh1l, sc1l, n1g, w_in, gq, gk, rope_tabs, AB_PLAN, widths,
                                       (BF16, F32, F32), a_scale, TOKEN_TILE, "inproj_ab_lat")
            ck = _dup_heads(cache_k_a[:, j]).astype(BF16)
            cv = _dup_heads(cache_v_a[:, j]).astype(BF16)
            oa = _attn_a_lat(qkv_a, ck, cv, a_sink[j], bounds)
            s0 = state_ret[:, j].reshape(nbl, 2, 4, 2, HEAD_DIM, HEAD_DIM)
            z = jnp.zeros_like(s0[:, :, :, 0])
            s0 = jnp.concatenate([jnp.concatenate([s0[:, :, :, 0], z], axis=-1),
                                  jnp.concatenate([z, s0[:, :, :, 1]], axis=-1)], axis=-2)
            s0 = s0.transpose(0, 2, 1, 3, 4)
            ob, _ = _retention(qkv_b, log_g, s0)
            ys = _out_call(_out_ab_kernel, "out_ab_lat", ys, [oa, ob, gb], [rg, w_out],
                           (g1l, sh2l, sc2l, g2l), ffn, TOKEN_TILE)
        else:
            lam_init = 0.8 - 0.6 * math.exp(-0.3 * l)
            w_in = cd_w_in[j].astype(BF16)
            w_out = cd_w_out[j].astype(BF16)
            gq, gk = _tile2(d_q_norm[j]), _tile2(d_k_norm[j])
            lp = d_lambda[j].astype(F32)
            lam = (jnp.exp(jnp.sum(lp[0] * lp[1])) - jnp.exp(jnp.sum(lp[2] * lp[3])) + lam_init).reshape(1)
            prm = _s5_params(ssm_lambda_re[j], ssm_lambda_im[j], ssm_log_dt[j], ssm_b_re[j], ssm_b_im[j],
                             ssm_c_re[j], ssm_c_im[j])
            consts =[ssm_d[j].reshape(1, -1), ssm_glu_w[j].astype(BF16), ssm_glu_b[j].reshape(1, -1),
                      d_subln[j].reshape(1, LANES), w_out]
            kern = functools.partial(_out_cd_kernel, od_scale=1.0 - lam_init)
            widths = (-4 * LANES, 12 * LANES)
            d_scale = Q_SCALE
            u, qkv_d = _inproj(yp, sh1c, sc1c, n1g, w_in, gq, gk, None, CD_PLAN, widths,
                               (F32, F32), d_scale, TOKEN_TILE, "inproj_cd_ctx")
            u_b = u.reshape(-1, nbc, tcx, LANES).transpose(1, 0, 2, 3)
            y, fin_re, fin_im = _s5(u_b, prm, None, None)
            y = y.transpose(1, 0, 2, 3).reshape(u.shape)
            qkv_d = qkv_d.reshape(nbc, tcx, -1)
            d_bounds = _norm_bounds(d_q_norm[j], d_k_norm[j])
            od = _diff_attn(qkv_d, lam, None, d_bounds, lq=tcx)
            yp = _out_call(kern, "out_cd_ctx", yp, [y, u, od.reshape(1, nbc * tcx, -1)],
                           consts, (g1c, sh2c, sc2c, g2c), ffn, TOKEN_TILE)
            new_sr.append(fin_re)
            new_si.append(fin_im)
            new_kd.append(qkv_d[:, :, 4 * LANES:8 * LANES].reshape(nbc, tcx, 4, 2, HEAD_DIM))
            new_vd.append(qkv_d[:, :, 8 * LANES:].reshape(nbc, tcx, 4, 2 * HEAD_DIM))
            u, qkv_d = _inproj(ys, sh1l, sc1l, n1g, w_in, gq, gk, rope_tabs, CD_PLAN, widths,
                               (F32, BF16), d_scale, TOKEN_TILE, "inproj_cd_lat")
            y, _, _ = _s5(u, prm, state_ssm_re[:, j], state_ssm_im[:, j])
            ckd = cache_k_d[:, j].reshape(nbl, -1, 4 * LANES)
            cvd = cache_v_d[:, j].reshape(nbl, -1, 4 * LANES)
            od = _diff_attn(qkv_d, lam, (ckd, cvd), d_bounds)
            ys = _out_call(kern, "out_cd_lat", ys, [y, u, od], consts, (g1l, sh2l, sc2l, g2l), ffn, TOKEN_TILE)
    return (yp.reshape(nbc, tcx, d), ys,
            jnp.stack(new_ka, axis=1), jnp.stack(new_va, axis=1), jnp.stack(new_ret, axis=1),
            jnp.stack(new_sr, axis=1), jnp.stack(new_si, axis=1),
            jnp.stack(new_kd, axis=1), jnp.stack(new_vd, axis=1))
```

```python
import functools
import math

import jax
import jax.numpy as jnp
from jax import lax
from jax.experimental import pallas as pl
from jax.experimental.pallas import tpu as pltpu

F32 = jnp.float32
BF16 = jnp.bfloat16

LANES = 128
SUBLANES = 8
HEAD_DIM = 64
GRID_W = 64
ROPE_BASE = 10000.0
WINDOW = 128
RET_CHUNK = 128
RET_UNROLL = 8
FFN_SPLIT = 2
TOKEN_TILE = 512
C_GROUP = 16
C_STATE = 64
NEG_INF = -1e30
EPS = 1e-6
LOG2E = math.log2(math.e)
Q_SCALE = HEAD_DIM ** -0.5 * LOG2E
SHIFT_LIMIT = 50.0
VMEM_LIMIT = 56 * 1024 * 1024


def _params(*sem):
    return pltpu.CompilerParams(dimension_semantics=sem, vmem_limit_bytes=VMEM_LIMIT)


def _mm(a, b):
    return jnp.dot(a.astype(BF16), b.astype(BF16), preferred_element_type=F32)


def _mm_nt(a, b):
    return lax.dot_general(a.astype(BF16), b.astype(BF16), (((1,), (1,)), ((), ())),
                           preferred_element_type=F32)


def _mm_tn(a, b):
    return lax.dot_general(a.astype(BF16), b.astype(BF16), (((0,), (0,)), ((), ())),
                           preferred_element_type=F32)


def _rms(x):
    return x * lax.rsqrt(jnp.mean(x * x, axis=-1, keepdims=True) + EPS)


def _silu(x):
    return x * jax.nn.sigmoid(x)


def _lane_lo():
    return lax.broadcasted_iota(jnp.int32, (1, LANES), 1) < HEAD_DIM


def _same_head():
    r = lax.broadcasted_iota(jnp.int32, (LANES, LANES), 0) < HEAD_DIM
    c = lax.broadcasted_iota(jnp.int32, (LANES, LANES), 1) < HEAD_DIM
    return r == c


def _head_ones():
    return _same_head().astype(BF16)


MXU_DIM = 256


def _head_rsqrt(x):
    r = lax.broadcasted_iota(jnp.int32, (MXU_DIM, MXU_DIM), 0) >> (HEAD_DIM.bit_length() - 1)
    c = lax.broadcasted_iota(jnp.int32, (MXU_DIM, MXU_DIM), 1) >> (HEAD_DIM.bit_length() - 1)
    ss = jnp.dot((x * x).astype(BF16), (r == c).astype(BF16), preferred_element_type=F32)
    return lax.rsqrt(ss * (1.0 / HEAD_DIM) + EPS)


def _max_sq_head_norm(x):
    bd = _head_ones()
    best = None
    for j in range(x.shape[1] // LANES):
        xf = x[:, LANES * j:LANES * (j + 1)].astype(F32)
        m = jnp.max(jnp.dot((xf * xf).astype(BF16), bd, preferred_element_type=F32))
        best = m if best is None else jnp.maximum(best, m)
    return best


def _norm_bounds(gq, gk):
    q_sq = HEAD_DIM * Q_SCALE * Q_SCALE * jnp.max(gq.astype(F32) ** 2)
    k_sq = HEAD_DIM * jnp.max(gk.astype(F32) ** 2)
    return jnp.stack([q_sq, k_sq])


def _rope128(x, cos, sin_signed):
    lane = lax.broadcasted_iota(jnp.int32, (1, LANES), 1)
    first = (lane & (HEAD_DIM // 2)) == 0
    partner = jnp.where(first, pltpu.roll(x, LANES - HEAD_DIM // 2, 1), pltpu.roll(x, HEAD_DIM // 2, 1))
    return x * cos + partner * sin_signed


def _ada_kernel(c_ref, w_ref, b_ref, o_ref):
    o_ref[0] = _mm(_silu(c_ref[...]), w_ref[0]) + b_ref[0]


def _ada(cond, ada_w, ada_b):
    depth, d, n = ada_w.shape
    tn = n // 4
    return pl.pallas_call(
        _ada_kernel,
        grid=(depth, n // tn),
        in_specs=[pl.BlockSpec(cond.shape, lambda l, j: (0, 0)),
                  pl.BlockSpec((1, d, tn), lambda l, j: (l, 0, j)),
                  pl.BlockSpec((1, 1, tn), lambda l, j: (l, 0, j))],
        out_specs=pl.BlockSpec((1, cond.shape[0], tn), lambda l, j: (l, 0, j)),
        out_shape=jax.ShapeDtypeStruct((depth, cond.shape[0], n), F32),
        compiler_params=_params("parallel", "parallel"),
        name="ada_modulation",
    )(cond, ada_w, ada_b.reshape(depth, 1, n))


RAW, NORM_Q, NORM_K, SCALE_Q = 0, 1, 2, 3


def _inproj_kernel(*refs, plan, rope, q_scale):
    x_ref, sh_ref, sc_ref, ng_ref, w_ref, gq_ref, gk_ref = refs[:7]
    pos = 7
    if rope:
        cos_ref, sin_ref = refs[7:9]
        pos = 9
    outs = refs[pos:]
    x = x_ref[0]
    h = _rms(x) * ng_ref[...] * (1.0 + sc_ref[0]) + sh_ref[0]
    p = _mm(h, w_ref[...])
    inv_rms = {}
    for j, (op, oi, dj) in enumerate(plan):
        chunk = p[:, LANES * j:LANES * (j + 1)]
        if op in (NORM_Q, NORM_K):
            g = gq_ref[...] if op == NORM_Q else gk_ref[...]
            j0 = j - j % 2
            if j0 not in inv_rms:
                inv_rms[j0] = _head_rsqrt(p[:, LANES * j0:LANES * (j0 + 2)])
            chunk = chunk * inv_rms[j0][:, LANES * (j - j0):LANES * (j - j0 + 1)] * g
            if rope:
                chunk = _rope128(chunk, cos_ref[...], sin_ref[...])
            if op == NORM_Q:
                chunk = chunk * q_scale
        elif op == SCALE_Q:
            chunk = chunk * (HEAD_DIM ** -0.5)
        if len(outs[oi].shape) == 4:
            outs[oi][0, dj] = chunk.astype(outs[oi].dtype)
        else:
            outs[oi][0, :, LANES * dj:LANES * (dj + 1)] = chunk.astype(outs[oi].dtype)


def _inproj(x, sh, sc, ng, w, gq, gk, rope_tabs, plan, out_widths, out_dtypes, q_scale, tm, name):
    b, t, d = x.shape
    n = w.shape[1]
    rope = rope_tabs is not None
    in_specs = [pl.BlockSpec((1, tm, d), lambda i, j: (i, j, 0)),
                pl.BlockSpec((1, 1, d), lambda i, j: (i, 0, 0)),
                pl.BlockSpec((1, 1, d), lambda i, j: (i, 0, 0)),
                pl.BlockSpec((1, d), lambda i, j: (0, 0)),
                pl.BlockSpec((d, n), lambda i, j: (0, 0)),
                pl.BlockSpec((1, LANES), lambda i, j: (0, 0)),
                pl.BlockSpec((1, LANES), lambda i, j: (0, 0))]
    args = [x, sh, sc, ng, w, gq, gk]
    if rope:
        in_specs += [pl.BlockSpec((tm, LANES), lambda i, j: (j, 0))] * 2
        args += list(rope_tabs)
    return pl.pallas_call(
        functools.partial(_inproj_kernel, plan=plan, rope=rope, q_scale=q_scale),
        grid=(b, t // tm),
        in_specs=in_specs,
        out_specs=[pl.BlockSpec((1, tm, wd), lambda i, j: (i, j, 0)) if wd > 0 else
                   pl.BlockSpec((1, -wd // LANES, tm, LANES), lambda i, j: (i, 0, j, 0)) for wd in out_widths],
        out_shape=[jax.ShapeDtypeStruct((b, t, wd) if wd > 0 else (b, -wd // LANES, t, LANES), dt)
                   for wd, dt in zip(out_widths, out_dtypes)],
        compiler_params=_params("parallel", "parallel"),
        name=name,
    )(*args)


def _attn_a_kernel(*refs, latent, lq):
    if latent:
        (bnd_ref, sink_ref, q_ref, kp_ref, kc_ref, kn_ref, vp_ref, vc_ref, vn_ref, ck_ref, cv_ref, o_ref,
         cn_ref) = refs
        i = pl.program_id(1)

        @pl.when(i == 0)
        def _():
            cn_ref[0] = _max_sq_head_norm(ck_ref[0])

        k_sq = jnp.maximum(bnd_ref[1], cn_ref[0])
        last = pl.num_programs(1) - 1
        k_all = jnp.concatenate([ck_ref[0], kp_ref[0], kc_ref[0], kn_ref[0]], axis=0)
        v_all = jnp.concatenate([cv_ref[0], vp_ref[0], vc_ref[0], vn_ref[0]], axis=0)
        tc = ck_ref.shape[1]
        assert lq & (lq - 1) == 0
        r = lax.broadcasted_iota(jnp.int32, (2 * lq, lq), 0) & (lq - 1)
        c = lax.broadcasted_iota(jnp.int32, (2 * lq, lq), 1)
        piece_masks = {tc: (c >= r) & (i > 0), tc + 2 * lq: (c <= r) & (i < last)}
    else:
        bnd_ref, sink_ref, q_ref, k_ref, v_ref, o_ref = refs
        k_all = k_ref[0]
        v_all = v_ref[0]
        piece_masks = {}
        k_sq = bnd_ref[1]
    nk = k_all.shape[0]
    lo = _lane_lo()
    row = lax.broadcasted_iota(jnp.int32, (2 * lq, 1), 0)

    def masked(x, fill):
        if not piece_masks:
            return x
        pieces = [jnp.where(piece_masks[o], x[:, o:o + lq], fill) if o in piece_masks else x[:, o:o + lq]
                  for o in range(0, nk, lq)]
        return jnp.concatenate(pieces, axis=1)

    sinks = [sink_ref[h] * LOG2E for h in range(8)]
    sink_max = functools.reduce(jnp.maximum, sinks)
    bound_sq = bnd_ref[0] * k_sq
    fixed = (bound_sq <= SHIFT_LIMIT * SHIFT_LIMIT) & (sink_max <= SHIFT_LIMIT)

    def attend(softmax):
        for g in range(2):
            k2 = k_all[:, LANES * g:LANES * (g + 1)].astype(BF16)
            v2 = v_all[:, LANES * g:LANES * (g + 1)].astype(BF16)
            for jp in range(2):
                j = 2 * g + jp
                q = q_ref[0, :, LANES * j:LANES * (j + 1)]
                q2 = jnp.concatenate([jnp.where(lo, q, 0.0), jnp.where(lo, 0.0, q)], axis=0)
                sk = jnp.where(row < lq, sinks[2 * j], sinks[2 * j + 1])
                p, den = softmax(_mm_nt(q2, k2), sk)
                o2 = _mm(p, v2) / den
                o_ref[0, :, LANES * j:LANES * (j + 1)] = jnp.where(lo, o2[:lq], o2[lq:])

    @pl.when(fixed)
    def _():
        shift = jnp.maximum(jnp.sqrt(jnp.full((1, 1), bound_sq, F32)), sink_max)

        def softmax(s, sk):
            p = masked(jnp.exp2(s - shift), 0.0)
            return p, jnp.sum(p, axis=1, keepdims=True) + jnp.exp2(sk - shift)

        attend(softmax)

    @pl.when(jnp.logical_not(fixed))
    def _():
        def softmax(s, sk):
            s = masked(s, NEG_INF)
            m = jnp.maximum(jnp.max(s, axis=1, keepdims=True), sk)
            p = jnp.exp2(s - m)
            return p, jnp.sum(p, axis=1, keepdims=True) + jnp.exp2(sk - m)

        attend(softmax)


def _attn_a_ctx(qkv, sink, bounds):
    b, t, _ = qkv.shape
    return pl.pallas_call(
        functools.partial(_attn_a_kernel, latent=False, lq=t),
        grid=(b,),
        in_specs=[pl.BlockSpec(memory_space=pltpu.SMEM),
                  pl.BlockSpec(memory_space=pltpu.SMEM),
                  pl.BlockSpec((1, t, 4 * LANES), lambda i: (i, 0, 0)),
                  pl.BlockSpec((1, t, 2 * LANES), lambda i: (i, 0, 2)),
                  pl.BlockSpec((1, t, 2 * LANES), lambda i: (i, 0, 3))],
        out_specs=pl.BlockSpec((1, t, 4 * LANES), lambda i: (i, 0, 0)),
        out_shape=jax.ShapeDtypeStruct((b, t, 4 * LANES), F32),
        compiler_params=_params("parallel"),
        name="attn_a_ctx",
    )(bounds, sink, qkv, qkv, qkv)


def _attn_a_lat(qkv, ck, cv, sink, bounds):
    b, t, _ = qkv.shape
    lq = WINDOW
    nb = t // lq
    tc = ck.shape[1]

    def kv_spec(col, off):
        return pl.BlockSpec((1, lq, 2 * LANES), lambda i, j: (i, jnp.clip(j + off, 0, nb - 1), col))

    return pl.pallas_call(
        functools.partial(_attn_a_kernel, latent=True, lq=lq),
        grid=(b, nb),
        in_specs=[pl.BlockSpec(memory_space=pltpu.SMEM),
                  pl.BlockSpec(memory_space=pltpu.SMEM),
                  pl.BlockSpec((1, lq, 4 * LANES), lambda i, j: (i, j, 0)),
                  kv_spec(2, -1), kv_spec(2, 0), kv_spec(2, 1),
                  kv_spec(3, -1), kv_spec(3, 0), kv_spec(3, 1),
                  pl.BlockSpec((1, tc, 2 * LANES), lambda i, j: (i, 0, 0)),
                  pl.BlockSpec((1, tc, 2 * LANES), lambda i, j: (i, 0, 0))],
        out_specs=pl.BlockSpec((1, lq, 4 * LANES), lambda i, j: (i, j, 0)),
        out_shape=jax.ShapeDtypeStruct((b, t, 4 * LANES), F32),
        scratch_shapes=[pltpu.SMEM((1,), F32)],
        compiler_params=_params("parallel", "arbitrary"),
        name="attn_a_lat",
    )(bounds, sink, qkv, qkv, qkv, qkv, qkv, qkv, qkv, ck, cv)


def _retention_kernel(*refs, zero_init):
    if zero_init:
        lg_ref, q_ref, k_ref, v_ref, o_ref, sfin_ref, sp_ref = refs
    else:
        lg_ref, q_ref, k_ref, v_ref, s0_ref, o_ref, sfin_ref, sp_ref = refs
    pr = pl.program_id(1)
    L = RET_CHUNK
    n = q_ref.shape[1] // L
    lo = _lane_lo()
    lgf = jnp.where(lo, lg_ref[0, 2 * pr], lg_ref[0, 2 * pr + 1])
    lgb = jnp.where(lo, lg_ref[1, 2 * pr], lg_ref[1, 2 * pr + 1])
    idx = lax.broadcasted_iota(jnp.int32, (L, 1), 0).astype(F32)
    qdf = jnp.exp(lgf * (idx + 1.0))
    kdf = jnp.exp(lgf * (L - 1.0 - idx))
    cdf = jnp.exp(lgf * float(L))
    qdb = jnp.exp(lgb * (L - idx))
    kdb = jnp.exp(lgb * idx)
    cdb = jnp.exp(lgb * float(L))
    diff = (lax.broadcasted_iota(jnp.int32, (L, L), 0) - lax.broadcasted_iota(jnp.int32, (L, L), 1)).astype(F32)

    def decay(h):
        f = jnp.exp(lg_ref[0, 2 * pr + h] * jnp.maximum(diff, 0.0))
        bwd = jnp.exp(lg_ref[1, 2 * pr + h] * jnp.maximum(-diff, 0.0))
        return jnp.where(diff > 0, f, jnp.where(diff < 0, bwd, 2.0))

    d0 = decay(0)
    d1 = decay(1)
    blk = _same_head()

    if zero_init:
        sf0 = jnp.zeros((LANES, LANES), F32)
        sb0 = jnp.zeros((LANES, LANES), F32)
    else:
        sf0 = s0_ref[0, 0, 0]
        sb0 = s0_ref[0, 0, 1]

    qd = jnp.concatenate([qdf, qdb], axis=1)
    kd = jnp.concatenate([kdf, kdb], axis=1)
    blk2 = jnp.concatenate([blk, blk], axis=0)
    unroll = math.gcd(n, RET_UNROLL)

    def chunk(c):
        return pl.ds(pl.multiple_of(c * L, L), L)

    def kv_body(c, carry):
        k = k_ref[0, chunk(c), :]
        kv = _mm_tn(jnp.concatenate([k, k], axis=1) * kd, v_ref[0, chunk(c), :])
        sp_ref[c] = jnp.where(blk2, kv, 0.0)
        return carry

    lax.fori_loop(0, n, kv_body, 0, unroll=unroll)

    def f_body(c, sf):
        kv = sp_ref[c, :LANES, :]
        sp_ref[c, :LANES, :] = sf
        return sf * cdf + kv

    def b_body(i, sb):
        c = n - 1 - i
        kv = sp_ref[c, LANES:, :]
        sp_ref[c, LANES:, :] = sb
        return sb * cdb + kv

    sfin_ref[0, 0, 0] = lax.fori_loop(0, n, f_body, sf0)
    sfin_ref[0, 0, 1] = lax.fori_loop(0, n, b_body, sb0)

    def o_body(c, carry):
        q = q_ref[0, chunk(c), :]
        v = v_ref[0, chunk(c), :]
        s = _mm_nt(jnp.concatenate([jnp.where(lo, q, 0.0), jnp.where(lo, 0.0, q)], axis=0), k_ref[0, chunk(c), :])
        sd = jnp.concatenate([s[:L] * d0, s[L:] * d1], axis=1)
        v2 = jnp.concatenate([jnp.where(lo, v, 0.0), jnp.where(lo, 0.0, v)], axis=0)
        o_ref[0, chunk(c), :] = _mm(sd, v2) + _mm(jnp.concatenate([q, q], axis=1) * qd, sp_ref[c])
        return carry

    lax.fori_loop(0, n, o_body, 0, unroll=unroll)


def _retention(qkv, log_g, s0):
    b, t, _ = qkv.shape
    npair = 4
    in_specs = [pl.BlockSpec(memory_space=pltpu.SMEM),
                pl.BlockSpec((1, t, LANES), lambda i, p: (i, 0, p)),
                pl.BlockSpec((1, t, LANES), lambda i, p: (i, 0, npair + p)),
                pl.BlockSpec((1, t, LANES), lambda i, p: (i, 0, 2 * npair + p))]
    args = [log_g, qkv, qkv, qkv]
    if s0 is not None:
        in_specs.append(pl.BlockSpec((1, 1, 2, LANES, LANES), lambda i, p: (i, p, 0, 0, 0)))
        args.append(s0)
    return pl.pallas_call(
        functools.partial(_retention_kernel, zero_init=s0 is None),
        grid=(b, npair),
        in_specs=in_specs,
        out_specs=[pl.BlockSpec((1, t, LANES), lambda i, p: (i, 0, p)),
                   pl.BlockSpec((1, 1, 2, LANES, LANES), lambda i, p: (i, p, 0, 0, 0))],
        out_shape=[jax.ShapeDtypeStruct((b, t, npair * LANES), F32),
                   jax.ShapeDtypeStruct((b, npair, 2, LANES, LANES), F32)],
        scratch_shapes=[pltpu.VMEM((t // RET_CHUNK, 2 * LANES, LANES), F32)],
        compiler_params=_params("parallel", "parallel"),
        name="retention",
    )(*args)


S5_CHUNK = 16
S5_GROUPS = 4


def _s5_kernel(u_ref, m_ref, w_ref, v_ref, a_ref, h0_ref, y_ref, fin_ref, z_ref, sf_ref, sb_ref, *, nb):
    rows = u_ref.shape[2]
    n = rows // nb
    lo = _lane_lo()
    nbg = nb // SUBLANES
    for gi in range(u_ref.shape[0]):
        u = jnp.concatenate([u_ref[gi, t2] for t2 in range(u_ref.shape[1])], axis=1).astype(BF16)
        z_ref[...] = jnp.dot(u, w_ref[gi], preferred_element_type=F32)
        are = jnp.broadcast_to(a_ref[gi, 0:1, :], (SUBLANES, LANES))
        aim = jnp.broadcast_to(a_ref[gi, 1:2, :], (SUBLANES, LANES))

        def step(i, carry):
            out = []
            for bg in range(nbg):
                hre, him = carry[bg]
                up = pl.ds(pl.multiple_of(i * nb + bg * SUBLANES, SUBLANES), SUBLANES)
                dn = pl.ds(pl.multiple_of((n - 1 - i) * nb + bg * SUBLANES, SUBLANES), SUBLANES)
                sf_ref[up, :LANES] = hre
                sf_ref[up, LANES:] = him
                sb_ref[dn, :LANES] = hre
                sb_ref[dn, LANES:] = him
                zre = jnp.where(lo, z_ref[up, :LANES], z_ref[dn, :LANES])
                zim = jnp.where(lo, z_ref[up, LANES:], z_ref[dn, LANES:])
                out.append((are * hre - aim * him + zre, are * him + aim * hre + zim))
            return tuple(out)

        init = tuple((h0_ref[gi, bg * SUBLANES:(bg + 1) * SUBLANES, :LANES],
                      h0_ref[gi, bg * SUBLANES:(bg + 1) * SUBLANES, LANES:]) for bg in range(nbg))
        last = lax.fori_loop(0, n, step, init, unroll=2)
        for bg in range(nbg):
            fin_ref[gi, bg * SUBLANES:(bg + 1) * SUBLANES, :LANES] = last[bg][0]
            fin_ref[gi, bg * SUBLANES:(bg + 1) * SUBLANES, LANES:] = last[bg][1]
        y = (jnp.dot(u, m_ref[gi], preferred_element_type=F32)
             + _mm(sf_ref[...], v_ref[gi, 0]) + _mm(sb_ref[...], v_ref[gi, 1]))
        for t2 in range(y_ref.shape[1]):
            y_ref[gi, t2] = y[:, LANES * t2:LANES * (t2 + 1)]


GROUPS_PER_TILE = LANES // C_GROUP


def _lane_group():
    return lax.broadcasted_iota(jnp.int32, (1, LANES), 1) >> (C_GROUP.bit_length() - 1)


S5_TILES = S5_CHUNK * C_GROUP // LANES


def _swap_blocks(xs, grp):
    xs = list(xs)
    d = GROUPS_PER_TILE // 2
    while d >= 1:
        keep = (grp & d) == 0
        for i in range(GROUPS_PER_TILE):
            if i & d:
                continue
            lo_x, hi_x = xs[i], xs[i + d]
            xs[i] = jnp.where(keep, lo_x, pltpu.roll(hi_x, C_GROUP * d, 1))
            xs[i + d] = jnp.where(keep, pltpu.roll(lo_x, LANES - C_GROUP * d, 1), hi_x)
        d //= 2
    return xs


def _s5_pack_kernel(u_ref, o_ref):
    nb, tiles, tb, _ = u_ref.shape
    n = tb // S5_CHUNK
    grp = _lane_group()

    def per_batch(b, carry):
        for j in range(tiles):
            for t2 in range(S5_TILES):
                steps = [u_ref[b, j, pl.ds(GROUPS_PER_TILE * t2 + tl, n, stride=S5_CHUNK), :]
                         for tl in range(GROUPS_PER_TILE)]
                for q, x in enumerate(_swap_blocks(steps, grp)):
                    o_ref[GROUPS_PER_TILE * j + q, t2, pl.ds(b, n, stride=nb), :] = x
        return carry

    lax.fori_loop(0, nb, per_batch, 0)


def _s5_unpack_kernel(y_ref, o_ref):
    nb, tiles, tb, _ = o_ref.shape
    n = tb // S5_CHUNK
    grp = _lane_group()

    def per_batch(b, carry):
        for j in range(tiles):
            for t2 in range(S5_TILES):
                src = [y_ref[GROUPS_PER_TILE * j + q, t2, pl.ds(b, n, stride=nb), :] for q in range(GROUPS_PER_TILE)]
                for tl, x in enumerate(_swap_blocks(src, grp)):
                    o_ref[b, j, pl.ds(GROUPS_PER_TILE * t2 + tl, n, stride=S5_CHUNK), :] = x
        return carry

    lax.fori_loop(0, nb, per_batch, 0)


def _s5_relayout(x, b, t, cw, pack):
    g = cw // C_GROUP
    tb = max(S5_CHUNK * SUBLANES, 2048 // b)
    tok = pl.BlockSpec((b, cw // LANES, tb, LANES), lambda i: (0, 0, i, 0))
    grp = pl.BlockSpec((g, S5_TILES, tb // S5_CHUNK * b, LANES), lambda i: (0, 0, i, 0))
    tok_shape = jax.ShapeDtypeStruct((b, cw // LANES, t, LANES), F32)
    grp_shape = jax.ShapeDtypeStruct((g, S5_TILES, t // S5_CHUNK * b, LANES), F32)
    return pl.pallas_call(
        _s5_pack_kernel if pack else _s5_unpack_kernel,
        grid=(t // tb,),
        in_specs=[tok if pack else grp],
        out_specs=grp if pack else tok,
        out_shape=grp_shape if pack else tok_shape,
        compiler_params=_params("parallel"),
        name="s5_pack" if pack else "s5_unpack",
    )(x)


def _s5(u, prm, h0_re, h0_im):
    m, w, v, a = prm
    b, tiles, t, _ = u.shape
    cw = tiles * LANES
    g = cw // C_GROUP
    n = t // S5_CHUNK
    rows = n * b
    width = S5_CHUNK * C_GROUP
    ug = _s5_relayout(u, b, t, cw, True)
    if h0_re is None:
        h0 = jnp.zeros((g, b, 4 * C_STATE), F32)
    else:
        h0 = jnp.concatenate([h0_re[:, 0], h0_re[:, 1], h0_im[:, 0], h0_im[:, 1]], axis=-1).transpose(1, 0, 2)
    gs = S5_GROUPS

    def grp(shape):
        nd = len(shape)
        return pl.BlockSpec((gs,) + shape[1:], lambda i: (i,) + (0,) * (nd - 1))

    y, fin = pl.pallas_call(
        functools.partial(_s5_kernel, nb=b),
        grid=(g // gs,),
        in_specs=[grp(ug.shape), grp(m.shape), grp(w.shape), grp(v.shape), grp(a.shape), grp(h0.shape)],
        out_specs=[grp(ug.shape), grp(h0.shape)],
        out_shape=[jax.ShapeDtypeStruct(ug.shape, F32), jax.ShapeDtypeStruct(h0.shape, F32)],
        scratch_shapes=[pltpu.VMEM((rows, width), F32)] * 3,
        compiler_params=_params("parallel"),
        name="s5_scan",
    )(ug, m, w, v, a, h0)
    y = _s5_relayout(y, b, t, cw, False)
    fin = fin.reshape(g, b, 2, 2, C_STATE).transpose(2, 1, 3, 0, 4)
    return y, fin[0], fin[1]


def _s5_params(lam_re, lam_im, log_dt, b_re, b_im, c_re, c_im):
    dt = jnp.exp(log_dt)[..., None]
    mag = jnp.exp(lam_re * dt)
    a_re = mag * jnp.cos(lam_im * dt)
    a_im = mag * jnp.sin(lam_im * dt)
    den = lam_re * lam_re + lam_im * lam_im
    f_re = ((a_re - 1.0) * lam_re + a_im * lam_im) / den
    f_im = (a_im * lam_re - (a_re - 1.0) * lam_im) / den
    fb_re = f_re[..., None] * b_re[None] - f_im[..., None] * b_im[None]
    fb_im = f_re[..., None] * b_im[None] + f_im[..., None] * b_re[None]
    g = lam_re.shape[1]
    L = S5_CHUNK
    k = jnp.arange(L + 1, dtype=F32)[:, None, None, None]
    pmag = jnp.exp(lam_re * dt * k)
    p_re = pmag * jnp.cos(lam_im * dt * k)
    p_im = pmag * jnp.sin(lam_im * dt * k)
    fbt_re = fb_re.transpose(0, 1, 3, 2)
    fbt_im = fb_im.transpose(0, 1, 3, 2)

    ca_re = c_re[None, None] * p_re[:, :, :, None, :] - c_im[None, None] * p_im[:, :, :, None, :]
    ca_im = c_re[None, None] * p_im[:, :, :, None, :] + c_im[None, None] * p_re[:, :, :, None, :]
    car = ca_re.transpose(1, 2, 0, 3, 4)[:, :, None]
    cai = ca_im.transpose(1, 2, 0, 3, 4)[:, :, None]
    kern = jnp.sum(car * fbt_re[:, :, :, None, None, :] - cai * fbt_im[:, :, :, None, None, :], axis=-1)
    wid = L * C_GROUP
    fwd = kern[0, :, :, :L].reshape(g, C_GROUP, wid)
    bwd = kern[1, :, :, L - 1::-1][:, :, :L].reshape(g, C_GROUP, wid)
    rows = []
    for s in range(L):
        f = jnp.pad(fwd[:, :, :wid - s * C_GROUP], ((0, 0), (0, 0), (s * C_GROUP, 0)))
        bk = jnp.pad(bwd[:, :, (L - 1 - s) * C_GROUP:], ((0, 0), (0, 0), (0, (L - 1 - s) * C_GROUP)))
        rows.append(f + bk)
    m = jnp.stack(rows, axis=1).reshape(g, wid, wid)

    def to_state(pw_re, pw_im, d):
        re = pw_re[:, :, None, :] * fbt_re[d][None] - pw_im[:, :, None, :] * fbt_im[d][None]
        im = pw_re[:, :, None, :] * fbt_im[d][None] + pw_im[:, :, None, :] * fbt_re[d][None]
        return re.transpose(1, 0, 2, 3), im.transpose(1, 0, 2, 3)

    wf_re, wf_im = to_state(p_re[L - 1::-1, 0][:L], p_im[L - 1::-1, 0][:L], 0)
    wb_re, wb_im = to_state(p_re[:L, 1], p_im[:L, 1], 1)
    w = jnp.concatenate([wf_re, wb_re, wf_im, wb_im], axis=-1).reshape(g, wid, 4 * C_STATE)

    def from_state(car, cai):
        return car.transpose(1, 3, 0, 2), -cai.transpose(1, 3, 0, 2)

    vf_re, vf_im = from_state(ca_re[1:, 0], ca_im[1:, 0])
    vb_re, vb_im = from_state(ca_re[L:0:-1, 1], ca_im[L:0:-1, 1])
    zero = jnp.zeros_like(vf_re)
    v = jnp.stack([jnp.concatenate([vf_re, zero, vf_im, zero], axis=1),
                   jnp.concatenate([zero, vb_re, zero, vb_im], axis=1)], axis=1)
    v = v.reshape(g, 2, 4 * C_STATE, L * C_GROUP)

    a = jnp.stack([jnp.concatenate([p_re[L, 0], p_re[L, 1]], axis=-1),
                   jnp.concatenate([p_im[L, 0], p_im[L, 1]], axis=-1)], axis=1)
    return m.astype(BF16), w.astype(BF16), v.astype(BF16), a


def _diff_attn_kernel(*refs, latent, kc):
    if latent:
        bnd_ref, lam_ref, q_ref, k_ref, v_ref, ck_ref, cv_ref, o_ref, cn_ref = refs
    else:
        bnd_ref, lam_ref, q_ref, k_ref, v_ref, o_ref = refs
    lq = q_ref.shape[1]
    lo = _lane_lo()
    q = q_ref[0]
    qs = (jnp.where(lo, q, 0.0).astype(BF16), jnp.where(lo, 0.0, q).astype(BF16))

    def over_keys(step, carry):
        if latent:
            carry = step(ck_ref[0], cv_ref[0], carry)

            def body(c, carry):
                rows = pl.ds(pl.multiple_of(c * kc, kc), kc)
                return step(k_ref[0, rows, :], v_ref[0, rows, :], carry)

            return lax.fori_loop(0, k_ref.shape[1] // kc, body, carry)
        return step(k_ref[0], v_ref[0], carry)

    def finish(l0, acc0, l1, acc1):
        o_ref[0] = acc0 / l0 - lam_ref[0] * (acc1 / l1)

    def online_step(kb, vb, carry):
        kb = kb.astype(BF16)
        vb = vb.astype(BF16)
        out = []
        for qc, (m, l, acc) in zip(qs, carry):
            s = _mm_nt(qc, kb)
            m_new = jnp.maximum(m, jnp.max(s, axis=1, keepdims=True))
            alpha = jnp.exp2(m - m_new)
            p = jnp.exp2(s - m_new)
            l = alpha * l + jnp.sum(p, axis=1, keepdims=True)
            acc = alpha * acc + _mm(p, vb)
            out.append((m_new, l, acc))
        return tuple(out)

    def online():
        init = (jnp.full((lq, 1), NEG_INF, F32), jnp.zeros((lq, 1), F32), jnp.zeros((lq, LANES), F32))
        (_, l0, acc0), (_, l1, acc1) = over_keys(online_step, (init, init))
        finish(l0, acc0, l1, acc1)

    if latent:
        @pl.when(pl.program_id(2) == 0)
        def _():
            cn_ref[0] = _max_sq_head_norm(ck_ref[0])

        bound_sq = bnd_ref[0] * jnp.maximum(bnd_ref[1], cn_ref[0])
    else:
        bound_sq = bnd_ref[0] * bnd_ref[1]
    fixed = bound_sq <= SHIFT_LIMIT * SHIFT_LIMIT

    @pl.when(fixed)
    def _():
        shift = jnp.sqrt(jnp.full((1, 1), bound_sq, F32))

        def step(kb, vb, carry):
            kb = kb.astype(BF16)
            vb = vb.astype(BF16)
            out = []
            for qc, (l, acc) in zip(qs, carry):
                p = jnp.exp2(_mm_nt(qc, kb) - shift)
                out.append((l + jnp.sum(p, axis=1, keepdims=True), acc + _mm(p, vb)))
            return tuple(out)

        init = (jnp.zeros((lq, 1), F32), jnp.zeros((lq, LANES), F32))
        (l0, acc0), (l1, acc1) = over_keys(step, (init, init))
        finish(l0, acc0, l1, acc1)

    @pl.when(jnp.logical_not(fixed))
    def _():
        online()


def _diff_attn(qkv, lam, cache=None, bounds=None, lq=512, kc=2048):
    b, t, _ = qkv.shape
    nh = 4
    latent = cache is not None
    kc = min(kc, t)
    assert t % kc == 0 and t % lq == 0
    in_specs = [pl.BlockSpec(memory_space=pltpu.SMEM),
                pl.BlockSpec(memory_space=pltpu.SMEM),
                pl.BlockSpec((1, lq, LANES), lambda i, h, j: (i, j, h)),
                pl.BlockSpec((1, t, LANES), lambda i, h, j: (i, 0, nh + h)),
                pl.BlockSpec((1, t, LANES), lambda i, h, j: (i, 0, 2 * nh + h))]
    args = [bounds, lam, qkv, qkv, qkv]
    if latent:
        tc = cache[0].shape[1]
        in_specs += [pl.BlockSpec((1, tc, LANES), lambda i, h, j: (i, 0, h))] * 2
        args += list(cache)
    return pl.pallas_call(
        functools.partial(_diff_attn_kernel, latent=latent, kc=kc),
        grid=(b, nh, t // lq),
        in_specs=in_specs,
        out_specs=pl.BlockSpec((1, lq, LANES), lambda i, h, j: (i, j, h)),
        out_shape=jax.ShapeDtypeStruct((b, t, nh * LANES), F32),
        scratch_shapes=[pltpu.SMEM((1,), F32)] if latent else [],
        compiler_params=_params("parallel", "parallel", "arbitrary"),
        name="diff_attn_lat" if latent else "diff_attn_ctx",
    )(*args)


def _resid_ffn(x, mix, g1, n2g, sh2, sc2, g2, w1_ref, w3_ref, w2_ref):
    x1 = x + g1 * mix
    h = (_rms(x1) * n2g * (1.0 + sc2) + sh2).astype(BF16)
    tiles = w1_ref.shape[1] // MXU_DIM
    cuts = [MXU_DIM * (tiles * s // FFN_SPLIT) for s in range(FFN_SPLIT + 1)]
    f = None
    for c0, c1 in zip(cuts[:-1], cuts[1:]):
        a = jnp.dot(h, w1_ref[:, c0:c1], preferred_element_type=F32)
        b = jnp.dot(h, w3_ref[:, c0:c1], preferred_element_type=F32)
        part = _mm(_silu(a) * b, w2_ref[c0:c1, :])
        f = part if f is None else f + part
    return x1 + g2 * f


def _out_ab_kernel(x_ref, oa_ref, ob_ref, gb_ref, rg_ref, wo_ref, g1_ref, n2g_ref, sh2_ref, sc2_ref,
                   g2_ref, w1_ref, w3_ref, w2_ref, y_ref):
    half = wo_ref.shape[0] // 2
    ob = ob_ref[0]
    inv = jnp.concatenate([_head_rsqrt(ob[:, c0:c0 + MXU_DIM]) for c0 in range(0, half, MXU_DIM)], axis=1)
    ob = ob * inv * rg_ref[...] * _silu(gb_ref[0])
    mix = _mm(oa_ref[0], wo_ref[:half, :]) + _mm(ob, wo_ref[half:, :])
    y_ref[0] = _resid_ffn(x_ref[0], mix, g1_ref[0], n2g_ref[...], sh2_ref[0], sc2_ref[0], g2_ref[0],
                          w1_ref, w3_ref, w2_ref)


def _out_cd_kernel(x_ref, ys_ref, u_ref, od_ref, dsk_ref, gw_ref, gbias_ref, sg_ref, wo_ref,
                   g1_ref, n2g_ref, sh2_ref, sc2_ref, g2_ref, w1_ref, w3_ref, w2_ref, y_ref, *, od_scale):
    half = wo_ref.shape[0] // 2
    tiles = range(ys_ref.shape[1])
    y = (jnp.concatenate([ys_ref[0, j] for j in tiles], axis=1)
         + dsk_ref[...] * jnp.concatenate([u_ref[0, j] for j in tiles], axis=1))
    g = jax.nn.gelu(y)
    oc = g * jax.nn.sigmoid(_mm(g, gw_ref[...]) + gbias_ref[...])
    od = jnp.concatenate([_rms(od_ref[0, :, c0:c0 + LANES]) * sg_ref[...] * od_scale
                          for c0 in range(0, half, LANES)], axis=1)
    mix = _mm(oc, wo_ref[:half, :]) + _mm(od, wo_ref[half:, :])
    y_ref[0] = _resid_ffn(x_ref[0], mix, g1_ref[0], n2g_ref[...], sh2_ref[0], sc2_ref[0], g2_ref[0],
                          w1_ref, w3_ref, w2_ref)


def _const_spec(shape):
    nd = len(shape)
    return pl.BlockSpec(shape, lambda i, j: (0,) * nd, pipeline_mode=pl.Buffered(1))


def _out_call(kern, name, x, tok_args, const_args, mod, ffn, tm):
    b, t, d = x.shape
    g1, sh2, sc2, g2 = mod
    n2g, w1, w3, w2 = ffn

    def tok(a):
        if a.ndim == 4:
            return pl.BlockSpec((1, a.shape[1], tm, LANES), lambda i, j: (i, 0, j, 0))
        return pl.BlockSpec((1, tm, a.shape[2]), lambda i, j: (i, j, 0))

    def per_batch(a):
        return pl.BlockSpec((1, 1, d), lambda i, j: (i, 0, 0))

    args = [x] + list(tok_args) + list(const_args) + [g1, n2g, sh2, sc2, g2, w1, w3, w2]
    in_specs = ([tok(x)] + [tok(a) for a in tok_args] + [_const_spec(a.shape) for a in const_args]
                + [per_batch(g1), _const_spec(n2g.shape), per_batch(sh2), per_batch(sc2), per_batch(g2),
                   _const_spec(w1.shape), _const_spec(w3.shape), _const_spec(w2.shape)])
    return pl.pallas_call(
        kern,
        grid=(b, t // tm),
        in_specs=in_specs,
        out_specs=tok(x),
        out_shape=jax.ShapeDtypeStruct(x.shape, F32),
        compiler_params=_params("parallel", "parallel"),
        name=name,
    )(*args)


def _ab_plan():
    plan = [(NORM_Q, 0, j) for j in range(4)] + [(NORM_K, 0, 4), (NORM_K, 0, 5), (RAW, 0, 6), (RAW, 0, 7)]
    plan += [(SCALE_Q, 1, j) for j in range(4)] + [(RAW, 1, 4 + j) for j in range(8)]
    plan += [(RAW, 2, j) for j in range(4)]
    return tuple(plan)


def _dup_kv_cols(w):
    q, kv, rest = w[:, :4 * LANES], w[:, 4 * LANES:6 * LANES], w[:, 6 * LANES:]
    kv = jnp.repeat(kv.reshape(w.shape[0], 4, 1, HEAD_DIM), 2, axis=2).reshape(w.shape[0], 4 * LANES)
    return jnp.concatenate([q, kv, rest], axis=1)


def _dup_heads(x):
    return jnp.repeat(x[..., :, None, :], 2, axis=-2).reshape(x.shape[:-2] + (4 * HEAD_DIM,))


def _cd_plan():
    plan = [(RAW, 0, j) for j in range(4)]
    plan += [(NORM_Q, 1, j) for j in range(4)] + [(NORM_K, 1, 4 + j) for j in range(4)]
    plan += [(RAW, 1, 8 + j) for j in range(4)]
    return tuple(plan)


AB_PLAN = _ab_plan()
CD_PLAN = _cd_plan()


def _rope_tables(t):
    rows = t // GRID_W
    r = jnp.repeat(jnp.arange(rows, dtype=F32), GRID_W)
    col = jnp.tile(jnp.arange(GRID_W, dtype=F32), rows)
    n_freq = HEAD_DIM // 4
    inv = ROPE_BASE ** (-jnp.arange(n_freq, dtype=F32) / n_freq)
    ang = jnp.concatenate([r[:, None] * inv, col[:, None] * inv], axis=-1)
    cos, sin = jnp.cos(ang), jnp.sin(ang)
    cos128 = jnp.tile(cos, (1, 4))
    sin128 = jnp.tile(jnp.concatenate([-sin, sin], axis=-1), (1, 2))
    return cos128, sin128


def _tile2(g):
    return jnp.tile(g.astype(F32), 2).reshape(1, LANES)


def kernel(x_prompt, x_sample, c, c_ctx, cache_k_a, cache_v_a, state_ret, state_ssm_re, state_ssm_im,
           cache_k_d, cache_v_d, ada_w, ada_b, norm1_g, norm2_g, ffn_w1, ffn_w3, ffn_w2,
           ab_w_in, ab_w_out, a_q_norm, a_k_norm, a_sink, ret_decay, ret_norm,
           cd_w_in, cd_w_out, ssm_lambda_re, ssm_lambda_im, ssm_log_dt, ssm_b_re, ssm_b_im,
           ssm_c_re, ssm_c_im, ssm_d, ssm_glu_w, ssm_glu_b, d_q_norm, d_k_norm, d_lambda, d_subln):
    nbc, tcx, d = x_prompt.shape
    nbl, tl, _ = x_sample.shape
    depth = ada_w.shape[0]
    rope_tabs = _rope_tables(tl)

    ncond = -(-(nbl + 1) // SUBLANES) * SUBLANES
    cond = jnp.zeros((ncond, d), F32).at[:nbl].set(c).at[nbl].set(c_ctx)
    mods = _ada(cond, ada_w, ada_b)

    def mod_split(l, lo, hi):
        m = mods[l, lo:hi].reshape(hi - lo, 1, 6, d)
        return [m[:, :, k] for k in range(6)]

    yp = x_prompt.reshape(1, nbc * tcx, d)
    ys = x_sample
    new_ka, new_va, new_ret, new_sr, new_si, new_kd, new_vd = [], [], [], [], [], [], []
    for l in range(depth):
        j = l // 2
        ffn = (norm2_g[l].reshape(1, d), ffn_w1[l].astype(BF16), ffn_w3[l].astype(BF16), ffn_w2[l].astype(BF16))
        n1g = norm1_g[l].reshape(1, d)
        sh1c, sc1c, g1c, sh2c, sc2c, g2c = mod_split(l, nbl, nbl + 1)
        sh1l, sc1l, g1l, sh2l, sc2l, g2l = mod_split(l, 0, nbl)
        if l % 2 == 0:
            w_in = _dup_kv_cols(ab_w_in[j]).astype(BF16)
            w_out = ab_w_out[j].astype(BF16)
            gq, gk = _tile2(a_q_norm[j]), _tile2(a_k_norm[j])
            rg = jnp.tile(ret_norm[j].astype(F32), ab_w_out.shape[1] // 2 // HEAD_DIM).reshape(1, -1)
            log_g = jax.nn.log_sigmoid(ret_decay[j].astype(F32))
            widths = (8 * LANES, 12 * LANES, 4 * LANES)
            a_scale = Q_SCALE
            qkv_a, qkv_b, gb = _inproj(yp, sh1c, sc1c, n1g, w_in, gq, gk, None, AB_PLAN, widths,
                                       (F32, F32, F32), a_scale, TOKEN_TILE, "inproj_ab_ctx")
            qkv_a = qkv_a.reshape(nbc, tcx, -1)
            bounds = _norm_bounds(a_q_norm[j], a_k_norm[j])
            oa = _attn_a_ctx(qkv_a, a_sink[j], bounds)
            ob, s_ret = _retention(qkv_b.reshape(nbc, tcx, -1), log_g, None)
            yp = _out_call(_out_ab_kernel, "out_ab_ctx", yp,
                           [oa.reshape(1, nbc * tcx, -1), ob.reshape(1, nbc * tcx, -1), gb],
                           [rg, w_out], (g1c, sh2c, sc2c, g2c), ffn, TOKEN_TILE)
            new_ka.append(qkv_a[:, :, 4 * LANES:6 * LANES].reshape(nbc, tcx, 2, 2, HEAD_DIM)[:, :, :, 0])
            new_va.append(qkv_a[:, :, 6 * LANES:8 * LANES].reshape(nbc, tcx, 2, 2, HEAD_DIM)[:, :, :, 0])
            s_ret = jnp.stack([s_ret[..., :HEAD_DIM, :HEAD_DIM], s_ret[..., HEAD_DIM:, HEAD_DIM:]], axis=2)
            new_ret.append(s_ret.transpose(0, 3, 1, 2, 4, 5).reshape(nbc, 2, 8, HEAD_DIM, HEAD_DIM))
            qkv_a, qkv_b, gb = _inproj(ys, sh1l, sc1l, n1g, w_in, gq, gk, rope_tabs, AB_PLAN, widths,
                                       (BF16, F32, F32), a_scale, TOKEN_TILE, "inproj_ab_lat")
            ck = _dup_heads(cache_k_a[:, j]).astype(BF16)
            cv = _dup_heads(cache_v_a[:, j]).astype(BF16)
            oa = _attn_a_lat(qkv_a, ck, cv, a_sink[j], bounds)
            s0 = state_ret[:, j].reshape(nbl, 2, 4, 2, HEAD_DIM, HEAD_DIM)
            z = jnp.zeros_like(s0[:, :, :, 0])
            s0 = jnp.concatenate([jnp.concatenate([s0[:, :, :, 0], z], axis=-1),
                                  jnp.concatenate([z, s0[:, :, :, 1]], axis=-1)], axis=-2)
            s0 = s0.transpose(0, 2, 1, 3, 4)
            ob, _ = _retention(qkv_b, log_g, s0)
            ys = _out_call(_out_ab_kernel, "out_ab_lat", ys, [oa, ob, gb], [rg, w_out],
                           (g1l, sh2l, sc2l, g2l), ffn, TOKEN_TILE)
        else:
            lam_init = 0.8 - 0.6 * math.exp(-0.3 * l)
            w_in = cd_w_in[j].astype(BF16)
            w_out = cd_w_out[j].astype(BF16)
            gq, gk = _tile2(d_q_norm[j]), _tile2(d_k_norm[j])
            lp = d_lambda[j].astype(F32)
            lam = (jnp.exp(jnp.sum(lp[0] * lp[1])) - jnp.exp(jnp.sum(lp[2] * lp[3])) + lam_init).reshape(1)
            prm = _s5_params(ssm_lambda_re[j], ssm_lambda_im[j], ssm_log_dt[j], ssm_b_re[j], ssm_b_im[j],
                             ssm_c_re[j], ssm_c_im[j])
            consts =[ssm_d[j].reshape(1, -1), ssm_glu_w[j].astype(BF16), ssm_glu_b[j].reshape(1, -1),
                      d_subln[j].reshape(1, LANES), w_out]
            kern = functools.partial(_out_cd_kernel, od_scale=1.0 - lam_init)
            widths = (-4 * LANES, 12 * LANES)
            d_scale = Q_SCALE
            u, qkv_d = _inproj(yp, sh1c, sc1c, n1g, w_in, gq, gk, None, CD_PLAN, widths,
                               (F32, F32), d_scale, TOKEN_TILE, "inproj_cd_ctx")
            u_b = u.reshape(-1, nbc, tcx, LANES).transpose(1, 0, 2, 3)
            y, fin_re, fin_im = _s5(u_b, prm, None, None)
            y = y.transpose(1, 0, 2, 3).reshape(u.shape)
            qkv_d = qkv_d.reshape(nbc, tcx, -1)
            d_bounds = _norm_bounds(d_q_norm[j], d_k_norm[j])
            od = _diff_attn(qkv_d, lam, None, d_bounds, lq=tcx)
            yp = _out_call(kern, "out_cd_ctx", yp, [y, u, od.reshape(1, nbc * tcx, -1)],
                           consts, (g1c, sh2c, sc2c, g2c), ffn, TOKEN_TILE)
            new_sr.append(fin_re)
            new_si.append(fin_im)
            new_kd.append(qkv_d[:, :, 4 * LANES:8 * LANES].reshape(nbc, tcx, 4, 2, HEAD_DIM))
            new_vd.append(qkv_d[:, :, 8 * LANES:].reshape(nbc, tcx, 4, 2 * HEAD_DIM))
            u, qkv_d = _inproj(ys, sh1l, sc1l, n1g, w_in, gq, gk, rope_tabs, CD_PLAN, widths,
                               (F32, BF16), d_scale, TOKEN_TILE, "inproj_cd_lat")
            y, _, _ = _s5(u, prm, state_ssm_re[:, j], state_ssm_im[:, j])
            ckd = cache_k_d[:, j].reshape(nbl, -1, 4 * LANES)
            cvd = cache_v_d[:, j].reshape(nbl, -1, 4 * LANES)
            od = _diff_attn(qkv_d, lam, (ckd, cvd), d_bounds)
            ys = _out_call(kern, "out_cd_lat", ys, [y, u, od], consts, (g1l, sh2l, sc2l, g2l), ffn, TOKEN_TILE)
    return (yp.reshape(nbc, tcx, d), ys,
            jnp.stack(new_ka, axis=1), jnp.stack(new_va, axis=1), jnp.stack(new_ret, axis=1),
            jnp.stack(new_sr, axis=1), jnp.stack(new_si, axis=1),
            jnp.stack(new_kd, axis=1), jnp.stack(new_vd, axis=1))
```

```python
import functools
import math

import jax
import jax.numpy as jnp
from jax import lax
from jax.experimental import pallas as pl
from jax.experimental.pallas import tpu as pltpu

F32 = jnp.float32
BF16 = jnp.bfloat16

LANES = 128
SUBLANES = 8
HEAD_DIM = 64
GRID_W = 64
ROPE_BASE = 10000.0
WINDOW = 128
RET_CHUNK = 128
RET_UNROLL = 8
FFN_SPLIT = 2
TOKEN_TILE = 512
C_GROUP = 16
C_STATE = 64
NEG_INF = -1e30
EPS = 1e-6
LOG2E = math.log2(math.e)
Q_SCALE = HEAD_DIM ** -0.5 * LOG2E
SHIFT_LIMIT = 50.0
VMEM_LIMIT = 56 * 1024 * 1024


def _params(*sem):
    return pltpu.CompilerParams(dimension_semantics=sem, vmem_limit_bytes=VMEM_LIMIT)


def _mm(a, b):
    return jnp.dot(a.astype(BF16), b.astype(BF16), preferred_element_type=F32)


def _mm_nt(a, b):
    return lax.dot_general(a.astype(BF16), b.astype(BF16), (((1,), (1,)), ((), ())),
                           preferred_element_type=F32)


def _mm_tn(a, b):
    return lax.dot_general(a.astype(BF16), b.astype(BF16), (((0,), (0,)), ((), ())),
                           preferred_element_type=F32)


def _rms(x):
    return x * lax.rsqrt(jnp.mean(x * x, axis=-1, keepdims=True) + EPS)


def _silu(x):
    return x * jax.nn.sigmoid(x)


def _lane_lo():
    return lax.broadcasted_iota(jnp.int32, (1, LANES), 1) < HEAD_DIM


def _same_head():
    r = lax.broadcasted_iota(jnp.int32, (LANES, LANES), 0) < HEAD_DIM
    c = lax.broadcasted_iota(jnp.int32, (LANES, LANES), 1) < HEAD_DIM
    return r == c


def _head_ones():
    return _same_head().astype(BF16)


MXU_DIM = 256


def _head_rsqrt(x):
    r = lax.broadcasted_iota(jnp.int32, (MXU_DIM, MXU_DIM), 0) >> (HEAD_DIM.bit_length() - 1)
    c = lax.broadcasted_iota(jnp.int32, (MXU_DIM, MXU_DIM), 1) >> (HEAD_DIM.bit_length() - 1)
    ss = jnp.dot((x * x).astype(BF16), (r == c).astype(BF16), preferred_element_type=F32)
    return lax.rsqrt(ss * (1.0 / HEAD_DIM) + EPS)


def _max_sq_head_norm(x):
    bd = _head_ones()
    best = None
    for j in range(x.shape[1] // LANES):
        xf = x[:, LANES * j:LANES * (j + 1)].astype(F32)
        m = jnp.max(jnp.dot((xf * xf).astype(BF16), bd, preferred_element_type=F32))
        best = m if best is None else jnp.maximum(best, m)
    return best


def _norm_bounds(gq, gk):
    q_sq = HEAD_DIM * Q_SCALE * Q_SCALE * jnp.max(gq.astype(F32) ** 2)
    k_sq = HEAD_DIM * jnp.max(gk.astype(F32) ** 2)
    return jnp.stack([q_sq, k_sq])


def _rope128(x, cos, sin_signed):
    lane = lax.broadcasted_iota(jnp.int32, (1, LANES), 1)
    first = (lane & (HEAD_DIM // 2)) == 0
    partner = jnp.where(first, pltpu.roll(x, LANES - HEAD_DIM // 2, 1), pltpu.roll(x, HEAD_DIM // 2, 1))
    return x * cos + partner * sin_signed


def _ada_kernel(c_ref, w_ref, b_ref, o_ref):
    o_ref[0] = _mm(_silu(c_ref[...]), w_ref[0]) + b_ref[0]


def _ada(cond, ada_w, ada_b):
    depth, d, n = ada_w.shape
    tn = n // 4
    return pl.pallas_call(
        _ada_kernel,
        grid=(depth, n // tn),
        in_specs=[pl.BlockSpec(cond.shape, lambda l, j: (0, 0)),
                  pl.BlockSpec((1, d, tn), lambda l, j: (l, 0, j)),
                  pl.BlockSpec((1, 1, tn), lambda l, j: (l, 0, j))],
        out_specs=pl.BlockSpec((1, cond.shape[0], tn), lambda l, j: (l, 0, j)),
        out_shape=jax.ShapeDtypeStruct((depth, cond.shape[0], n), F32),
        compiler_params=_params("parallel", "parallel"),
        name="ada_modulation",
    )(cond, ada_w, ada_b.reshape(depth, 1, n))


RAW, NORM_Q, NORM_K, SCALE_Q = 0, 1, 2, 3


def _inproj_kernel(*refs, plan, rope, q_scale):
    x_ref, sh_ref, sc_ref, ng_ref, w_ref, gq_ref, gk_ref = refs[:7]
    pos = 7
    if rope:
        cos_ref, sin_ref = refs[7:9]
        pos = 9
    outs = refs[pos:]
    x = x_ref[0]
    h = _rms(x) * ng_ref[...] * (1.0 + sc_ref[0]) + sh_ref[0]
    p = _mm(h, w_ref[...])
    inv_rms = {}
    for j, (op, oi, dj) in enumerate(plan):
        chunk = p[:, LANES * j:LANES * (j + 1)]
        if op in (NORM_Q, NORM_K):
            g = gq_ref[...] if op == NORM_Q else gk_ref[...]
            j0 = j - j % 2
            if j0 not in inv_rms:
                inv_rms[j0] = _head_rsqrt(p[:, LANES * j0:LANES * (j0 + 2)])
            chunk = chunk * inv_rms[j0][:, LANES * (j - j0):LANES * (j - j0 + 1)] * g
            if rope:
                chunk = _rope128(chunk, cos_ref[...], sin_ref[...])
            if op == NORM_Q:
                chunk = chunk * q_scale
        elif op == SCALE_Q:
            chunk = chunk * (HEAD_DIM ** -0.5)
        if len(outs[oi].shape) == 4:
            outs[oi][0, dj] = chunk.astype(outs[oi].dtype)
        else:
            outs[oi][0, :, LANES * dj:LANES * (dj + 1)] = chunk.astype(outs[oi].dtype)


def _inproj(x, sh, sc, ng, w, gq, gk, rope_tabs, plan, out_widths, out_dtypes, q_scale, tm, name):
    b, t, d = x.shape
    n = w.shape[1]
    rope = rope_tabs is not None
    in_specs = [pl.BlockSpec((1, tm, d), lambda i, j: (i, j, 0)),
                pl.BlockSpec((1, 1, d), lambda i, j: (i, 0, 0)),
                pl.BlockSpec((1, 1, d), lambda i, j: (i, 0, 0)),
                pl.BlockSpec((1, d), lambda i, j: (0, 0)),
                pl.BlockSpec((d, n), lambda i, j: (0, 0)),
                pl.BlockSpec((1, LANES), lambda i, j: (0, 0)),
                pl.BlockSpec((1, LANES), lambda i, j: (0, 0))]
    args = [x, sh, sc, ng, w, gq, gk]
    if rope:
        in_specs += [pl.BlockSpec((tm, LANES), lambda i, j: (j, 0))] * 2
        args += list(rope_tabs)
    return pl.pallas_call(
        functools.partial(_inproj_kernel, plan=plan, rope=rope, q_scale=q_scale),
        grid=(b, t // tm),
        in_specs=in_specs,
        out_specs=[pl.BlockSpec((1, tm, wd), lambda i, j: (i, j, 0)) if wd > 0 else
                   pl.BlockSpec((1, -wd // LANES, tm, LANES), lambda i, j: (i, 0, j, 0)) for wd in out_widths],
        out_shape=[jax.ShapeDtypeStruct((b, t, wd) if wd > 0 else (b, -wd // LANES, t, LANES), dt)
                   for wd, dt in zip(out_widths, out_dtypes)],
        compiler_params=_params("parallel", "parallel"),
        name=name,
    )(*args)


def _attn_a_kernel(*refs, latent, lq):
    if latent:
        (bnd_ref, sink_ref, q_ref, kp_ref, kc_ref, kn_ref, vp_ref, vc_ref, vn_ref, ck_ref, cv_ref, o_ref,
         cn_ref) = refs
        i = pl.program_id(1)

        @pl.when(i == 0)
        def _():
            cn_ref[0] = _max_sq_head_norm(ck_ref[0])

        k_sq = jnp.maximum(bnd_ref[1], cn_ref[0])
        last = pl.num_programs(1) - 1
        k_all = jnp.concatenate([ck_ref[0], kp_ref[0], kc_ref[0], kn_ref[0]], axis=0)
        v_all = jnp.concatenate([cv_ref[0], vp_ref[0], vc_ref[0], vn_ref[0]], axis=0)
        tc = ck_ref.shape[1]
        assert lq & (lq - 1) == 0
        r = lax.broadcasted_iota(jnp.int32, (2 * lq, lq), 0) & (lq - 1)
        c = lax.broadcasted_iota(jnp.int32, (2 * lq, lq), 1)
        piece_masks = {tc: (c >= r) & (i > 0), tc + 2 * lq: (c <= r) & (i < last)}
    else:
        bnd_ref, sink_ref, q_ref, k_ref, v_ref, o_ref = refs
        k_all = k_ref[0]
        v_all = v_ref[0]
        piece_masks = {}
        k_sq = bnd_ref[1]
    nk = k_all.shape[0]
    lo = _lane_lo()
    row = lax.broadcasted_iota(jnp.int32, (2 * lq, 1), 0)

    def masked(x, fill):
        if not piece_masks:
            return x
        pieces = [jnp.where(piece_masks[o], x[:, o:o + lq], fill) if o in piece_masks else x[:, o:o + lq]
                  for o in range(0, nk, lq)]
        return jnp.concatenate(pieces, axis=1)

    sinks = [sink_ref[h] * LOG2E for h in range(8)]
    sink_max = functools.reduce(jnp.maximum, sinks)
    bound_sq = bnd_ref[0] * k_sq
    fixed = (bound_sq <= SHIFT_LIMIT * SHIFT_LIMIT) & (sink_max <= SHIFT_LIMIT)

    def attend(softmax):
        for g in range(2):
            k2 = k_all[:, LANES * g:LANES * (g + 1)].astype(BF16)
            v2 = v_all[:, LANES * g:LANES * (g + 1)].astype(BF16)
            for jp in range(2):
                j = 2 * g + jp
                q = q_ref[0, :, LANES * j:LANES * (j + 1)]
                q2 = jnp.concatenate([jnp.where(lo, q, 0.0), jnp.where(lo, 0.0, q)], axis=0)
                sk = jnp.where(row < lq, sinks[2 * j], sinks[2 * j + 1])
                p, den = softmax(_mm_nt(q2, k2), sk)
                o2 = _mm(p, v2) / den
                o_ref[0, :, LANES * j:LANES * (j + 1)] = jnp.where(lo, o2[:lq], o2[lq:])

    @pl.when(fixed)
    def _():
        shift = jnp.maximum(jnp.sqrt(jnp.full((1, 1), bound_sq, F32)), sink_max)

        def softmax(s, sk):
            p = masked(jnp.exp2(s - shift), 0.0)
            return p, jnp.sum(p, axis=1, keepdims=True) + jnp.exp2(sk - shift)

        attend(softmax)

    @pl.when(jnp.logical_not(fixed))
    def _():
        def softmax(s, sk):
            s = masked(s, NEG_INF)
            m = jnp.maximum(jnp.max(s, axis=1, keepdims=True), sk)
            p = jnp.exp2(s - m)
            return p, jnp.sum(p, axis=1, keepdims=True) + jnp.exp2(sk - m)

        attend(softmax)


def _attn_a_ctx(qkv, sink, bounds):
    b, t, _ = qkv.shape
    return pl.pallas_call(
        functools.partial(_attn_a_kernel, latent=False, lq=t),
        grid=(b,),
        in_specs=[pl.BlockSpec(memory_space=pltpu.SMEM),
                  pl.BlockSpec(memory_space=pltpu.SMEM),
                  pl.BlockSpec((1, t, 4 * LANES), lambda i: (i, 0, 0)),
                  pl.BlockSpec((1, t, 2 * LANES), lambda i: (i, 0, 2)),
                  pl.BlockSpec((1, t, 2 * LANES), lambda i: (i, 0, 3))],
        out_specs=pl.BlockSpec((1, t, 4 * LANES), lambda i: (i, 0, 0)),
        out_shape=jax.ShapeDtypeStruct((b, t, 4 * LANES), F32),
        compiler_params=_params("parallel"),
        name="attn_a_ctx",
    )(bounds, sink, qkv, qkv, qkv)


def _attn_a_lat(qkv, ck, cv, sink, bounds):
    b, t, _ = qkv.shape
    lq = WINDOW
    nb = t // lq
    tc = ck.shape[1]

    def kv_spec(col, off):
        return pl.BlockSpec((1, lq, 2 * LANES), lambda i, j: (i, jnp.clip(j + off, 0, nb - 1), col))

    return pl.pallas_call(
        functools.partial(_attn_a_kernel, latent=True, lq=lq),
        grid=(b, nb),
        in_specs=[pl.BlockSpec(memory_space=pltpu.SMEM),
                  pl.BlockSpec(memory_space=pltpu.SMEM),
                  pl.BlockSpec((1, lq, 4 * LANES), lambda i, j: (i, j, 0)),
                  kv_spec(2, -1), kv_spec(2, 0), kv_spec(2, 1),
                  kv_spec(3, -1), kv_spec(3, 0), kv_spec(3, 1),
                  pl.BlockSpec((1, tc, 2 * LANES), lambda i, j: (i, 0, 0)),
                  pl.BlockSpec((1, tc, 2 * LANES), lambda i, j: (i, 0, 0))],
        out_specs=pl.BlockSpec((1, lq, 4 * LANES), lambda i, j: (i, j, 0)),
        out_shape=jax.ShapeDtypeStruct((b, t, 4 * LANES), F32),
        scratch_shapes=[pltpu.SMEM((1,), F32)],
        compiler_params=_params("parallel", "arbitrary"),
        name="attn_a_lat",
    )(bounds, sink, qkv, qkv, qkv, qkv, qkv, qkv, qkv, ck, cv)


def _retention_kernel(*refs, zero_init):
    if zero_init:
        lg_ref, q_ref, k_ref, v_ref, o_ref, sfin_ref, sp_ref = refs
    else:
        lg_ref, q_ref, k_ref, v_ref, s0_ref, o_ref, sfin_ref, sp_ref = refs
    pr = pl.program_id(1)
    L = RET_CHUNK
    n = q_ref.shape[1] // L
    lo = _lane_lo()
    lgf = jnp.where(lo, lg_ref[0, 2 * pr], lg_ref[0, 2 * pr + 1])
    lgb = jnp.where(lo, lg_ref[1, 2 * pr], lg_ref[1, 2 * pr + 1])
    idx = lax.broadcasted_iota(jnp.int32, (L, 1), 0).astype(F32)
    qdf = jnp.exp(lgf * (idx + 1.0))
    kdf = jnp.exp(lgf * (L - 1.0 - idx))
    cdf = jnp.exp(lgf * float(L))
    qdb = jnp.exp(lgb * (L - idx))
    kdb = jnp.exp(lgb * idx)
    cdb = jnp.exp(lgb * float(L))
    diff = (lax.broadcasted_iota(jnp.int32, (L, L), 0) - lax.broadcasted_iota(jnp.int32, (L, L), 1)).astype(F32)

    def decay(h):
        f = jnp.exp(lg_ref[0, 2 * pr + h] * jnp.maximum(diff, 0.0))
        bwd = jnp.exp(lg_ref[1, 2 * pr + h] * jnp.maximum(-diff, 0.0))
        return jnp.where(diff > 0, f, jnp.where(diff < 0, bwd, 2.0))

    d0 = decay(0)
    d1 = decay(1)
    blk = _same_head()

    if zero_init:
        sf0 = jnp.zeros((LANES, LANES), F32)
        sb0 = jnp.zeros((LANES, LANES), F32)
    else:
        sf0 = s0_ref[0, 0, 0]
        sb0 = s0_ref[0, 0, 1]

    qd = jnp.concatenate([qdf, qdb], axis=1)
    kd = jnp.concatenate([kdf, kdb], axis=1)
    blk2 = jnp.concatenate([blk, blk], axis=0)
    unroll = math.gcd(n, RET_UNROLL)

    def chunk(c):
        return pl.ds(pl.multiple_of(c * L, L), L)

    def kv_body(c, carry):
        k = k_ref[0, chunk(c), :]
        kv = _mm_tn(jnp.concatenate([k, k], axis=1) * kd, v_ref[0, chunk(c), :])
        sp_ref[c] = jnp.where(blk2, kv, 0.0)
        return carry

    lax.fori_loop(0, n, kv_body, 0, unroll=unroll)

    def f_body(c, sf):
        kv = sp_ref[c, :LANES, :]
        sp_ref[c, :LANES, :] = sf
        return sf * cdf + kv

    def b_body(i, sb):
        c = n - 1 - i
        kv = sp_ref[c, LANES:, :]
        sp_ref[c, LANES:, :] = sb
        return sb * cdb + kv

    sfin_ref[0, 0, 0] = lax.fori_loop(0, n, f_body, sf0)
    sfin_ref[0, 0, 1] = lax.fori_loop(0, n, b_body, sb0)

    def o_body(c, carry):
        q = q_ref[0, chunk(c), :]
        v = v_ref[0, chunk(c), :]
        s = _mm_nt(jnp.concatenate([jnp.where(lo, q, 0.0), jnp.where(lo, 0.0, q)], axis=0), k_ref[0, chunk(c), :])
        sd = jnp.concatenate([s[:L] * d0, s[L:] * d1], axis=1)
        v2 = jnp.concatenate([jnp.where(lo, v, 0.0), jnp.where(lo, 0.0, v)], axis=0)
        o_ref[0, chunk(c), :] = _mm(sd, v2) + _mm(jnp.concatenate([q, q], axis=1) * qd, sp_ref[c])
        return carry

    lax.fori_loop(0, n, o_body, 0, unroll=unroll)


def _retention(qkv, log_g, s0):
    b, t, _ = qkv.shape
    npair = 4
    in_specs = [pl.BlockSpec(memory_space=pltpu.SMEM),
                pl.BlockSpec((1, t, LANES), lambda i, p: (i, 0, p)),
                pl.BlockSpec((1, t, LANES), lambda i, p: (i, 0, npair + p)),
                pl.BlockSpec((1, t, LANES), lambda i, p: (i, 0, 2 * npair + p))]
    args = [log_g, qkv, qkv, qkv]
    if s0 is not None:
        in_specs.append(pl.BlockSpec((1, 1, 2, LANES, LANES), lambda i, p: (i, p, 0, 0, 0)))
        args.append(s0)
    return pl.pallas_call(
        functools.partial(_retention_kernel, zero_init=s0 is None),
        grid=(b, npair),
        in_specs=in_specs,
        out_specs=[pl.BlockSpec((1, t, LANES), lambda i, p: (i, 0, p)),
                   pl.BlockSpec((1, 1, 2, LANES, LANES), lambda i, p: (i, p, 0, 0, 0))],
        out_shape=[jax.ShapeDtypeStruct((b, t, npair * LANES), F32),
                   jax.ShapeDtypeStruct((b, npair, 2, LANES, LANES), F32)],
        scratch_shapes=[pltpu.VMEM((t // RET_CHUNK, 2 * LANES, LANES), F32)],
        compiler_params=_params("parallel", "parallel"),
        name="retention",
    )(*args)


S5_CHUNK = 16
S5_GROUPS = 4


def _s5_kernel(u_ref, m_ref, w_ref, v_ref, a_ref, h0_ref, y_ref, fin_ref, z_ref, sf_ref, sb_ref, *, nb):
    rows = u_ref.shape[2]
    n = rows // nb
    lo = _lane_lo()
    nbg = nb // SUBLANES
    for gi in range(u_ref.shape[0]):
        u = jnp.concatenate([u_ref[gi, t2] for t2 in range(u_ref.shape[1])], axis=1).astype(BF16)
        z_ref[...] = jnp.dot(u, w_ref[gi], preferred_element_type=F32)
        are = jnp.broadcast_to(a_ref[gi, 0:1, :], (SUBLANES, LANES))
        aim = jnp.broadcast_to(a_ref[gi, 1:2, :], (SUBLANES, LANES))

        def step(i, carry):
            out = []
            for bg in range(nbg):
                hre, him = carry[bg]
                up = pl.ds(pl.multiple_of(i * nb + bg * SUBLANES, SUBLANES), SUBLANES)
                dn = pl.ds(pl.multiple_of((n - 1 - i) * nb + bg * SUBLANES, SUBLANES), SUBLANES)
                sf_ref[up, :LANES] = hre
                sf_ref[up, LANES:] = him
                sb_ref[dn, :LANES] = hre
                sb_ref[dn, LANES:] = him
                zre = jnp.where(lo, z_ref[up, :LANES], z_ref[dn, :LANES])
                zim = jnp.where(lo, z_ref[up, LANES:], z_ref[dn, LANES:])
                out.append((are * hre - aim * him + zre, are * him + aim * hre + zim))
            return tuple(out)

        init = tuple((h0_ref[gi, bg * SUBLANES:(bg + 1) * SUBLANES, :LANES],
                      h0_ref[gi, bg * SUBLANES:(bg + 1) * SUBLANES, LANES:]) for bg in range(nbg))
        last = lax.fori_loop(0, n, step, init, unroll=2)
        for bg in range(nbg):
            fin_ref[gi, bg * SUBLANES:(bg + 1) * SUBLANES, :LANES] = last[bg][0]
            fin_ref[gi, bg * SUBLANES:(bg + 1) * SUBLANES, LANES:] = last[bg][1]
        y = (jnp.dot(u, m_ref[gi], preferred_element_type=F32)
             + _mm(sf_ref[...], v_ref[gi, 0]) + _mm(sb_ref[...], v_ref[gi, 1]))
        for t2 in range(y_ref.shape[1]):
            y_ref[gi, t2] = y[:, LANES * t2:LANES * (t2 + 1)]


GROUPS_PER_TILE = LANES // C_GROUP


def _lane_group():
    return lax.broadcasted_iota(jnp.int32, (1, LANES), 1) >> (C_GROUP.bit_length() - 1)


S5_TILES = S5_CHUNK * C_GROUP // LANES


def _swap_blocks(xs, grp):
    xs = list(xs)
    d = GROUPS_PER_TILE // 2
    while d >= 1:
        keep = (grp & d) == 0
        for i in range(GROUPS_PER_TILE):
            if i & d:
                continue
            lo_x, hi_x = xs[i], xs[i + d]
            xs[i] = jnp.where(keep, lo_x, pltpu.roll(hi_x, C_GROUP * d, 1))
            xs[i + d] = jnp.where(keep, pltpu.roll(lo_x, LANES - C_GROUP * d, 1), hi_x)
        d //= 2
    return xs


def _s5_pack_kernel(u_ref, o_ref):
    nb, tiles, tb, _ = u_ref.shape
    n = tb // S5_CHUNK
    grp = _lane_group()

    def per_batch(b, carry):
        for j in range(tiles):
            for t2 in range(S5_TILES):
                steps = [u_ref[b, j, pl.ds(GROUPS_PER_TILE * t2 + tl, n, stride=S5_CHUNK), :]
                         for tl in range(GROUPS_PER_TILE)]
                for q, x in enumerate(_swap_blocks(steps, grp)):
                    o_ref[GROUPS_PER_TILE * j + q, t2, pl.ds(b, n, stride=nb), :] = x
        return carry

    lax.fori_loop(0, nb, per_batch, 0)


def _s5_unpack_kernel(y_ref, o_ref):
    nb, tiles, tb, _ = o_ref.shape
    n = tb // S5_CHUNK
    grp = _lane_group()

    def per_batch(b, carry):
        for j in range(tiles):
            for t2 in range(S5_TILES):
                src = [y_ref[GROUPS_PER_TILE * j + q, t2, pl.ds(b, n, stride=nb), :] for q in range(GROUPS_PER_TILE)]
                for tl, x in enumerate(_swap_blocks(src, grp)):
                    o_ref[b, j, pl.ds(GROUPS_PER_TILE * t2 + tl, n, stride=S5_CHUNK), :] = x
        return carry

    lax.fori_loop(0, nb, per_batch, 0)


def _s5_relayout(x, b, t, cw, pack):
    g = cw // C_GROUP
    tb = max(S5_CHUNK * SUBLANES, 2048 // b)
    tok = pl.BlockSpec((b, cw // LANES, tb, LANES), lambda i: (0, 0, i, 0))
    grp = pl.BlockSpec((g, S5_TILES, tb // S5_CHUNK * b, LANES), lambda i: (0, 0, i, 0))
    tok_shape = jax.ShapeDtypeStruct((b, cw // LANES, t, LANES), F32)
    grp_shape = jax.ShapeDtypeStruct((g, S5_TILES, t // S5_CHUNK * b, LANES), F32)
    return pl.pallas_call(
        _s5_pack_kernel if pack else _s5_unpack_kernel,
        grid=(t // tb,),
        in_specs=[tok if pack else grp],
        out_specs=grp if pack else tok,
        out_shape=grp_shape if pack else tok_shape,
        compiler_params=_params("parallel"),
        name="s5_pack" if pack else "s5_unpack",
    )(x)


def _s5(u, prm, h0_re, h0_im):
    m, w, v, a = prm
    b, tiles, t, _ = u.shape
    cw = tiles * LANES
    g = cw // C_GROUP
    n = t // S5_CHUNK
    rows = n * b
    width = S5_CHUNK * C_GROUP
    ug = _s5_relayout(u, b, t, cw, True)
    if h0_re is None:
        h0 = jnp.zeros((g, b, 4 * C_STATE), F32)
    else:
        h0 = jnp.concatenate([h0_re[:, 0], h0_re[:, 1], h0_im[:, 0], h0_im[:, 1]], axis=-1).transpose(1, 0, 2)
    gs = S5_GROUPS

    def grp(shape):
        nd = len(shape)
        return pl.BlockSpec((gs,) + shape[1:], lambda i: (i,) + (0,) * (nd - 1))

    y, fin = pl.pallas_call(
        functools.partial(_s5_kernel, nb=b),
        grid=(g // gs,),
        in_specs=[grp(ug.shape), grp(m.shape), grp(w.shape), grp(v.shape), grp(a.shape), grp(h0.shape)],
        out_specs=[grp(ug.shape), grp(h0.shape)],
        out_shape=[jax.ShapeDtypeStruct(ug.shape, F32), jax.ShapeDtypeStruct(h0.shape, F32)],
        scratch_shapes=[pltpu.VMEM((rows, width), F32)] * 3,
        compiler_params=_params("parallel"),
        name="s5_scan",
    )(ug, m, w, v, a, h0)
    y = _s5_relayout(y, b, t, cw, False)
    fin = fin.reshape(g, b, 2, 2, C_STATE).transpose(2, 1, 3, 0, 4)
    return y, fin[0], fin[1]


def _s5_params(lam_re, lam_im, log_dt, b_re, b_im, c_re, c_im):
    dt = jnp.exp(log_dt)[..., None]
    mag = jnp.exp(lam_re * dt)
    a_re = mag * jnp.cos(lam_im * dt)
    a_im = mag * jnp.sin(lam_im * dt)
    den = lam_re * lam_re + lam_im * lam_im
    f_re = ((a_re - 1.0) * lam_re + a_im * lam_im) / den
    f_im = (a_im * lam_re - (a_re - 1.0) * lam_im) / den
    fb_re = f_re[..., None] * b_re[None] - f_im[..., None] * b_im[None]
    fb_im = f_re[..., None] * b_im[None] + f_im[..., None] * b_re[None]
    g = lam_re.shape[1]
    L = S5_CHUNK
    k = jnp.arange(L + 1, dtype=F32)[:, None, None, None]
    pmag = jnp.exp(lam_re * dt * k)
    p_re = pmag * jnp.cos(lam_im * dt * k)
    p_im = pmag * jnp.sin(lam_im * dt * k)
    fbt_re = fb_re.transpose(0, 1, 3, 2)
    fbt_im = fb_im.transpose(0, 1, 3, 2)

    ca_re = c_re[None, None] * p_re[:, :, :, None, :] - c_im[None, None] * p_im[:, :, :, None, :]
    ca_im = c_re[None, None] * p_im[:, :, :, None, :] + c_im[None, None] * p_re[:, :, :, None, :]
    car = ca_re.transpose(1, 2, 0, 3, 4)[:, :, None]
    cai = ca_im.transpose(1, 2, 0, 3, 4)[:, :, None]
    kern = jnp.sum(car * fbt_re[:, :, :, None, None, :] - cai * fbt_im[:, :, :, None, None, :], axis=-1)
    wid = L * C_GROUP
    fwd = kern[0, :, :, :L].reshape(g, C_GROUP, wid)
    bwd = kern[1, :, :, L - 1::-1][:, :, :L].reshape(g, C_GROUP, wid)
    rows = []
    for s in range(L):
        f = jnp.pad(fwd[:, :, :wid - s * C_GROUP], ((0, 0), (0, 0), (s * C_GROUP, 0)))
        bk = jnp.pad(bwd[:, :, (L - 1 - s) * C_GROUP:], ((0, 0), (0, 0), (0, (L - 1 - s) * C_GROUP)))
        rows.append(f + bk)
    m = jnp.stack(rows, axis=1).reshape(g, wid, wid)

    def to_state(pw_re, pw_im, d):
        re = pw_re[:, :, None, :] * fbt_re[d][None] - pw_im[:, :, None, :] * fbt_im[d][None]
        im = pw_re[:, :, None, :] * fbt_im[d][None] + pw_im[:, :, None, :] * fbt_re[d][None]
        return re.transpose(1, 0, 2, 3), im.transpose(1, 0, 2, 3)

    wf_re, wf_im = to_state(p_re[L - 1::-1, 0][:L], p_im[L - 1::-1, 0][:L], 0)
    wb_re, wb_im = to_state(p_re[:L, 1], p_im[:L, 1], 1)
    w = jnp.concatenate([wf_re, wb_re, wf_im, wb_im], axis=-1).reshape(g, wid, 4 * C_STATE)

    def from_state(car, cai):
        return car.transpose(1, 3, 0, 2), -cai.transpose(1, 3, 0, 2)

    vf_re, vf_im = from_state(ca_re[1:, 0], ca_im[1:, 0])
    vb_re, vb_im = from_state(ca_re[L:0:-1, 1], ca_im[L:0:-1, 1])
    zero = jnp.zeros_like(vf_re)
    v = jnp.stack([jnp.concatenate([vf_re, zero, vf_im, zero], axis=1),
                   jnp.concatenate([zero, vb_re, zero, vb_im], axis=1)], axis=1)
    v = v.reshape(g, 2, 4 * C_STATE, L * C_GROUP)

    a = jnp.stack([jnp.concatenate([p_re[L, 0], p_re[L, 1]], axis=-1),
                   jnp.concatenate([p_im[L, 0], p_im[L, 1]], axis=-1)], axis=1)
    return m.astype(BF16), w.astype(BF16), v.astype(BF16), a


def _diff_attn_kernel(*refs, latent, kc):
    if latent:
        bnd_ref, lam_ref, q_ref, k_ref, v_ref, ck_ref, cv_ref, o_ref, p_ref, cn_ref = refs
    else:
        bnd_ref, lam_ref, q_ref, k_ref, v_ref, o_ref, p_ref = refs
    lq = q_ref.shape[1]
    lo = _lane_lo()
    q = q_ref[0]
    qs = (jnp.where(lo, q, 0.0).astype(BF16), jnp.where(lo, 0.0, q).astype(BF16))

    def over_keys(step, carry):
        if latent:
            carry = step(ck_ref[0], cv_ref[0], carry)

            def body(c, carry):
                rows = pl.ds(pl.multiple_of(c * kc, kc), kc)
                return step(k_ref[0, rows, :], v_ref[0, rows, :], carry)

            return lax.fori_loop(0, k_ref.shape[1] // kc, body, carry)
        return step(k_ref[0], v_ref[0], carry)

    def finish(l0, acc0, l1, acc1):
        o_ref[0] = acc0 / l0 - lam_ref[0] * (acc1 / l1)

    def online_step(kb, vb, carry):
        kb = kb.astype(BF16)
        vb = vb.astype(BF16)
        out = []
        for qc, (m, l, acc) in zip(qs, carry):
            s = _mm_nt(qc, kb)
            m_new = jnp.maximum(m, jnp.max(s, axis=1, keepdims=True))
            alpha = jnp.exp2(m - m_new)
            p = jnp.exp2(s - m_new)
            l = alpha * l + jnp.sum(p, axis=1, keepdims=True)
            acc = alpha * acc + _mm(p, vb)
            out.append((m_new, l, acc))
        return tuple(out)

    def online():
        init = (jnp.full((lq, 1), NEG_INF, F32), jnp.zeros((lq, 1), F32), jnp.zeros((lq, LANES), F32))
        (_, l0, acc0), (_, l1, acc1) = over_keys(online_step, (init, init))
        finish(l0, acc0, l1, acc1)

    if latent:
        @pl.when(pl.program_id(2) == 0)
        def _():
            cn_ref[0] = _max_sq_head_norm(ck_ref[0])

        bound_sq = bnd_ref[0] * jnp.maximum(bnd_ref[1], cn_ref[0])
    else:
        bound_sq = bnd_ref[0] * bnd_ref[1]
    fixed = bound_sq <= SHIFT_LIMIT * SHIFT_LIMIT

    @pl.when(fixed)
    def _():
        shift = jnp.sqrt(jnp.full((1, 1), bound_sq, F32))
        nt = k_ref.shape[1]
        blocks = [(k_ref, v_ref, c0, min(c0 + kc, nt)) for c0 in range(0, nt, kc)]
        if latent:
            blocks = [(ck_ref, cv_ref, 0, ck_ref.shape[1])] + blocks
        sums = [jnp.zeros((lq, 1), F32), jnp.zeros((lq, 1), F32)]
        col = 0
        spans = []
        for kr, vr, r0, r1 in blocks:
            kb = kr[0, r0:r1, :].astype(BF16)
            for ci, qc in enumerate(qs):
                p = jnp.exp2(_mm_nt(qc, kb) - shift)
                sums[ci] = sums[ci] + jnp.sum(p, axis=1, keepdims=True)
                p_ref[ci, :, col:col + r1 - r0] = p.astype(BF16)
            spans.append((col, vr, r0, r1))
            col += r1 - r0
        c0 = 1.0 / sums[0]
        c1 = lam_ref[0] / sums[1]
        acc = jnp.zeros((lq, LANES), F32)
        for col, vr, r0, r1 in spans:
            cols = slice(col, col + r1 - r0)
            w = p_ref[0, :, cols].astype(F32) * c0 - p_ref[1, :, cols].astype(F32) * c1
            acc = acc + _mm(w, vr[0, r0:r1, :])
        o_ref[0] = acc

    @pl.when(jnp.logical_not(fixed))
    def _():
        online()


def _diff_attn(qkv, lam, cache=None, bounds=None, lq=512, kc=2048):
    b, t, _ = qkv.shape
    nh = 4
    latent = cache is not None
    kc = min(kc, t)
    assert t % kc == 0 and t % lq == 0
    in_specs = [pl.BlockSpec(memory_space=pltpu.SMEM),
                pl.BlockSpec(memory_space=pltpu.SMEM),
                pl.BlockSpec((1, lq, LANES), lambda i, h, j: (i, j, h)),
                pl.BlockSpec((1, t, LANES), lambda i, h, j: (i, 0, nh + h)),
                pl.BlockSpec((1, t, LANES), lambda i, h, j: (i, 0, 2 * nh + h))]
    args = [bounds, lam, qkv, qkv, qkv]
    if latent:
        tc = cache[0].shape[1]
        in_specs += [pl.BlockSpec((1, tc, LANES), lambda i, h, j: (i, 0, h))] * 2
        args += list(cache)
    return pl.pallas_call(
        functools.partial(_diff_attn_kernel, latent=latent, kc=kc),
        grid=(b, nh, t // lq),
        in_specs=in_specs,
        out_specs=pl.BlockSpec((1, lq, LANES), lambda i, h, j: (i, j, h)),
        out_shape=jax.ShapeDtypeStruct((b, t, nh * LANES), F32),
        scratch_shapes=([pltpu.VMEM((2, lq, t + (cache[0].shape[1] if latent else 0)), BF16)]
                        + ([pltpu.SMEM((1,), F32)] if latent else [])),
        compiler_params=_params("parallel", "parallel", "arbitrary"),
        name="diff_attn_lat" if latent else "diff_attn_ctx",
    )(*args)


def _resid_ffn(x, mix, g1, n2g, sh2, sc2, g2, w1_ref, w3_ref, w2_ref):
    x1 = x + g1 * mix
    h = (_rms(x1) * n2g * (1.0 + sc2) + sh2).astype(BF16)
    tiles = w1_ref.shape[1] // MXU_DIM
    cuts = [MXU_DIM * (tiles * s // FFN_SPLIT) for s in range(FFN_SPLIT + 1)]
    f = None
    for c0, c1 in zip(cuts[:-1], cuts[1:]):
        a = jnp.dot(h, w1_ref[:, c0:c1], preferred_element_type=F32)
        b = jnp.dot(h, w3_ref[:, c0:c1], preferred_element_type=F32)
        part = _mm(_silu(a) * b, w2_ref[c0:c1, :])
        f = part if f is None else f + part
    return x1 + g2 * f


def _out_ab_kernel(x_ref, oa_ref, ob_ref, gb_ref, rg_ref, wo_ref, g1_ref, n2g_ref, sh2_ref, sc2_ref,
                   g2_ref, w1_ref, w3_ref, w2_ref, y_ref):
    half = wo_ref.shape[0] // 2
    ob = ob_ref[0]
    inv = jnp.concatenate([_head_rsqrt(ob[:, c0:c0 + MXU_DIM]) for c0 in range(0, half, MXU_DIM)], axis=1)
    ob = ob * inv * rg_ref[...] * _silu(gb_ref[0])
    mix = _mm(oa_ref[0], wo_ref[:half, :]) + _mm(ob, wo_ref[half:, :])
    y_ref[0] = _resid_ffn(x_ref[0], mix, g1_ref[0], n2g_ref[...], sh2_ref[0], sc2_ref[0], g2_ref[0],
                          w1_ref, w3_ref, w2_ref)


def _out_cd_kernel(x_ref, ys_ref, u_ref, od_ref, dsk_ref, gw_ref, gbias_ref, sg_ref, wo_ref,
                   g1_ref, n2g_ref, sh2_ref, sc2_ref, g2_ref, w1_ref, w3_ref, w2_ref, y_ref, *, od_scale):
    half = wo_ref.shape[0] // 2
    tiles = range(ys_ref.shape[1])
    y = (jnp.concatenate([ys_ref[0, j] for j in tiles], axis=1)
         + dsk_ref[...] * jnp.concatenate([u_ref[0, j] for j in tiles], axis=1))
    g = jax.nn.gelu(y)
    oc = g * jax.nn.sigmoid(_mm(g, gw_ref[...]) + gbias_ref[...])
    od = jnp.concatenate([_rms(od_ref[0, :, c0:c0 + LANES]) * sg_ref[...] * od_scale
                          for c0 in range(0, half, LANES)], axis=1)
    mix = _mm(oc, wo_ref[:half, :]) + _mm(od, wo_ref[half:, :])
    y_ref[0] = _resid_ffn(x_ref[0], mix, g1_ref[0], n2g_ref[...], sh2_ref[0], sc2_ref[0], g2_ref[0],
                          w1_ref, w3_ref, w2_ref)


def _const_spec(shape):
    nd = len(shape)
    return pl.BlockSpec(shape, lambda i, j: (0,) * nd, pipeline_mode=pl.Buffered(1))


def _out_call(kern, name, x, tok_args, const_args, mod, ffn, tm):
    b, t, d = x.shape
    g1, sh2, sc2, g2 = mod
    n2g, w1, w3, w2 = ffn

    def tok(a):
        if a.ndim == 4:
            return pl.BlockSpec((1, a.shape[1], tm, LANES), lambda i, j: (i, 0, j, 0))
        return pl.BlockSpec((1, tm, a.shape[2]), lambda i, j: (i, j, 0))

    def per_batch(a):
        return pl.BlockSpec((1, 1, d), lambda i, j: (i, 0, 0))

    args = [x] + list(tok_args) + list(const_args) + [g1, n2g, sh2, sc2, g2, w1, w3, w2]
    in_specs = ([tok(x)] + [tok(a) for a in tok_args] + [_const_spec(a.shape) for a in const_args]
                + [per_batch(g1), _const_spec(n2g.shape), per_batch(sh2), per_batch(sc2), per_batch(g2),
                   _const_spec(w1.shape), _const_spec(w3.shape), _const_spec(w2.shape)])
    return pl.pallas_call(
        kern,
        grid=(b, t // tm),
        in_specs=in_specs,
        out_specs=tok(x),
        out_shape=jax.ShapeDtypeStruct(x.shape, F32),
        compiler_params=_params("parallel", "parallel"),
        name=name,
    )(*args)


def _ab_plan():
    plan = [(NORM_Q, 0, j) for j in range(4)] + [(NORM_K, 0, 4), (NORM_K, 0, 5), (RAW, 0, 6), (RAW, 0, 7)]
    plan += [(SCALE_Q, 1, j) for j in range(4)] + [(RAW, 1, 4 + j) for j in range(8)]
    plan += [(RAW, 2, j) for j in range(4)]
    return tuple(plan)


def _dup_kv_cols(w):
    q, kv, rest = w[:, :4 * LANES], w[:, 4 * LANES:6 * LANES], w[:, 6 * LANES:]
    kv = jnp.repeat(kv.reshape(w.shape[0], 4, 1, HEAD_DIM), 2, axis=2).reshape(w.shape[0], 4 * LANES)
    return jnp.concatenate([q, kv, rest], axis=1)


def _dup_heads(x):
    return jnp.repeat(x[..., :, None, :], 2, axis=-2).reshape(x.shape[:-2] + (4 * HEAD_DIM,))


def _cd_plan():
    plan = [(RAW, 0, j) for j in range(4)]
    plan += [(NORM_Q, 1, j) for j in range(4)] + [(NORM_K, 1, 4 + j) for j in range(4)]
    plan += [(RAW, 1, 8 + j) for j in range(4)]
    return tuple(plan)


AB_PLAN = _ab_plan()
CD_PLAN = _cd_plan()


def _rope_tables(t):
    rows = t // GRID_W
    r = jnp.repeat(jnp.arange(rows, dtype=F32), GRID_W)
    col = jnp.tile(jnp.arange(GRID_W, dtype=F32), rows)
    n_freq = HEAD_DIM // 4
    inv = ROPE_BASE ** (-jnp.arange(n_freq, dtype=F32) / n_freq)
    ang = jnp.concatenate([r[:, None] * inv, col[:, None] * inv], axis=-1)
    cos, sin = jnp.cos(ang), jnp.sin(ang)
    cos128 = jnp.tile(cos, (1, 4))
    sin128 = jnp.tile(jnp.concatenate([-sin, sin], axis=-1), (1, 2))
    return cos128, sin128


def _tile2(g):
    return jnp.tile(g.astype(F32), 2).reshape(1, LANES)


def kernel(x_prompt, x_sample, c, c_ctx, cache_k_a, cache_v_a, state_ret, state_ssm_re, state_ssm_im,
           cache_k_d, cache_v_d, ada_w, ada_b, norm1_g, norm2_g, ffn_w1, ffn_w3, ffn_w2,
           ab_w_in, ab_w_out, a_q_norm, a_k_norm, a_sink, ret_decay, ret_norm,
           cd_w_in, cd_w_out, ssm_lambda_re, ssm_lambda_im, ssm_log_dt, ssm_b_re, ssm_b_im,
           ssm_c_re, ssm_c_im, ssm_d, ssm_glu_w, ssm_glu_b, d_q_norm, d_k_norm, d_lambda, d_subln):
    nbc, tcx, d = x_prompt.shape
    nbl, tl, _ = x_sample.shape
    depth = ada_w.shape[0]
    rope_tabs = _rope_tables(tl)

    ncond = -(-(nbl + 1) // SUBLANES) * SUBLANES
    cond = jnp.zeros((ncond, d), F32).at[:nbl].set(c).at[nbl].set(c_ctx)
    mods = _ada(cond, ada_w, ada_b)

    def mod_split(l, lo, hi):
        m = mods[l, lo:hi].reshape(hi - lo, 1, 6, d)
        return [m[:, :, k] for k in range(6)]

    yp = x_prompt.reshape(1, nbc * tcx, d)
    ys = x_sample
    new_ka, new_va, new_ret, new_sr, new_si, new_kd, new_vd = [], [], [], [], [], [], []
    for l in range(depth):
        j = l // 2
        ffn = (norm2_g[l].reshape(1, d), ffn_w1[l].astype(BF16), ffn_w3[l].astype(BF16), ffn_w2[l].astype(BF16))
        n1g = norm1_g[l].reshape(1, d)
        sh1c, sc1c, g1c, sh2c, sc2c, g2c = mod_split(l, nbl, nbl + 1)
        sh1l, sc1l, g1l, sh2l, sc2l, g2l = mod_split(l, 0, nbl)
        if l % 2 == 0:
            w_in = _dup_kv_cols(ab_w_in[j]).astype(BF16)
            w_out = ab_w_out[j].astype(BF16)
            gq, gk = _tile2(a_q_norm[j]), _tile2(a_k_norm[j])
            rg = jnp.tile(ret_norm[j].astype(F32), ab_w_out.shape[1] // 2 // HEAD_DIM).reshape(1, -1)
            log_g = jax.nn.log_sigmoid(ret_decay[j].astype(F32))
            widths = (8 * LANES, 12 * LANES, 4 * LANES)
            a_scale = Q_SCALE
            qkv_a, qkv_b, gb = _inproj(yp, sh1c, sc1c, n1g, w_in, gq, gk, None, AB_PLAN, widths,
                                       (F32, F32, F32), a_scale, TOKEN_TILE, "inproj_ab_ctx")
            qkv_a = qkv_a.reshape(nbc, tcx, -1)
            bounds = _norm_bounds(a_q_norm[j], a_k_norm[j])
            oa = _attn_a_ctx(qkv_a, a_sink[j], bounds)
            ob, s_ret = _retention(qkv_b.reshape(nbc, tcx, -1), log_g, None)
            yp = _out_call(_out_ab_kernel, "out_ab_ctx", yp,
                           [oa.reshape(1, nbc * tcx, -1), ob.reshape(1, nbc * tcx, -1), gb],
                           [rg, w_out], (g1c, sh2c, sc2c, g2c), ffn, TOKEN_TILE)
            new_ka.append(qkv_a[:, :, 4 * LANES:6 * LANES].reshape(nbc, tcx, 2, 2, HEAD_DIM)[:, :, :, 0])
            new_va.append(qkv_a[:, :, 6 * LANES:8 * LANES].reshape(nbc, tcx, 2, 2, HEAD_DIM)[:, :, :, 0])
            s_ret = jnp.stack([s_ret[..., :HEAD_DIM, :HEAD_DIM], s_ret[..., HEAD_DIM:, HEAD_DIM:]], axis=2)
            new_ret.append(s_ret.transpose(0, 3, 1, 2, 4, 5).reshape(nbc, 2, 8, HEAD_DIM, HEAD_DIM))
            qkv_a, qkv_b, gb = _inproj(ys, sh1l, sc1l, n1g, w_in, gq, gk, rope_tabs, AB_PLAN, widths,
                                       (BF16, F32, F32), a_scale, TOKEN_TILE, "inproj_ab_lat")
            ck = _dup_heads(cache_k_a[:, j]).astype(BF16)
            cv = _dup_heads(cache_v_a[:, j]).astype(BF16)
            oa = _attn_a_lat(qkv_a, ck, cv, a_sink[j], bounds)
            s0 = state_ret[:, j].reshape(nbl, 2, 4, 2, HEAD_DIM, HEAD_DIM)
            z = jnp.zeros_like(s0[:, :, :, 0])
            s0 = jnp.concatenate([jnp.concatenate([s0[:, :, :, 0], z], axis=-1),
                                  jnp.concatenate([z, s0[:, :, :, 1]], axis=-1)], axis=-2)
            s0 = s0.transpose(0, 2, 1, 3, 4)
            ob, _ = _retention(qkv_b, log_g, s0)
            ys = _out_call(_out_ab_kernel, "out_ab_lat", ys, [oa, ob, gb], [rg, w_out],
                           (g1l, sh2l, sc2l, g2l), ffn, TOKEN_TILE)
        else:
            lam_init = 0.8 - 0.6 * math.exp(-0.3 * l)
            w_in = cd_w_in[j].astype(BF16)
            w_out = cd_w_out[j].astype(BF16)
            gq, gk = _tile2(d_q_norm[j]), _tile2(d_k_norm[j])
            lp = d_lambda[j].astype(F32)
            lam = (jnp.exp(jnp.sum(lp[0] * lp[1])) - jnp.exp(jnp.sum(lp[2] * lp[3])) + lam_init).reshape(1)
            prm = _s5_params(ssm_lambda_re[j], ssm_lambda_im[j], ssm_log_dt[j], ssm_b_re[j], ssm_b_im[j],
                             ssm_c_re[j], ssm_c_im[j])
            consts =[ssm_d[j].reshape(1, -1), ssm_glu_w[j].astype(BF16), ssm_glu_b[j].reshape(1, -1),
                      d_subln[j].reshape(1, LANES), w_out]
            kern = functools.partial(_out_cd_kernel, od_scale=1.0 - lam_init)
            widths = (-4 * LANES, 12 * LANES)
            d_scale = Q_SCALE
            u, qkv_d = _inproj(yp, sh1c, sc1c, n1g, w_in, gq, gk, None, CD_PLAN, widths,
                               (F32, F32), d_scale, TOKEN_TILE, "inproj_cd_ctx")
            u_b = u.reshape(-1, nbc, tcx, LANES).transpose(1, 0, 2, 3)
            y, fin_re, fin_im = _s5(u_b, prm, None, None)
            y = y.transpose(1, 0, 2, 3).reshape(u.shape)
            qkv_d = qkv_d.reshape(nbc, tcx, -1)
            d_bounds = _norm_bounds(d_q_norm[j], d_k_norm[j])
            od = _diff_attn(qkv_d, lam, None, d_bounds, lq=tcx)
            yp = _out_call(kern, "out_cd_ctx", yp, [y, u, od.reshape(1, nbc * tcx, -1)],
                           consts, (g1c, sh2c, sc2c, g2c), ffn, TOKEN_TILE)
            new_sr.append(fin_re)
            new_si.append(fin_im)
            new_kd.append(qkv_d[:, :, 4 * LANES:8 * LANES].reshape(nbc, tcx, 4, 2, HEAD_DIM))
            new_vd.append(qkv_d[:, :, 8 * LANES:].reshape(nbc, tcx, 4, 2 * HEAD_DIM))
            u, qkv_d = _inproj(ys, sh1l, sc1l, n1g, w_in, gq, gk, rope_tabs, CD_PLAN, widths,
                               (F32, BF16), d_scale, TOKEN_TILE, "inproj_cd_lat")
            y, _, _ = _s5(u, prm, state_ssm_re[:, j], state_ssm_im[:, j])
            ckd = cache_k_d[:, j].reshape(nbl, -1, 4 * LANES)
            cvd = cache_v_d[:, j].reshape(nbl, -1, 4 * LANES)
            od = _diff_attn(qkv_d, lam, (ckd, cvd), d_bounds)
            ys = _out_call(kern, "out_cd_lat", ys, [y, u, od], consts, (g1l, sh2l, sc2l, g2l), ffn, TOKEN_TILE)
    return (yp.reshape(nbc, tcx, d), ys,
            jnp.stack(new_ka, axis=1), jnp.stack(new_va, axis=1), jnp.stack(new_ret, axis=1),
            jnp.stack(new_sr, axis=1), jnp.stack(new_si, axis=1),
            jnp.stack(new_kd, axis=1), jnp.stack(new_vd, axis=1))
```

```python
import functools
import math

import jax
import jax.numpy as jnp
from jax import lax
from jax.experimental import pallas as pl
from jax.experimental.pallas import tpu as pltpu

F32 = jnp.float32
BF16 = jnp.bfloat16

LANES = 128
SUBLANES = 8
HEAD_DIM = 64
GRID_W = 64
ROPE_BASE = 10000.0
WINDOW = 128
RET_CHUNK = 128
RET_UNROLL = 16
FFN_SPLIT = 2
TOKEN_TILE = 512
C_GROUP = 16
C_STATE = 64
NEG_INF = -1e30
EPS = 1e-6
LOG2E = math.log2(math.e)
Q_SCALE = HEAD_DIM ** -0.5 * LOG2E
SHIFT_LIMIT = 50.0
VMEM_LIMIT = 56 * 1024 * 1024


def _params(*sem):
    return pltpu.CompilerParams(dimension_semantics=sem, vmem_limit_bytes=VMEM_LIMIT)


def _mm(a, b):
    return jnp.dot(a.astype(BF16), b.astype(BF16), preferred_element_type=F32)


def _mm_nt(a, b):
    return lax.dot_general(a.astype(BF16), b.astype(BF16), (((1,), (1,)), ((), ())),
                           preferred_element_type=F32)


def _mm_tn(a, b):
    return lax.dot_general(a.astype(BF16), b.astype(BF16), (((0,), (0,)), ((), ())),
                           preferred_element_type=F32)


def _rms(x):
    return x * lax.rsqrt(jnp.mean(x * x, axis=-1, keepdims=True) + EPS)


def _silu(x):
    return x * jax.nn.sigmoid(x)


def _lane_lo():
    return lax.broadcasted_iota(jnp.int32, (1, LANES), 1) < HEAD_DIM


def _same_head():
    r = lax.broadcasted_iota(jnp.int32, (LANES, LANES), 0) < HEAD_DIM
    c = lax.broadcasted_iota(jnp.int32, (LANES, LANES), 1) < HEAD_DIM
    return r == c


def _head_ones():
    return _same_head().astype(BF16)


MXU_DIM = 256


def _head_rsqrt(x):
    r = lax.broadcasted_iota(jnp.int32, (MXU_DIM, MXU_DIM), 0) >> (HEAD_DIM.bit_length() - 1)
    c = lax.broadcasted_iota(jnp.int32, (MXU_DIM, MXU_DIM), 1) >> (HEAD_DIM.bit_length() - 1)
    ss = jnp.dot((x * x).astype(BF16), (r == c).astype(BF16), preferred_element_type=F32)
    return lax.rsqrt(ss * (1.0 / HEAD_DIM) + EPS)


def _max_sq_head_norm(x):
    bd = _head_ones()
    best = None
    for j in range(x.shape[1] // LANES):
        xf = x[:, LANES * j:LANES * (j + 1)].astype(F32)
        m = jnp.max(jnp.dot((xf * xf).astype(BF16), bd, preferred_element_type=F32))
        best = m if best is None else jnp.maximum(best, m)
    return best


def _norm_bounds(gq, gk):
    q_sq = HEAD_DIM * Q_SCALE * Q_SCALE * jnp.max(gq.astype(F32) ** 2)
    k_sq = HEAD_DIM * jnp.max(gk.astype(F32) ** 2)
    return jnp.stack([q_sq, k_sq])


def _rope128(x, cos, sin_signed):
    lane = lax.broadcasted_iota(jnp.int32, (1, LANES), 1)
    first = (lane & (HEAD_DIM // 2)) == 0
    partner = jnp.where(first, pltpu.roll(x, LANES - HEAD_DIM // 2, 1), pltpu.roll(x, HEAD_DIM // 2, 1))
    return x * cos + partner * sin_signed


def _ada_kernel(c_ref, w_ref, b_ref, o_ref):
    o_ref[0] = _mm(_silu(c_ref[...]), w_ref[0]) + b_ref[0]


def _ada(cond, ada_w, ada_b):
    depth, d, n = ada_w.shape
    tn = n // 4
    return pl.pallas_call(
        _ada_kernel,
        grid=(depth, n // tn),
        in_specs=[pl.BlockSpec(cond.shape, lambda l, j: (0, 0)),
                  pl.BlockSpec((1, d, tn), lambda l, j: (l, 0, j)),
                  pl.BlockSpec((1, 1, tn), lambda l, j: (l, 0, j))],
        out_specs=pl.BlockSpec((1, cond.shape[0], tn), lambda l, j: (l, 0, j)),
        out_shape=jax.ShapeDtypeStruct((depth, cond.shape[0], n), F32),
        compiler_params=_params("parallel", "parallel"),
        name="ada_modulation",
    )(cond, ada_w, ada_b.reshape(depth, 1, n))


RAW, NORM_Q, NORM_K, SCALE_Q = 0, 1, 2, 3


def _inproj_kernel(*refs, plan, rope, q_scale):
    x_ref, sh_ref, sc_ref, ng_ref, w_ref, gq_ref, gk_ref = refs[:7]
    pos = 7
    if rope:
        cos_ref, sin_ref = refs[7:9]
        pos = 9
    outs = refs[pos:]
    x = x_ref[0]
    h = _rms(x) * ng_ref[...] * (1.0 + sc_ref[0]) + sh_ref[0]
    p = _mm(h, w_ref[...])
    inv_rms = {}
    for j, (op, oi, dj) in enumerate(plan):
        chunk = p[:, LANES * j:LANES * (j + 1)]
        if op in (NORM_Q, NORM_K):
            g = gq_ref[...] if op == NORM_Q else gk_ref[...]
            j0 = j - j % 2
            if j0 not in inv_rms:
                inv_rms[j0] = _head_rsqrt(p[:, LANES * j0:LANES * (j0 + 2)])
            chunk = chunk * inv_rms[j0][:, LANES * (j - j0):LANES * (j - j0 + 1)] * g
            if rope:
                chunk = _rope128(chunk, cos_ref[...], sin_ref[...])
            if op == NORM_Q:
                chunk = chunk * q_scale
        elif op == SCALE_Q:
            chunk = chunk * (HEAD_DIM ** -0.5)
        if len(outs[oi].shape) == 4:
            outs[oi][0, dj] = chunk.astype(outs[oi].dtype)
        else:
            outs[oi][0, :, LANES * dj:LANES * (dj + 1)] = chunk.astype(outs[oi].dtype)


def _inproj(x, sh, sc, ng, w, gq, gk, rope_tabs, plan, out_widths, out_dtypes, q_scale, tm, name):
    b, t, d = x.shape
    n = w.shape[1]
    rope = rope_tabs is not None
    in_specs = [pl.BlockSpec((1, tm, d), lambda i, j: (i, j, 0)),
                pl.BlockSpec((1, 1, d), lambda i, j: (i, 0, 0)),
                pl.BlockSpec((1, 1, d), lambda i, j: (i, 0, 0)),
                pl.BlockSpec((1, d), lambda i, j: (0, 0)),
                pl.BlockSpec((d, n), lambda i, j: (0, 0)),
                pl.BlockSpec((1, LANES), lambda i, j: (0, 0)),
                pl.BlockSpec((1, LANES), lambda i, j: (0, 0))]
    args = [x, sh, sc, ng, w, gq, gk]
    if rope:
        in_specs += [pl.BlockSpec((tm, LANES), lambda i, j: (j, 0))] * 2
        args += list(rope_tabs)
    return pl.pallas_call(
        functools.partial(_inproj_kernel, plan=plan, rope=rope, q_scale=q_scale),
        grid=(b, t // tm),
        in_specs=in_specs,
        out_specs=[pl.BlockSpec((1, tm, wd), lambda i, j: (i, j, 0)) if wd > 0 else
                   pl.BlockSpec((1, -wd // LANES, tm, LANES), lambda i, j: (i, 0, j, 0)) for wd in out_widths],
        out_shape=[jax.ShapeDtypeStruct((b, t, wd) if wd > 0 else (b, -wd // LANES, t, LANES), dt)
                   for wd, dt in zip(out_widths, out_dtypes)],
        compiler_params=_params("parallel", "parallel"),
        name=name,
    )(*args)


def _attn_a_kernel(*refs, latent, lq):
    if latent:
        (bnd_ref, sink_ref, q_ref, kp_ref, kc_ref, kn_ref, vp_ref, vc_ref, vn_ref, ck_ref, cv_ref, o_ref,
         cn_ref) = refs
        i = pl.program_id(1)

        @pl.when(i == 0)
        def _():
            cn_ref[0] = _max_sq_head_norm(ck_ref[0])

        k_sq = jnp.maximum(bnd_ref[1], cn_ref[0])
        last = pl.num_programs(1) - 1
        k_all = jnp.concatenate([ck_ref[0], kp_ref[0], kc_ref[0], kn_ref[0]], axis=0)
        v_all = jnp.concatenate([cv_ref[0], vp_ref[0], vc_ref[0], vn_ref[0]], axis=0)
        tc = ck_ref.shape[1]
        assert lq & (lq - 1) == 0
        r = lax.broadcasted_iota(jnp.int32, (2 * lq, lq), 0) & (lq - 1)
        c = lax.broadcasted_iota(jnp.int32, (2 * lq, lq), 1)
        piece_masks = {tc: (c >= r) & (i > 0), tc + 2 * lq: (c <= r) & (i < last)}
    else:
        bnd_ref, sink_ref, q_ref, k_ref, v_ref, o_ref = refs
        k_all = k_ref[0]
        v_all = v_ref[0]
        piece_masks = {}
        k_sq = bnd_ref[1]
    nk = k_all.shape[0]
    lo = _lane_lo()
    row = lax.broadcasted_iota(jnp.int32, (2 * lq, 1), 0)

    def masked(x, fill):
        if not piece_masks:
            return x
        pieces = [jnp.where(piece_masks[o], x[:, o:o + lq], fill) if o in piece_masks else x[:, o:o + lq]
                  for o in range(0, nk, lq)]
        return jnp.concatenate(pieces, axis=1)

    sinks = [sink_ref[h] * LOG2E for h in range(8)]
    sink_max = functools.reduce(jnp.maximum, sinks)
    bound_sq = bnd_ref[0] * k_sq
    fixed = (bound_sq <= SHIFT_LIMIT * SHIFT_LIMIT) & (sink_max <= SHIFT_LIMIT)

    def attend(softmax):
        for g in range(2):
            k2 = k_all[:, LANES * g:LANES * (g + 1)].astype(BF16)
            v2 = v_all[:, LANES * g:LANES * (g + 1)].astype(BF16)
            for jp in range(2):
                j = 2 * g + jp
                q = q_ref[0, :, LANES * j:LANES * (j + 1)]
                q2 = jnp.concatenate([jnp.where(lo, q, 0.0), jnp.where(lo, 0.0, q)], axis=0)
                sk = jnp.where(row < lq, sinks[2 * j], sinks[2 * j + 1])
                p, den = softmax(_mm_nt(q2, k2), sk)
                o2 = _mm(p, v2) / den
                o_ref[0, :, LANES * j:LANES * (j + 1)] = jnp.where(lo, o2[:lq], o2[lq:])

    @pl.when(fixed)
    def _():
        shift = jnp.maximum(jnp.sqrt(jnp.full((1, 1), bound_sq, F32)), sink_max)

        def softmax(s, sk):
            p = masked(jnp.exp2(s - shift), 0.0)
            return p, jnp.sum(p, axis=1, keepdims=True) + jnp.exp2(sk - shift)

        attend(softmax)

    @pl.when(jnp.logical_not(fixed))
    def _():
        def softmax(s, sk):
            s = masked(s, NEG_INF)
            m = jnp.maximum(jnp.max(s, axis=1, keepdims=True), sk)
            p = jnp.exp2(s - m)
            return p, jnp.sum(p, axis=1, keepdims=True) + jnp.exp2(sk - m)

        attend(softmax)


def _attn_a_ctx(qkv, sink, bounds):
    b, t, _ = qkv.shape
    return pl.pallas_call(
        functools.partial(_attn_a_kernel, latent=False, lq=t),
        grid=(b,),
        in_specs=[pl.BlockSpec(memory_space=pltpu.SMEM),
                  pl.BlockSpec(memory_space=pltpu.SMEM),
                  pl.BlockSpec((1, t, 4 * LANES), lambda i: (i, 0, 0)),
                  pl.BlockSpec((1, t, 2 * LANES), lambda i: (i, 0, 2)),
                  pl.BlockSpec((1, t, 2 * LANES), lambda i: (i, 0, 3))],
        out_specs=pl.BlockSpec((1, t, 4 * LANES), lambda i: (i, 0, 0)),
        out_shape=jax.ShapeDtypeStruct((b, t, 4 * LANES), F32),
        compiler_params=_params("parallel"),
        name="attn_a_ctx",
    )(bounds, sink, qkv, qkv, qkv)


def _attn_a_lat(qkv, ck, cv, sink, bounds):
    b, t, _ = qkv.shape
    lq = WINDOW
    nb = t // lq
    tc = ck.shape[1]

    def kv_spec(col, off):
        return pl.BlockSpec((1, lq, 2 * LANES), lambda i, j: (i, jnp.clip(j + off, 0, nb - 1), col))

    return pl.pallas_call(
        functools.partial(_attn_a_kernel, latent=True, lq=lq),
        grid=(b, nb),
        in_specs=[pl.BlockSpec(memory_space=pltpu.SMEM),
                  pl.BlockSpec(memory_space=pltpu.SMEM),
                  pl.BlockSpec((1, lq, 4 * LANES), lambda i, j: (i, j, 0)),
                  kv_spec(2, -1), kv_spec(2, 0), kv_spec(2, 1),
                  kv_spec(3, -1), kv_spec(3, 0), kv_spec(3, 1),
                  pl.BlockSpec((1, tc, 2 * LANES), lambda i, j: (i, 0, 0)),
                  pl.BlockSpec((1, tc, 2 * LANES), lambda i, j: (i, 0, 0))],
        out_specs=pl.BlockSpec((1, lq, 4 * LANES), lambda i, j: (i, j, 0)),
        out_shape=jax.ShapeDtypeStruct((b, t, 4 * LANES), F32),
        scratch_shapes=[pltpu.SMEM((1,), F32)],
        compiler_params=_params("parallel", "arbitrary"),
        name="attn_a_lat",
    )(bounds, sink, qkv, qkv, qkv, qkv, qkv, qkv, qkv, ck, cv)


def _retention_kernel(*refs, zero_init):
    if zero_init:
        lg_ref, q_ref, k_ref, v_ref, o_ref, sfin_ref, sp_ref = refs
    else:
        lg_ref, q_ref, k_ref, v_ref, s0_ref, o_ref, sfin_ref, sp_ref = refs
    pr = pl.program_id(1)
    L = RET_CHUNK
    n = q_ref.shape[1] // L
    lo = _lane_lo()
    lgf = jnp.where(lo, lg_ref[0, 2 * pr], lg_ref[0, 2 * pr + 1])
    lgb = jnp.where(lo, lg_ref[1, 2 * pr], lg_ref[1, 2 * pr + 1])
    idx = lax.broadcasted_iota(jnp.int32, (L, 1), 0).astype(F32)
    qdf = jnp.exp(lgf * (idx + 1.0))
    kdf = jnp.exp(lgf * (L - 1.0 - idx))
    cdf = jnp.exp(lgf * float(L))
    qdb = jnp.exp(lgb * (L - idx))
    kdb = jnp.exp(lgb * idx)
    cdb = jnp.exp(lgb * float(L))
    diff = (lax.broadcasted_iota(jnp.int32, (L, L), 0) - lax.broadcasted_iota(jnp.int32, (L, L), 1)).astype(F32)

    def decay(h):
        f = jnp.exp(lg_ref[0, 2 * pr + h] * jnp.maximum(diff, 0.0))
        bwd = jnp.exp(lg_ref[1, 2 * pr + h] * jnp.maximum(-diff, 0.0))
        return jnp.where(diff > 0, f, jnp.where(diff < 0, bwd, 2.0))

    d0 = decay(0)
    d1 = decay(1)
    blk = _same_head()

    if zero_init:
        sf0 = jnp.zeros((LANES, LANES), F32)
        sb0 = jnp.zeros((LANES, LANES), F32)
    else:
        sf0 = s0_ref[0, 0, 0]
        sb0 = s0_ref[0, 0, 1]

    qd = jnp.concatenate([qdf, qdb], axis=1)
    kd = jnp.concatenate([kdf, kdb], axis=1)
    blk2 = jnp.concatenate([blk, blk], axis=0)
    unroll = math.gcd(n, RET_UNROLL)

    def chunk(c):
        return pl.ds(pl.multiple_of(c * L, L), L)

    def kv_body(c, carry):
        k = k_ref[0, chunk(c), :]
        kv = _mm_tn(jnp.concatenate([k, k], axis=1) * kd, v_ref[0, chunk(c), :])
        sp_ref[c] = jnp.where(blk2, kv, 0.0)
        return carry

    lax.fori_loop(0, n, kv_body, 0, unroll=unroll)

    def f_body(c, sf):
        kv = sp_ref[c, :LANES, :]
        sp_ref[c, :LANES, :] = sf
        return sf * cdf + kv

    def b_body(i, sb):
        c = n - 1 - i
        kv = sp_ref[c, LANES:, :]
        sp_ref[c, LANES:, :] = sb
        return sb * cdb + kv

    sfin_ref[0, 0, 0] = lax.fori_loop(0, n, f_body, sf0)
    sfin_ref[0, 0, 1] = lax.fori_loop(0, n, b_body, sb0)

    def o_body(c, carry):
        q = q_ref[0, chunk(c), :]
        v = v_ref[0, chunk(c), :]
        s = _mm_nt(jnp.concatenate([jnp.where(lo, q, 0.0), jnp.where(lo, 0.0, q)], axis=0), k_ref[0, chunk(c), :])
        sd = jnp.concatenate([s[:L] * d0, s[L:] * d1], axis=1)
        v2 = jnp.concatenate([jnp.where(lo, v, 0.0), jnp.where(lo, 0.0, v)], axis=0)
        o_ref[0, chunk(c), :] = _mm(sd, v2) + _mm(jnp.concatenate([q, q], axis=1) * qd, sp_ref[c])
        return carry

    lax.fori_loop(0, n, o_body, 0, unroll=unroll)


def _retention(qkv, log_g, s0):
    b, t, _ = qkv.shape
    npair = 4
    in_specs = [pl.BlockSpec(memory_space=pltpu.SMEM),
                pl.BlockSpec((1, t, LANES), lambda i, p: (i, 0, p)),
                pl.BlockSpec((1, t, LANES), lambda i, p: (i, 0, npair + p)),
                pl.BlockSpec((1, t, LANES), lambda i, p: (i, 0, 2 * npair + p))]
    args = [log_g, qkv, qkv, qkv]
    if s0 is not None:
        in_specs.append(pl.BlockSpec((1, 1, 2, LANES, LANES), lambda i, p: (i, p, 0, 0, 0)))
        args.append(s0)
    return pl.pallas_call(
        functools.partial(_retention_kernel, zero_init=s0 is None),
        grid=(b, npair),
        in_specs=in_specs,
        out_specs=[pl.BlockSpec((1, t, LANES), lambda i, p: (i, 0, p)),
                   pl.BlockSpec((1, 1, 2, LANES, LANES), lambda i, p: (i, p, 0, 0, 0))],
        out_shape=[jax.ShapeDtypeStruct((b, t, npair * LANES), F32),
                   jax.ShapeDtypeStruct((b, npair, 2, LANES, LANES), F32)],
        scratch_shapes=[pltpu.VMEM((t // RET_CHUNK, 2 * LANES, LANES), F32)],
        compiler_params=_params("parallel", "parallel"),
        name="retention",
    )(*args)


S5_CHUNK = 16
S5_GROUPS = 4


def _s5_kernel(u_ref, m_ref, w_ref, v_ref, a_ref, h0_ref, y_ref, fin_ref, z_ref, sf_ref, sb_ref, *, nb):
    rows = u_ref.shape[2]
    n = rows // nb
    lo = _lane_lo()
    nbg = nb // SUBLANES
    for gi in range(u_ref.shape[0]):
        u = jnp.concatenate([u_ref[gi, t2] for t2 in range(u_ref.shape[1])], axis=1).astype(BF16)
        z_ref[...] = jnp.dot(u, w_ref[gi], preferred_element_type=F32)
        are = jnp.broadcast_to(a_ref[gi, 0:1, :], (SUBLANES, LANES))
        aim = jnp.broadcast_to(a_ref[gi, 1:2, :], (SUBLANES, LANES))

        def step(i, carry):
            out = []
            for bg in range(nbg):
                hre, him = carry[bg]
                up = pl.ds(pl.multiple_of(i * nb + bg * SUBLANES, SUBLANES), SUBLANES)
                dn = pl.ds(pl.multiple_of((n - 1 - i) * nb + bg * SUBLANES, SUBLANES), SUBLANES)
                sf_ref[up, :LANES] = hre
                sf_ref[up, LANES:] = him
                sb_ref[dn, :LANES] = hre
                sb_ref[dn, LANES:] = him
                zre = jnp.where(lo, z_ref[up, :LANES], z_ref[dn, :LANES])
                zim = jnp.where(lo, z_ref[up, LANES:], z_ref[dn, LANES:])
                out.append((are * hre - aim * him + zre, are * him + aim * hre + zim))
            return tuple(out)

        init = tuple((h0_ref[gi, bg * SUBLANES:(bg + 1) * SUBLANES, :LANES],
                      h0_ref[gi, bg * SUBLANES:(bg + 1) * SUBLANES, LANES:]) for bg in range(nbg))
        last = lax.fori_loop(0, n, step, init, unroll=2)
        for bg in range(nbg):
            fin_ref[gi, bg * SUBLANES:(bg + 1) * SUBLANES, :LANES] = last[bg][0]
            fin_ref[gi, bg * SUBLANES:(bg + 1) * SUBLANES, LANES:] = last[bg][1]
        y = (jnp.dot(u, m_ref[gi], preferred_element_type=F32)
             + _mm(sf_ref[...], v_ref[gi, 0]) + _mm(sb_ref[...], v_ref[gi, 1]))
        for t2 in range(y_ref.shape[1]):
            y_ref[gi, t2] = y[:, LANES * t2:LANES * (t2 + 1)]


GROUPS_PER_TILE = LANES // C_GROUP


def _lane_group():
    return lax.broadcasted_iota(jnp.int32, (1, LANES), 1) >> (C_GROUP.bit_length() - 1)


S5_TILES = S5_CHUNK * C_GROUP // LANES


def _swap_blocks(sets, grp):
    sets = [list(xs) for xs in sets]
    d = GROUPS_PER_TILE // 2
    while d >= 1:
        keep = (grp & d) == 0
        for xs in sets:
            for i in range(GROUPS_PER_TILE):
                if i & d:
                    continue
                lo_x, hi_x = xs[i], xs[i + d]
                xs[i] = jnp.where(keep, lo_x, pltpu.roll(hi_x, C_GROUP * d, 1))
                xs[i + d] = jnp.where(keep, pltpu.roll(lo_x, LANES - C_GROUP * d, 1), hi_x)
        d //= 2
    return sets


def _s5_pack_kernel(u_ref, o_ref):
    nb, tiles, tb, _ = u_ref.shape
    n = tb // S5_CHUNK
    grp = _lane_group()

    def per_batch(b, carry):
        keys = [(j, t2) for j in range(tiles) for t2 in range(S5_TILES)]
        sets = [[u_ref[b, j, pl.ds(GROUPS_PER_TILE * t2 + tl, n, stride=S5_CHUNK), :]
                 for tl in range(GROUPS_PER_TILE)] for j, t2 in keys]
        for (j, t2), xs in zip(keys, _swap_blocks(sets, grp)):
            for q, x in enumerate(xs):
                o_ref[GROUPS_PER_TILE * j + q, t2, pl.ds(b, n, stride=nb), :] = x
        return carry

    lax.fori_loop(0, nb, per_batch, 0)


def _s5_unpack_kernel(y_ref, o_ref):
    nb, tiles, tb, _ = o_ref.shape
    n = tb // S5_CHUNK
    grp = _lane_group()

    def per_batch(b, carry):
        keys = [(j, t2) for j in range(tiles) for t2 in range(S5_TILES)]
        sets = [[y_ref[GROUPS_PER_TILE * j + q, t2, pl.ds(b, n, stride=nb), :] for q in range(GROUPS_PER_TILE)]
                for j, t2 in keys]
        for (j, t2), xs in zip(keys, _swap_blocks(sets, grp)):
            for tl, x in enumerate(xs):
                o_ref[b, j, pl.ds(GROUPS_PER_TILE * t2 + tl, n, stride=S5_CHUNK), :] = x
        return carry

    lax.fori_loop(0, nb, per_batch, 0)


def _s5_relayout(x, b, t, cw, pack):
    g = cw // C_GROUP
    tb = max(S5_CHUNK * SUBLANES, 2048 // b)
    tok = pl.BlockSpec((b, cw // LANES, tb, LANES), lambda i: (0, 0, i, 0))
    grp = pl.BlockSpec((g, S5_TILES, tb // S5_CHUNK * b, LANES), lambda i: (0, 0, i, 0))
    tok_shape = jax.ShapeDtypeStruct((b, cw // LANES, t, LANES), F32)
    grp_shape = jax.ShapeDtypeStruct((g, S5_TILES, t // S5_CHUNK * b, LANES), F32)
    return pl.pallas_call(
        _s5_pack_kernel if pack else _s5_unpack_kernel,
        grid=(t // tb,),
        in_specs=[tok if pack else grp],
        out_specs=grp if pack else tok,
        out_shape=grp_shape if pack else tok_shape,
        compiler_params=_params("parallel"),
        name="s5_pack" if pack else "s5_unpack",
    )(x)


def _s5(u, prm, h0_re, h0_im):
    m, w, v, a = prm
    b, tiles, t, _ = u.shape
    cw = tiles * LANES
    g = cw // C_GROUP
    n = t // S5_CHUNK
    rows = n * b
    width = S5_CHUNK * C_GROUP
    ug = _s5_relayout(u, b, t, cw, True)
    if h0_re is None:
        h0 = jnp.zeros((g, b, 4 * C_STATE), F32)
    else:
        h0 = jnp.concatenate([h0_re[:, 0], h0_re[:, 1], h0_im[:, 0], h0_im[:, 1]], axis=-1).transpose(1, 0, 2)
    gs = S5_GROUPS

    def grp(shape):
        nd = len(shape)
        return pl.BlockSpec((gs,) + shape[1:], lambda i: (i,) + (0,) * (nd - 1))

    y, fin = pl.pallas_call(
        functools.partial(_s5_kernel, nb=b),
        grid=(g // gs,),
        in_specs=[grp(ug.shape), grp(m.shape), grp(w.shape), grp(v.shape), grp(a.shape), grp(h0.shape)],
        out_specs=[grp(ug.shape), grp(h0.shape)],
        out_shape=[jax.ShapeDtypeStruct(ug.shape, F32), jax.ShapeDtypeStruct(h0.shape, F32)],
        scratch_shapes=[pltpu.VMEM((rows, width), F32)] * 3,
        compiler_params=_params("parallel"),
        name="s5_scan",
    )(ug, m, w, v, a, h0)
    y = _s5_relayout(y, b, t, cw, False)
    fin = fin.reshape(g, b, 2, 2, C_STATE).transpose(2, 1, 3, 0, 4)
    return y, fin[0], fin[1]


def _s5_params(lam_re, lam_im, log_dt, b_re, b_im, c_re, c_im):
    dt = jnp.exp(log_dt)[..., None]
    mag = jnp.exp(lam_re * dt)
    a_re = mag * jnp.cos(lam_im * dt)
    a_im = mag * jnp.sin(lam_im * dt)
    den = lam_re * lam_re + lam_im * lam_im
    f_re = ((a_re - 1.0) * lam_re + a_im * lam_im) / den
    f_im = (a_im * lam_re - (a_re - 1.0) * lam_im) / den
    fb_re = f_re[..., None] * b_re[None] - f_im[..., None] * b_im[None]
    fb_im = f_re[..., None] * b_im[None] + f_im[..., None] * b_re[None]
    g = lam_re.shape[1]
    L = S5_CHUNK
    k = jnp.arange(L + 1, dtype=F32)[:, None, None, None]
    pmag = jnp.exp(lam_re * dt * k)
    p_re = pmag * jnp.cos(lam_im * dt * k)
    p_im = pmag * jnp.sin(lam_im * dt * k)
    fbt_re = fb_re.transpose(0, 1, 3, 2)
    fbt_im = fb_im.transpose(0, 1, 3, 2)

    ca_re = c_re[None, None] * p_re[:, :, :, None, :] - c_im[None, None] * p_im[:, :, :, None, :]
    ca_im = c_re[None, None] * p_im[:, :, :, None, :] + c_im[None, None] * p_re[:, :, :, None, :]
    car = ca_re.transpose(1, 2, 0, 3, 4)[:, :, None]
    cai = ca_im.transpose(1, 2, 0, 3, 4)[:, :, None]
    kern = jnp.sum(car * fbt_re[:, :, :, None, None, :] - cai * fbt_im[:, :, :, None, None, :], axis=-1)
    wid = L * C_GROUP
    fwd = kern[0, :, :, :L].reshape(g, C_GROUP, wid)
    bwd = kern[1, :, :, L - 1::-1][:, :, :L].reshape(g, C_GROUP, wid)
    rows = []
    for s in range(L):
        f = jnp.pad(fwd[:, :, :wid - s * C_GROUP], ((0, 0), (0, 0), (s * C_GROUP, 0)))
        bk = jnp.pad(bwd[:, :, (L - 1 - s) * C_GROUP:], ((0, 0), (0, 0), (0, (L - 1 - s) * C_GROUP)))
        rows.append(f + bk)
    m = jnp.stack(rows, axis=1).reshape(g, wid, wid)

    def to_state(pw_re, pw_im, d):
        re = pw_re[:, :, None, :] * fbt_re[d][None] - pw_im[:, :, None, :] * fbt_im[d][None]
        im = pw_re[:, :, None, :] * fbt_im[d][None] + pw_im[:, :, None, :] * fbt_re[d][None]
        return re.transpose(1, 0, 2, 3), im.transpose(1, 0, 2, 3)

    wf_re, wf_im = to_state(p_re[L - 1::-1, 0][:L], p_im[L - 1::-1, 0][:L], 0)
    wb_re, wb_im = to_state(p_re[:L, 1], p_im[:L, 1], 1)
    w = jnp.concatenate([wf_re, wb_re, wf_im, wb_im], axis=-1).reshape(g, wid, 4 * C_STATE)

    def from_state(car, cai):
        return car.transpose(1, 3, 0, 2), -cai.transpose(1, 3, 0, 2)

    vf_re, vf_im = from_state(ca_re[1:, 0], ca_im[1:, 0])
    vb_re, vb_im = from_state(ca_re[L:0:-1, 1], ca_im[L:0:-1, 1])
    zero = jnp.zeros_like(vf_re)
    v = jnp.stack([jnp.concatenate([vf_re, zero, vf_im, zero], axis=1),
                   jnp.concatenate([zero, vb_re, zero, vb_im], axis=1)], axis=1)
    v = v.reshape(g, 2, 4 * C_STATE, L * C_GROUP)

    a = jnp.stack([jnp.concatenate([p_re[L, 0], p_re[L, 1]], axis=-1),
                   jnp.concatenate([p_im[L, 0], p_im[L, 1]], axis=-1)], axis=1)
    return m.astype(BF16), w.astype(BF16), v.astype(BF16), a


def _diff_attn_kernel(*refs, latent, kc):
    if latent:
        bnd_ref, lam_ref, q_ref, k_ref, v_ref, ck_ref, cv_ref, o_ref, cn_ref = refs
    else:
        bnd_ref, lam_ref, q_ref, k_ref, v_ref, o_ref = refs
    lq = q_ref.shape[1]
    lo = _lane_lo()
    q = q_ref[0]
    qs = (jnp.where(lo, q, 0.0).astype(BF16), jnp.where(lo, 0.0, q).astype(BF16))

    def over_keys(step, carry):
        if latent:
            carry = step(ck_ref[0], cv_ref[0], carry)

            def body(c, carry):
                rows = pl.ds(pl.multiple_of(c * kc, kc), kc)
                return step(k_ref[0, rows, :], v_ref[0, rows, :], carry)

            return lax.fori_loop(0, k_ref.shape[1] // kc, body, carry)
        return step(k_ref[0], v_ref[0], carry)

    def finish(l0, acc0, l1, acc1):
        o_ref[0] = acc0 / l0 - lam_ref[0] * (acc1 / l1)

    def online_step(kb, vb, carry):
        kb = kb.astype(BF16)
        vb = vb.astype(BF16)
        out = []
        for qc, (m, l, acc) in zip(qs, carry):
            s = _mm_nt(qc, kb)
            m_new = jnp.maximum(m, jnp.max(s, axis=1, keepdims=True))
            alpha = jnp.exp2(m - m_new)
            p = jnp.exp2(s - m_new)
            l = alpha * l + jnp.sum(p, axis=1, keepdims=True)
            acc = alpha * acc + _mm(p, vb)
            out.append((m_new, l, acc))
        return tuple(out)

    def online():
        init = (jnp.full((lq, 1), NEG_INF, F32), jnp.zeros((lq, 1), F32), jnp.zeros((lq, LANES), F32))
        (_, l0, acc0), (_, l1, acc1) = over_keys(online_step, (init, init))
        finish(l0, acc0, l1, acc1)

    if latent:
        @pl.when(pl.program_id(2) == 0)
        def _():
            cn_ref[0] = _max_sq_head_norm(ck_ref[0])

        bound_sq = bnd_ref[0] * jnp.maximum(bnd_ref[1], cn_ref[0])
    else:
        bound_sq = bnd_ref[0] * bnd_ref[1]
    fixed = bound_sq <= SHIFT_LIMIT * SHIFT_LIMIT

    @pl.when(fixed)
    def _():
        shift = jnp.sqrt(jnp.full((1, 1), bound_sq, F32))

        def step(kb, vb, carry):
            kb = kb.astype(BF16)
            vb = vb.astype(BF16)
            out = []
            for qc, (l, acc) in zip(qs, carry):
                p = jnp.exp2(_mm_nt(qc, kb) - shift)
                out.append((l + jnp.sum(p, axis=1, keepdims=True), acc + _mm(p, vb)))
            return tuple(out)

        init = (jnp.zeros((lq, 1), F32), jnp.zeros((lq, LANES), F32))
        (l0, acc0), (l1, acc1) = over_keys(step, (init, init))
        finish(l0, acc0, l1, acc1)

    @pl.when(jnp.logical_not(fixed))
    def _():
        online()


def _diff_attn(qkv, lam, cache=None, bounds=None, lq=512, kc=4096):
    b, t, _ = qkv.shape
    nh = 4
    latent = cache is not None
    kc = min(kc, t)
    assert t % kc == 0 and t % lq == 0
    in_specs = [pl.BlockSpec(memory_space=pltpu.SMEM),
                pl.BlockSpec(memory_space=pltpu.SMEM),
                pl.BlockSpec((1, lq, LANES), lambda i, h, j: (i, j, h)),
                pl.BlockSpec((1, t, LANES), lambda i, h, j: (i, 0, nh + h)),
                pl.BlockSpec((1, t, LANES), lambda i, h, j: (i, 0, 2 * nh + h))]
    args = [bounds, lam, qkv, qkv, qkv]
    if latent:
        tc = cache[0].shape[1]
        in_specs += [pl.BlockSpec((1, tc, LANES), lambda i, h, j: (i, 0, h))] * 2
        args += list(cache)
    return pl.pallas_call(
        functools.partial(_diff_attn_kernel, latent=latent, kc=kc),
        grid=(b, nh, t // lq),
        in_specs=in_specs,
        out_specs=pl.BlockSpec((1, lq, LANES), lambda i, h, j: (i, j, h)),
        out_shape=jax.ShapeDtypeStruct((b, t, nh * LANES), F32),
        scratch_shapes=[pltpu.SMEM((1,), F32)] if latent else [],
        compiler_params=_params("parallel", "parallel", "arbitrary"),
        name="diff_attn_lat" if latent else "diff_attn_ctx",
    )(*args)


def _resid_ffn(x, mix, g1, n2g, sh2, sc2, g2, w1_ref, w3_ref, w2_ref):
    x1 = x + g1 * mix
    h = (_rms(x1) * n2g * (1.0 + sc2) + sh2).astype(BF16)
    tiles = w1_ref.shape[1] // MXU_DIM
    cuts = [MXU_DIM * (tiles * s // FFN_SPLIT) for s in range(FFN_SPLIT + 1)]
    f = None
    for c0, c1 in zip(cuts[:-1], cuts[1:]):
        a = jnp.dot(h, w1_ref[:, c0:c1], preferred_element_type=F32)
        b = jnp.dot(h, w3_ref[:, c0:c1], preferred_element_type=F32)
        part = _mm(_silu(a) * b, w2_ref[c0:c1, :])
        f = part if f is None else f + part
    return x1 + g2 * f


def _out_ab_kernel(x_ref, oa_ref, ob_ref, gb_ref, rg_ref, wo_ref, g1_ref, n2g_ref, sh2_ref, sc2_ref,
                   g2_ref, w1_ref, w3_ref, w2_ref, y_ref):
    half = wo_ref.shape[0] // 2
    ob = ob_ref[0]
    inv = jnp.concatenate([_head_rsqrt(ob[:, c0:c0 + MXU_DIM]) for c0 in range(0, half, MXU_DIM)], axis=1)
    ob = ob * inv * rg_ref[...] * _silu(gb_ref[0])
    mix = _mm(oa_ref[0], wo_ref[:half, :]) + _mm(ob, wo_ref[half:, :])
    y_ref[0] = _resid_ffn(x_ref[0], mix, g1_ref[0], n2g_ref[...], sh2_ref[0], sc2_ref[0], g2_ref[0],
                          w1_ref, w3_ref, w2_ref)


def _out_cd_kernel(x_ref, ys_ref, u_ref, od_ref, dsk_ref, gw_ref, gbias_ref, sg_ref, wo_ref,
                   g1_ref, n2g_ref, sh2_ref, sc2_ref, g2_ref, w1_ref, w3_ref, w2_ref, y_ref, *, od_scale):
    half = wo_ref.shape[0] // 2
    tiles = range(ys_ref.shape[1])
    y = (jnp.concatenate([ys_ref[0, j] for j in tiles], axis=1)
         + dsk_ref[...] * jnp.concatenate([u_ref[0, j] for j in tiles], axis=1))
    g = jax.nn.gelu(y)
    oc = g * jax.nn.sigmoid(_mm(g, gw_ref[...]) + gbias_ref[...])
    od = jnp.concatenate([_rms(od_ref[0, :, c0:c0 + LANES]) * sg_ref[...] * od_scale
                          for c0 in range(0, half, LANES)], axis=1)
    mix = _mm(oc, wo_ref[:half, :]) + _mm(od, wo_ref[half:, :])
    y_ref[0] = _resid_ffn(x_ref[0], mix, g1_ref[0], n2g_ref[...], sh2_ref[0], sc2_ref[0], g2_ref[0],
                          w1_ref, w3_ref, w2_ref)


def _const_spec(shape):
    nd = len(shape)
    return pl.BlockSpec(shape, lambda i, j: (0,) * nd, pipeline_mode=pl.Buffered(1))


def _out_call(kern, name, x, tok_args, const_args, mod, ffn, tm):
    b, t, d = x.shape
    g1, sh2, sc2, g2 = mod
    n2g, w1, w3, w2 = ffn

    def tok(a):
        if a.ndim == 4:
            return pl.BlockSpec((1, a.shape[1], tm, LANES), lambda i, j: (i, 0, j, 0))
        return pl.BlockSpec((1, tm, a.shape[2]), lambda i, j: (i, j, 0))

    def per_batch(a):
        return pl.BlockSpec((1, 1, d), lambda i, j: (i, 0, 0))

    args = [x] + list(tok_args) + list(const_args) + [g1, n2g, sh2, sc2, g2, w1, w3, w2]
    in_specs = ([tok(x)] + [tok(a) for a in tok_args] + [_const_spec(a.shape) for a in const_args]
                + [per_batch(g1), _const_spec(n2g.shape), per_batch(sh2), per_batch(sc2), per_batch(g2),
                   _const_spec(w1.shape), _const_spec(w3.shape), _const_spec(w2.shape)])
    return pl.pallas_call(
        kern,
        grid=(b, t // tm),
        in_specs=in_specs,
        out_specs=tok(x),
        out_shape=jax.ShapeDtypeStruct(x.shape, F32),
        compiler_params=_params("parallel", "parallel"),
        name=name,
    )(*args)


def _ab_plan():
    plan = [(NORM_Q, 0, j) for j in range(4)] + [(NORM_K, 0, 4), (NORM_K, 0, 5), (RAW, 0, 6), (RAW, 0, 7)]
    plan += [(SCALE_Q, 1, j) for j in range(4)] + [(RAW, 1, 4 + j) for j in range(8)]
    plan += [(RAW, 2, j) for j in range(4)]
    return tuple(plan)


def _dup_kv_cols(w):
    q, kv, rest = w[:, :4 * LANES], w[:, 4 * LANES:6 * LANES], w[:, 6 * LANES:]
    kv = jnp.repeat(kv.reshape(w.shape[0], 4, 1, HEAD_DIM), 2, axis=2).reshape(w.shape[0], 4 * LANES)
    return jnp.concatenate([q, kv, rest], axis=1)


def _dup_heads(x):
    return jnp.repeat(x[..., :, None, :], 2, axis=-2).reshape(x.shape[:-2] + (4 * HEAD_DIM,))


def _cd_plan():
    plan = [(RAW, 0, j) for j in range(4)]
    plan += [(NORM_Q, 1, j) for j in range(4)] + [(NORM_K, 1, 4 + j) for j in range(4)]
    plan += [(RAW, 1, 8 + j) for j in range(4)]
    return tuple(plan)


AB_PLAN = _ab_plan()
CD_PLAN = _cd_plan()


def _rope_tables(t):
    rows = t // GRID_W
    r = jnp.repeat(jnp.arange(rows, dtype=F32), GRID_W)
    col = jnp.tile(jnp.arange(GRID_W, dtype=F32), rows)
    n_freq = HEAD_DIM // 4
    inv = ROPE_BASE ** (-jnp.arange(n_freq, dtype=F32) / n_freq)
    ang = jnp.concatenate([r[:, None] * inv, col[:, None] * inv], axis=-1)
    cos, sin = jnp.cos(ang), jnp.sin(ang)
    cos128 = jnp.tile(cos, (1, 4))
    sin128 = jnp.tile(jnp.concatenate([-sin, sin], axis=-1), (1, 2))
    return cos128, sin128


def _tile2(g):
    return jnp.tile(g.astype(F32), 2).reshape(1, LANES)


def kernel(x_prompt, x_sample, c, c_ctx, cache_k_a, cache_v_a, state_ret, state_ssm_re, state_ssm_im,
           cache_k_d, cache_v_d, ada_w, ada_b, norm1_g, norm2_g, ffn_w1, ffn_w3, ffn_w2,
           ab_w_in, ab_w_out, a_q_norm, a_k_norm, a_sink, ret_decay, ret_norm,
           cd_w_in, cd_w_out, ssm_lambda_re, ssm_lambda_im, ssm_log_dt, ssm_b_re, ssm_b_im,
           ssm_c_re, ssm_c_im, ssm_d, ssm_glu_w, ssm_glu_b, d_q_norm, d_k_norm, d_lambda, d_subln):
    nbc, tcx, d = x_prompt.shape
    nbl, tl, _ = x_sample.shape
    depth = ada_w.shape[0]
    rope_tabs = _rope_tables(tl)

    ncond = -(-(nbl + 1) // SUBLANES) * SUBLANES
    cond = jnp.zeros((ncond, d), F32).at[:nbl].set(c).at[nbl].set(c_ctx)
    mods = _ada(cond, ada_w, ada_b)

    def mod_split(l, lo, hi):
        m = mods[l, lo:hi].reshape(hi - lo, 1, 6, d)
        return [m[:, :, k] for k in range(6)]

    yp = x_prompt.reshape(1, nbc * tcx, d)
    ys = x_sample
    new_ka, new_va, new_ret, new_sr, new_si, new_kd, new_vd = [], [], [], [], [], [], []
    for l in range(depth):
        j = l // 2
        ffn = (norm2_g[l].reshape(1, d), ffn_w1[l].astype(BF16), ffn_w3[l].astype(BF16), ffn_w2[l].astype(BF16))
        n1g = norm1_g[l].reshape(1, d)
        sh1c, sc1c, g1c, sh2c, sc2c, g2c = mod_split(l, nbl, nbl + 1)
        sh1l, sc1l, g1l, sh2l, sc2l, g2l = mod_split(l, 0, nbl)
        if l % 2 == 0:
            w_in = _dup_kv_cols(ab_w_in[j]).astype(BF16)
            w_out = ab_w_out[j].astype(BF16)
            gq, gk = _tile2(a_q_norm[j]), _tile2(a_k_norm[j])
            rg = jnp.tile(ret_norm[j].astype(F32), ab_w_out.shape[1] // 2 // HEAD_DIM).reshape(1, -1)
            log_g = jax.nn.log_sigmoid(ret_decay[j].astype(F32))
            widths = (8 * LANES, 12 * LANES, 4 * LANES)
            a_scale = Q_SCALE
            qkv_a, qkv_b, gb = _inproj(yp, sh1c, sc1c, n1g, w_in, gq, gk, None, AB_PLAN, widths,
                                       (F32, F32, F32), a_scale, TOKEN_TILE, "inproj_ab_ctx")
            qkv_a = qkv_a.reshape(nbc, tcx, -1)
            bounds = _norm_bounds(a_q_norm[j], a_k_norm[j])
            oa = _attn_a_ctx(qkv_a, a_sink[j], bounds)
            ob, s_ret = _retention(qkv_b.reshape(nbc, tcx, -1), log_g, None)
            yp = _out_call(_out_ab_kernel, "out_ab_ctx", yp,
                           [oa.reshape(1, nbc * tcx, -1), ob.reshape(1, nbc * tcx, -1), gb],
                           [rg, w_out], (g1c, sh2c, sc2c, g2c), ffn, TOKEN_TILE)
            new_ka.append(qkv_a[:, :, 4 * LANES:6 * LANES].reshape(nbc, tcx, 2, 2, HEAD_DIM)[:, :, :, 0])
            new_va.append(qkv_a[:, :, 6 * LANES:8 * LANES].reshape(nbc, tcx, 2, 2, HEAD_DIM)[:, :, :, 0])
            s_ret = jnp.stack([s_ret[..., :HEAD_DIM, :HEAD_DIM], s_ret[..., HEAD_DIM:, HEAD_DIM:]], axis=2)
            new_ret.append(s_ret.transpose(0, 3, 1, 2, 4, 5).reshape(nbc, 2, 8, HEAD_DIM, HEAD_DIM))
            qkv_a, qkv_b, gb = _inproj(ys, sh1l, sc1l, n1g, w_in, gq, gk, rope_tabs, AB_PLAN, widths,
                                       (BF16, F32, F32), a_scale, TOKEN_TILE, "inproj_ab_lat")
            ck = _dup_heads(cache_k_a[:, j]).astype(BF16)
            cv = _dup_heads(cache_v_a[:, j]).astype(BF16)
            oa = _attn_a_lat(qkv_a, ck, cv, a_sink[j], bounds)
            s0 = state_ret[:, j].reshape(nbl, 2, 4, 2, HEAD_DIM, HEAD_DIM)
            z = jnp.zeros_like(s0[:, :, :, 0])
            s0 = jnp.concatenate([jnp.concatenate([s0[:, :, :, 0], z], axis=-1),
                                  jnp.concatenate([z, s0[:, :, :, 1]], axis=-1)], axis=-2)
            s0 = s0.transpose(0, 2, 1, 3, 4)
            ob, _ = _retention(qkv_b, log_g, s0)
            ys = _out_call(_out_ab_kernel, "out_ab_lat", ys, [oa, ob, gb], [rg, w_out],
                           (g1l, sh2l, sc2l, g2l), ffn, TOKEN_TILE)
        else:
            lam_init = 0.8 - 0.6 * math.exp(-0.3 * l)
            w_in = cd_w_in[j].astype(BF16)
            w_out = cd_w_out[j].astype(BF16)
            gq, gk = _tile2(d_q_norm[j]), _tile2(d_k_norm[j])
            lp = d_lambda[j].astype(F32)
            lam = (jnp.exp(jnp.sum(lp[0] * lp[1])) - jnp.exp(jnp.sum(lp[2] * lp[3])) + lam_init).reshape(1)
            prm = _s5_params(ssm_lambda_re[j], ssm_lambda_im[j], ssm_log_dt[j], ssm_b_re[j], ssm_b_im[j],
                             ssm_c_re[j], ssm_c_im[j])
            consts =[ssm_d[j].reshape(1, -1), ssm_glu_w[j].astype(BF16), ssm_glu_b[j].reshape(1, -1),
                      d_subln[j].reshape(1, LANES), w_out]
            kern = functools.partial(_out_cd_kernel, od_scale=1.0 - lam_init)
            widths = (-4 * LANES, 12 * LANES)
            d_scale = Q_SCALE
            u, qkv_d = _inproj(yp, sh1c, sc1c, n1g, w_in, gq, gk, None, CD_PLAN, widths,
                               (F32, F32), d_scale, TOKEN_TILE, "inproj_cd_ctx")
            u_b = u.reshape(-1, nbc, tcx, LANES).transpose(1, 0, 2, 3)
            y, fin_re, fin_im = _s5(u_b, prm, None, None)
            y = y.transpose(1, 0, 2, 3).reshape(u.shape)
            qkv_d = qkv_d.reshape(nbc, tcx, -1)
            d_bounds = _norm_bounds(d_q_norm[j], d_k_norm[j])
            od = _diff_attn(qkv_d, lam, None, d_bounds, lq=tcx)
            yp = _out_call(kern, "out_cd_ctx", yp, [y, u, od.reshape(1, nbc * tcx, -1)],
                           consts, (g1c, sh2c, sc2c, g2c), ffn, TOKEN_TILE)
            new_sr.append(fin_re)
            new_si.append(fin_im)
            new_kd.append(qkv_d[:, :, 4 * LANES:8 * LANES].reshape(nbc, tcx, 4, 2, HEAD_DIM))
            new_vd.append(qkv_d[:, :, 8 * LANES:].reshape(nbc, tcx, 4, 2 * HEAD_DIM))
            u, qkv_d = _inproj(ys, sh1l, sc1l, n1g, w_in, gq, gk, rope_tabs, CD_PLAN, widths,
                               (F32, BF16), d_scale, TOKEN_TILE, "inproj_cd_lat")
            y, _, _ = _s5(u, prm, state_ssm_re[:, j], state_ssm_im[:, j])
            ckd = cache_k_d[:, j].reshape(nbl, -1, 4 * LANES)
            cvd = cache_v_d[:, j].reshape(nbl, -1, 4 * LANES)
            od = _diff_attn(qkv_d, lam, (ckd, cvd), d_bounds)
            ys = _out_call(kern, "out_cd_lat", ys, [y, u, od], consts, (g1l, sh2l, sc2l, g2l), ffn, TOKEN_TILE)
    return (yp.reshape(nbc, tcx, d), ys,
            jnp.stack(new_ka, axis=1), jnp.stack(new_va, axis=1), jnp.stack(new_ret, axis=1),
            jnp.stack(new_sr, axis=1), jnp.stack(new_si, axis=1),
            jnp.stack(new_kd, axis=1), jnp.stack(new_vd, axis=1))
```

```python
import functools
import math

import jax
import jax.numpy as jnp
from jax import lax
from jax.experimental import pallas as pl
from jax.experimental.pallas import tpu as pltpu

F32 = jnp.float32
BF16 = jnp.bfloat16

LANES = 128
SUBLANES = 8
HEAD_DIM = 64
GRID_W = 64
ROPE_BASE = 10000.0
WINDOW = 128
RET_CHUNK = 128
RET_UNROLL = 16
RET_SHORT = 512
FFN_SPLIT = 2
TOKEN_TILE = 512
C_GROUP = 16
C_STATE = 64
NEG_INF = -1e30
EPS = 1e-6
LOG2E = math.log2(math.e)
Q_SCALE = HEAD_DIM ** -0.5 * LOG2E
SHIFT_LIMIT = 50.0
VMEM_LIMIT = 56 * 1024 * 1024


def _params(*sem):
    return pltpu.CompilerParams(dimension_semantics=sem, vmem_limit_bytes=VMEM_LIMIT)


def _mm(a, b):
    return jnp.dot(a.astype(BF16), b.astype(BF16), preferred_element_type=F32)


def _mm_nt(a, b):
    return lax.dot_general(a.astype(BF16), b.astype(BF16), (((1,), (1,)), ((), ())),
                           preferred_element_type=F32)


def _mm_tn(a, b):
    return lax.dot_general(a.astype(BF16), b.astype(BF16), (((0,), (0,)), ((), ())),
                           preferred_element_type=F32)


def _rms(x):
    return x * lax.rsqrt(jnp.mean(x * x, axis=-1, keepdims=True) + EPS)


def _silu(x):
    return x * jax.nn.sigmoid(x)


def _lane_lo():
    return lax.broadcasted_iota(jnp.int32, (1, LANES), 1) < HEAD_DIM


def _same_head():
    r = lax.broadcasted_iota(jnp.int32, (LANES, LANES), 0) < HEAD_DIM
    c = lax.broadcasted_iota(jnp.int32, (LANES, LANES), 1) < HEAD_DIM
    return r == c


def _head_ones():
    return _same_head().astype(BF16)


MXU_DIM = 256


def _head_rsqrt(x):
    r = lax.broadcasted_iota(jnp.int32, (MXU_DIM, MXU_DIM), 0) >> (HEAD_DIM.bit_length() - 1)
    c = lax.broadcasted_iota(jnp.int32, (MXU_DIM, MXU_DIM), 1) >> (HEAD_DIM.bit_length() - 1)
    ss = jnp.dot((x * x).astype(BF16), (r == c).astype(BF16), preferred_element_type=F32)
    return lax.rsqrt(ss * (1.0 / HEAD_DIM) + EPS)


def _max_sq_head_norm(x):
    bd = _head_ones()
    best = None
    for j in range(x.shape[1] // LANES):
        xf = x[:, LANES * j:LANES * (j + 1)].astype(F32)
        m = jnp.max(jnp.dot((xf * xf).astype(BF16), bd, preferred_element_type=F32))
        best = m if best is None else jnp.maximum(best, m)
    return best


def _norm_bounds(gq, gk):
    q_sq = HEAD_DIM * Q_SCALE * Q_SCALE * jnp.max(gq.astype(F32) ** 2)
    k_sq = HEAD_DIM * jnp.max(gk.astype(F32) ** 2)
    return jnp.stack([q_sq, k_sq])


def _rope128(x, cos, sin_signed):
    lane = lax.broadcasted_iota(jnp.int32, (1, LANES), 1)
    first = (lane & (HEAD_DIM // 2)) == 0
    partner = jnp.where(first, pltpu.roll(x, LANES - HEAD_DIM // 2, 1), pltpu.roll(x, HEAD_DIM // 2, 1))
    return x * cos + partner * sin_signed


def _ada_kernel(c_ref, w_ref, b_ref, o_ref):
    o_ref[0] = _mm(_silu(c_ref[...]), w_ref[0]) + b_ref[0]


def _ada(cond, ada_w, ada_b):
    depth, d, n = ada_w.shape
    tn = n // 4
    return pl.pallas_call(
        _ada_kernel,
        grid=(depth, n // tn),
        in_specs=[pl.BlockSpec(cond.shape, lambda l, j: (0, 0)),
                  pl.BlockSpec((1, d, tn), lambda l, j: (l, 0, j)),
                  pl.BlockSpec((1, 1, tn), lambda l, j: (l, 0, j))],
        out_specs=pl.BlockSpec((1, cond.shape[0], tn), lambda l, j: (l, 0, j)),
        out_shape=jax.ShapeDtypeStruct((depth, cond.shape[0], n), F32),
        compiler_params=_params("parallel", "parallel"),
        name="ada_modulation",
    )(cond, ada_w, ada_b.reshape(depth, 1, n))


RAW, NORM_Q, NORM_K, SCALE_Q = 0, 1, 2, 3


def _inproj_kernel(*refs, plan, rope, q_scale):
    x_ref, sh_ref, sc_ref, ng_ref, w_ref, gq_ref, gk_ref = refs[:7]
    pos = 7
    if rope:
        cos_ref, sin_ref = refs[7:9]
        pos = 9
    outs = refs[pos:]
    x = x_ref[0]
    h = _rms(x) * ng_ref[...] * (1.0 + sc_ref[0]) + sh_ref[0]
    p = _mm(h, w_ref[...])
    inv_rms = {}
    for j, (op, oi, dj) in enumerate(plan):
        chunk = p[:, LANES * j:LANES * (j + 1)]
        if op in (NORM_Q, NORM_K):
            g = gq_ref[...] if op == NORM_Q else gk_ref[...]
            j0 = j - j % 2
            if j0 not in inv_rms:
                inv_rms[j0] = _head_rsqrt(p[:, LANES * j0:LANES * (j0 + 2)])
            chunk = chunk * inv_rms[j0][:, LANES * (j - j0):LANES * (j - j0 + 1)] * g
            if rope:
                chunk = _rope128(chunk, cos_ref[...], sin_ref[...])
            if op == NORM_Q:
                chunk = chunk * q_scale
        elif op == SCALE_Q:
            chunk = chunk * (HEAD_DIM ** -0.5)
        if len(outs[oi].shape) == 4:
            outs[oi][0, dj] = chunk.astype(outs[oi].dtype)
        else:
            outs[oi][0, :, LANES * dj:LANES * (dj + 1)] = chunk.astype(outs[oi].dtype)


def _inproj(x, sh, sc, ng, w, gq, gk, rope_tabs, plan, out_widths, out_dtypes, q_scale, tm, name):
    b, t, d = x.shape
    n = w.shape[1]
    rope = rope_tabs is not None
    in_specs = [pl.BlockSpec((1, tm, d), lambda i, j: (i, j, 0)),
                pl.BlockSpec((1, 1, d), lambda i, j: (i, 0, 0)),
                pl.BlockSpec((1, 1, d), lambda i, j: (i, 0, 0)),
                pl.BlockSpec((1, d), lambda i, j: (0, 0)),
                pl.BlockSpec((d, n), lambda i, j: (0, 0)),
                pl.BlockSpec((1, LANES), lambda i, j: (0, 0)),
                pl.BlockSpec((1, LANES), lambda i, j: (0, 0))]
    args = [x, sh, sc, ng, w, gq, gk]
    if rope:
        in_specs += [pl.BlockSpec((tm, LANES), lambda i, j: (j, 0))] * 2
        args += list(rope_tabs)
    return pl.pallas_call(
        functools.partial(_inproj_kernel, plan=plan, rope=rope, q_scale=q_scale),
        grid=(b, t // tm),
        in_specs=in_specs,
        out_specs=[pl.BlockSpec((1, tm, wd), lambda i, j: (i, j, 0)) if wd > 0 else
                   pl.BlockSpec((1, -wd // LANES, tm, LANES), lambda i, j: (i, 0, j, 0)) for wd in out_widths],
        out_shape=[jax.ShapeDtypeStruct((b, t, wd) if wd > 0 else (b, -wd // LANES, t, LANES), dt)
                   for wd, dt in zip(out_widths, out_dtypes)],
        compiler_params=_params("parallel", "parallel"),
        name=name,
    )(*args)


def _attn_a_kernel(*refs, latent, lq):
    if latent:
        (bnd_ref, sink_ref, q_ref, kp_ref, kc_ref, kn_ref, vp_ref, vc_ref, vn_ref, ck_ref, cv_ref, o_ref,
         cn_ref) = refs
        i = pl.program_id(1)

        @pl.when(i == 0)
        def _():
            cn_ref[0] = _max_sq_head_norm(ck_ref[0])

        k_sq = jnp.maximum(bnd_ref[1], cn_ref[0])
        last = pl.num_programs(1) - 1
        k_all = jnp.concatenate([ck_ref[0], kp_ref[0], kc_ref[0], kn_ref[0]], axis=0)
        v_all = jnp.concatenate([cv_ref[0], vp_ref[0], vc_ref[0], vn_ref[0]], axis=0)
        tc = ck_ref.shape[1]
        assert lq & (lq - 1) == 0
        r = lax.broadcasted_iota(jnp.int32, (2 * lq, lq), 0) & (lq - 1)
        c = lax.broadcasted_iota(jnp.int32, (2 * lq, lq), 1)
        piece_masks = {tc: (c >= r) & (i > 0), tc + 2 * lq: (c <= r) & (i < last)}
    else:
        bnd_ref, sink_ref, q_ref, k_ref, v_ref, o_ref = refs
        k_all = k_ref[0]
        v_all = v_ref[0]
        piece_masks = {}
        k_sq = bnd_ref[1]
    nk = k_all.shape[0]
    lo = _lane_lo()
    row = lax.broadcasted_iota(jnp.int32, (2 * lq, 1), 0)

    def masked(x, fill):
        if not piece_masks:
            return x
        pieces = [jnp.where(piece_masks[o], x[:, o:o + lq], fill) if o in piece_masks else x[:, o:o + lq]
                  for o in range(0, nk, lq)]
        return jnp.concatenate(pieces, axis=1)

    sinks = [sink_ref[h] * LOG2E for h in range(8)]
    sink_max = functools.reduce(jnp.maximum, sinks)
    bound_sq = bnd_ref[0] * k_sq
    fixed = (bound_sq <= SHIFT_LIMIT * SHIFT_LIMIT) & (sink_max <= SHIFT_LIMIT)

    def attend(softmax):
        for g in range(2):
            k2 = k_all[:, LANES * g:LANES * (g + 1)].astype(BF16)
            v2 = v_all[:, LANES * g:LANES * (g + 1)].astype(BF16)
            for jp in range(2):
                j = 2 * g + jp
                q = q_ref[0, :, LANES * j:LANES * (j + 1)]
                q2 = jnp.concatenate([jnp.where(lo, q, 0.0), jnp.where(lo, 0.0, q)], axis=0)
                sk = jnp.where(row < lq, sinks[2 * j], sinks[2 * j + 1])
                p, den = softmax(_mm_nt(q2, k2), sk)
                o2 = _mm(p, v2) / den
                o_ref[0, :, LANES * j:LANES * (j + 1)] = jnp.where(lo, o2[:lq], o2[lq:])

    @pl.when(fixed)
    def _():
        shift = jnp.maximum(jnp.sqrt(jnp.full((1, 1), bound_sq, F32)), sink_max)

        def softmax(s, sk):
            p = masked(jnp.exp2(s - shift), 0.0)
            return p, jnp.sum(p, axis=1, keepdims=True) + jnp.exp2(sk - shift)

        attend(softmax)

    @pl.when(jnp.logical_not(fixed))
    def _():
        def softmax(s, sk):
            s = masked(s, NEG_INF)
            m = jnp.maximum(jnp.max(s, axis=1, keepdims=True), sk)
            p = jnp.exp2(s - m)
            return p, jnp.sum(p, axis=1, keepdims=True) + jnp.exp2(sk - m)

        attend(softmax)


def _attn_a_ctx(qkv, sink, bounds):
    b, t, _ = qkv.shape
    return pl.pallas_call(
        functools.partial(_attn_a_kernel, latent=False, lq=t),
        grid=(b,),
        in_specs=[pl.BlockSpec(memory_space=pltpu.SMEM),
                  pl.BlockSpec(memory_space=pltpu.SMEM),
                  pl.BlockSpec((1, t, 4 * LANES), lambda i: (i, 0, 0)),
                  pl.BlockSpec((1, t, 2 * LANES), lambda i: (i, 0, 2)),
                  pl.BlockSpec((1, t, 2 * LANES), lambda i: (i, 0, 3))],
        out_specs=pl.BlockSpec((1, t, 4 * LANES), lambda i: (i, 0, 0)),
        out_shape=jax.ShapeDtypeStruct((b, t, 4 * LANES), F32),
        compiler_params=_params("parallel"),
        name="attn_a_ctx",
    )(bounds, sink, qkv, qkv, qkv)


def _attn_a_lat(qkv, ck, cv, sink, bounds):
    b, t, _ = qkv.shape
    lq = WINDOW
    nb = t // lq
    tc = ck.shape[1]

    def kv_spec(col, off):
        return pl.BlockSpec((1, lq, 2 * LANES), lambda i, j: (i, jnp.clip(j + off, 0, nb - 1), col))

    return pl.pallas_call(
        functools.partial(_attn_a_kernel, latent=True, lq=lq),
        grid=(b, nb),
        in_specs=[pl.BlockSpec(memory_space=pltpu.SMEM),
                  pl.BlockSpec(memory_space=pltpu.SMEM),
                  pl.BlockSpec((1, lq, 4 * LANES), lambda i, j: (i, j, 0)),
                  kv_spec(2, -1), kv_spec(2, 0), kv_spec(2, 1),
                  kv_spec(3, -1), kv_spec(3, 0), kv_spec(3, 1),
                  pl.BlockSpec((1, tc, 2 * LANES), lambda i, j: (i, 0, 0)),
                  pl.BlockSpec((1, tc, 2 * LANES), lambda i, j: (i, 0, 0))],
        out_specs=pl.BlockSpec((1, lq, 4 * LANES), lambda i, j: (i, j, 0)),
        out_shape=jax.ShapeDtypeStruct((b, t, 4 * LANES), F32),
        scratch_shapes=[pltpu.SMEM((1,), F32)],
        compiler_params=_params("parallel", "arbitrary"),
        name="attn_a_lat",
    )(bounds, sink, qkv, qkv, qkv, qkv, qkv, qkv, qkv, ck, cv)


def _retention_kernel(*refs, zero_init, pps):
    for p in range(pps):
        pr = pl.program_id(1) * pps + p
        _retention_pair(refs, zero_init, pr, slice(LANES * p, LANES * (p + 1)), p)


def _retention_pair(refs, zero_init, pr, lanes, pi):
    if zero_init:
        lg_ref, q_ref, k_ref, v_ref, o_ref, sfin_ref, sp_ref = refs
    else:
        lg_ref, q_ref, k_ref, v_ref, s0_ref, o_ref, sfin_ref, sp_ref = refs
    L = RET_CHUNK
    n = q_ref.shape[1] // L
    lo = _lane_lo()
    lgf = jnp.where(lo, lg_ref[0, 2 * pr], lg_ref[0, 2 * pr + 1])
    lgb = jnp.where(lo, lg_ref[1, 2 * pr], lg_ref[1, 2 * pr + 1])
    idx = lax.broadcasted_iota(jnp.int32, (L, 1), 0).astype(F32)
    qdf = jnp.exp(lgf * (idx + 1.0))
    kdf = jnp.exp(lgf * (L - 1.0 - idx))
    cdf = jnp.exp(lgf * float(L))
    qdb = jnp.exp(lgb * (L - idx))
    kdb = jnp.exp(lgb * idx)
    cdb = jnp.exp(lgb * float(L))
    diff = (lax.broadcasted_iota(jnp.int32, (L, L), 0) - lax.broadcasted_iota(jnp.int32, (L, L), 1)).astype(F32)

    def decay(h):
        f = jnp.exp(lg_ref[0, 2 * pr + h] * jnp.maximum(diff, 0.0))
        bwd = jnp.exp(lg_ref[1, 2 * pr + h] * jnp.maximum(-diff, 0.0))
        return jnp.where(diff > 0, f, jnp.where(diff < 0, bwd, 2.0))

    d0 = decay(0)
    d1 = decay(1)
    blk = _same_head()

    if zero_init:
        sf0 = jnp.zeros((LANES, LANES), F32)
        sb0 = jnp.zeros((LANES, LANES), F32)
    else:
        sf0 = s0_ref[0, pi, 0]
        sb0 = s0_ref[0, pi, 1]

    qd = jnp.concatenate([qdf, qdb], axis=1)
    kd = jnp.concatenate([kdf, kdb], axis=1)
    blk2 = jnp.concatenate([blk, blk], axis=0)
    unroll = math.gcd(n, RET_UNROLL)

    def chunk(c):
        return pl.ds(pl.multiple_of(c * L, L), L)

    def kv_body(c, carry):
        k = k_ref[0, chunk(c), lanes]
        kv = _mm_tn(jnp.concatenate([k, k], axis=1) * kd, v_ref[0, chunk(c), lanes])
        sp_ref[c] = jnp.where(blk2, kv, 0.0)
        return carry

    lax.fori_loop(0, n, kv_body, 0, unroll=unroll)

    def f_body(c, sf):
        kv = sp_ref[c, :LANES, :]
        sp_ref[c, :LANES, :] = sf
        return sf * cdf + kv

    def b_body(i, sb):
        c = n - 1 - i
        kv = sp_ref[c, LANES:, :]
        sp_ref[c, LANES:, :] = sb
        return sb * cdb + kv

    sfin_ref[0, pi, 0] = lax.fori_loop(0, n, f_body, sf0)
    sfin_ref[0, pi, 1] = lax.fori_loop(0, n, b_body, sb0)

    def o_body(c, carry):
        q = q_ref[0, chunk(c), lanes]
        v = v_ref[0, chunk(c), lanes]
        s = _mm_nt(jnp.concatenate([jnp.where(lo, q, 0.0), jnp.where(lo, 0.0, q)], axis=0),
                   k_ref[0, chunk(c), lanes])
        sd = jnp.concatenate([s[:L] * d0, s[L:] * d1], axis=1)
        v2 = jnp.concatenate([jnp.where(lo, v, 0.0), jnp.where(lo, 0.0, v)], axis=0)
        o_ref[0, chunk(c), lanes] = _mm(sd, v2) + _mm(jnp.concatenate([q, q], axis=1) * qd, sp_ref[c])
        return carry

    lax.fori_loop(0, n, o_body, 0, unroll=unroll)


def _retention(qkv, log_g, s0):
    b, t, _ = qkv.shape
    npair = 4
    pps = npair if t <= RET_SHORT else 1
    steps = npair // pps
    wide = LANES * pps
    in_specs = [pl.BlockSpec(memory_space=pltpu.SMEM),
                pl.BlockSpec((1, t, wide), lambda i, p: (i, 0, p)),
                pl.BlockSpec((1, t, wide), lambda i, p: (i, 0, steps + p)),
                pl.BlockSpec((1, t, wide), lambda i, p: (i, 0, 2 * steps + p))]
    args = [log_g, qkv, qkv, qkv]
    if s0 is not None:
        in_specs.append(pl.BlockSpec((1, pps, 2, LANES, LANES), lambda i, p: (i, p, 0, 0, 0)))
        args.append(s0)
    return pl.pallas_call(
        functools.partial(_retention_kernel, zero_init=s0 is None, pps=pps),
        grid=(b, steps),
        in_specs=in_specs,
        out_specs=[pl.BlockSpec((1, t, wide), lambda i, p: (i, 0, p)),
                   pl.BlockSpec((1, pps, 2, LANES, LANES), lambda i, p: (i, p, 0, 0, 0))],
        out_shape=[jax.ShapeDtypeStruct((b, t, npair * LANES), F32),
                   jax.ShapeDtypeStruct((b, npair, 2, LANES, LANES), F32)],
        scratch_shapes=[pltpu.VMEM((t // RET_CHUNK, 2 * LANES, LANES), F32)],
        compiler_params=_params("parallel", "parallel"),
        name="retention",
    )(*args)


S5_CHUNK = 16
S5_GROUPS = 4


def _s5_kernel(u_ref, m_ref, w_ref, v_ref, a_ref, h0_ref, y_ref, fin_ref, z_ref, sf_ref, sb_ref, *, nb):
    rows = u_ref.shape[2]
    n = rows // nb
    lo = _lane_lo()
    nbg = nb // SUBLANES
    for gi in range(u_ref.shape[0]):
        u = jnp.concatenate([u_ref[gi, t2] for t2 in range(u_ref.shape[1])], axis=1).astype(BF16)
        z_ref[...] = jnp.dot(u, w_ref[gi], preferred_element_type=F32)
        are = jnp.broadcast_to(a_ref[gi, 0:1, :], (SUBLANES, LANES))
        aim = jnp.broadcast_to(a_ref[gi, 1:2, :], (SUBLANES, LANES))

        def step(i, carry):
            out = []
            for bg in range(nbg):
                hre, him = carry[bg]
                up = pl.ds(pl.multiple_of(i * nb + bg * SUBLANES, SUBLANES), SUBLANES)
                dn = pl.ds(pl.multiple_of((n - 1 - i) * nb + bg * SUBLANES, SUBLANES), SUBLANES)
                sf_ref[up, :LANES] = hre
                sf_ref[up, LANES:] = him
                sb_ref[dn, :LANES] = hre
                sb_ref[dn, LANES:] = him
                zre = jnp.where(lo, z_ref[up, :LANES], z_ref[dn, :LANES])
                zim = jnp.where(lo, z_ref[up, LANES:], z_ref[dn, LANES:])
                out.append((are * hre - aim * him + zre, are * him + aim * hre + zim))
            return tuple(out)

        init = tuple((h0_ref[gi, bg * SUBLANES:(bg + 1) * SUBLANES, :LANES],
                      h0_ref[gi, bg * SUBLANES:(bg + 1) * SUBLANES, LANES:]) for bg in range(nbg))
        last = lax.fori_loop(0, n, step, init, unroll=2)
        for bg in range(nbg):
            fin_ref[gi, bg * SUBLANES:(bg + 1) * SUBLANES, :LANES] = last[bg][0]
            fin_ref[gi, bg * SUBLANES:(bg + 1) * SUBLANES, LANES:] = last[bg][1]
        y = (jnp.dot(u, m_ref[gi], preferred_element_type=F32)
             + _mm(sf_ref[...], v_ref[gi, 0]) + _mm(sb_ref[...], v_ref[gi, 1]))
        for t2 in range(y_ref.shape[1]):
            y_ref[gi, t2] = y[:, LANES * t2:LANES * (t2 + 1)]


GROUPS_PER_TILE = LANES // C_GROUP


def _lane_group():
    return lax.broadcasted_iota(jnp.int32, (1, LANES), 1) >> (C_GROUP.bit_length() - 1)


S5_TILES = S5_CHUNK * C_GROUP // LANES


def _swap_blocks(sets, grp):
    sets = [list(xs) for xs in sets]
    d = GROUPS_PER_TILE // 2
    while d >= 1:
        keep = (grp & d) == 0
        for xs in sets:
            for i in range(GROUPS_PER_TILE):
                if i & d:
                    continue
                lo_x, hi_x = xs[i], xs[i + d]
                xs[i] = jnp.where(keep, lo_x, pltpu.roll(hi_x, C_GROUP * d, 1))
                xs[i + d] = jnp.where(keep, pltpu.roll(lo_x, LANES - C_GROUP * d, 1), hi_x)
        d //= 2
    return sets


def _s5_pack_kernel(u_ref, o_ref):
    nb, tiles, tb, _ = u_ref.shape
    n = tb // S5_CHUNK
    grp = _lane_group()

    def per_batch(b, carry):
        keys = [(j, t2) for j in range(tiles) for t2 in range(S5_TILES)]
        sets = [[u_ref[b, j, pl.ds(GROUPS_PER_TILE * t2 + tl, n, stride=S5_CHUNK), :]
                 for tl in range(GROUPS_PER_TILE)] for j, t2 in keys]
        for (j, t2), xs in zip(keys, _swap_blocks(sets, grp)):
            for q, x in enumerate(xs):
                o_ref[GROUPS_PER_TILE * j + q, t2, pl.ds(b, n, stride=nb), :] = x
        return carry

    lax.fori_loop(0, nb, per_batch, 0)


def _s5_unpack_kernel(y_ref, o_ref):
    nb, tiles, tb, _ = o_ref.shape
    n = tb // S5_CHUNK
    grp = _lane_group()

    def per_batch(b, carry):
        keys = [(j, t2) for j in range(tiles) for t2 in range(S5_TILES)]
        sets = [[y_ref[GROUPS_PER_TILE * j + q, t2, pl.ds(b, n, stride=nb), :] for q in range(GROUPS_PER_TILE)]
                for j, t2 in keys]
        for (j, t2), xs in zip(keys, _swap_blocks(sets, grp)):
            for tl, x in enumerate(xs):
                o_ref[b, j, pl.ds(GROUPS_PER_TILE * t2 + tl, n, stride=S5_CHUNK), :] = x
        return carry

    lax.fori_loop(0, nb, per_batch, 0)


def _s5_relayout(x, b, t, cw, pack):
    g = cw // C_GROUP
    tb = max(S5_CHUNK * SUBLANES, 2048 // b)
    tok = pl.BlockSpec((b, cw // LANES, tb, LANES), lambda i: (0, 0, i, 0))
    grp = pl.BlockSpec((g, S5_TILES, tb // S5_CHUNK * b, LANES), lambda i: (0, 0, i, 0))
    tok_shape = jax.ShapeDtypeStruct((b, cw // LANES, t, LANES), F32)
    grp_shape = jax.ShapeDtypeStruct((g, S5_TILES, t // S5_CHUNK * b, LANES), F32)
    return pl.pallas_call(
        _s5_pack_kernel if pack else _s5_unpack_kernel,
        grid=(t // tb,),
        in_specs=[tok if pack else grp],
        out_specs=grp if pack else tok,
        out_shape=grp_shape if pack else tok_shape,
        compiler_params=_params("parallel"),
        name="s5_pack" if pack else "s5_unpack",
    )(x)


def _s5(u, prm, h0_re, h0_im):
    m, w, v, a = prm
    b, tiles, t, _ = u.shape
    cw = tiles * LANES
    g = cw // C_GROUP
    n = t // S5_CHUNK
    rows = n * b
    width = S5_CHUNK * C_GROUP
    ug = _s5_relayout(u, b, t, cw, True)
    if h0_re is None:
        h0 = jnp.zeros((g, b, 4 * C_STATE), F32)
    else:
        h0 = jnp.concatenate([h0_re[:, 0], h0_re[:, 1], h0_im[:, 0], h0_im[:, 1]], axis=-1).transpose(1, 0, 2)
    gs = S5_GROUPS

    def grp(shape):
        nd = len(shape)
        return pl.BlockSpec((gs,) + shape[1:], lambda i: (i,) + (0,) * (nd - 1))

    y, fin = pl.pallas_call(
        functools.partial(_s5_kernel, nb=b),
        grid=(g // gs,),
        in_specs=[grp(ug.shape), grp(m.shape), grp(w.shape), grp(v.shape), grp(a.shape), grp(h0.shape)],
        out_specs=[grp(ug.shape), grp(h0.shape)],
        out_shape=[jax.ShapeDtypeStruct(ug.shape, F32), jax.ShapeDtypeStruct(h0.shape, F32)],
        scratch_shapes=[pltpu.VMEM((rows, width), F32)] * 3,
        compiler_params=_params("parallel"),
        name="s5_scan",
    )(ug, m, w, v, a, h0)
    y = _s5_relayout(y, b, t, cw, False)
    fin = fin.reshape(g, b, 2, 2, C_STATE).transpose(2, 1, 3, 0, 4)
    return y, fin[0], fin[1]


def _s5_params(lam_re, lam_im, log_dt, b_re, b_im, c_re, c_im):
    dt = jnp.exp(log_dt)[..., None]
    mag = jnp.exp(lam_re * dt)
    a_re = mag * jnp.cos(lam_im * dt)
    a_im = mag * jnp.sin(lam_im * dt)
    den = lam_re * lam_re + lam_im * lam_im
    f_re = ((a_re - 1.0) * lam_re + a_im * lam_im) / den
    f_im = (a_im * lam_re - (a_re - 1.0) * lam_im) / den
    fb_re = f_re[..., None] * b_re[None] - f_im[..., None] * b_im[None]
    fb_im = f_re[..., None] * b_im[None] + f_im[..., None] * b_re[None]
    g = lam_re.shape[1]
    L = S5_CHUNK
    k = jnp.arange(L + 1, dtype=F32)[:, None, None, None]
    pmag = jnp.exp(lam_re * dt * k)
    p_re = pmag * jnp.cos(lam_im * dt * k)
    p_im = pmag * jnp.sin(lam_im * dt * k)
    fbt_re = fb_re.transpose(0, 1, 3, 2)
    fbt_im = fb_im.transpose(0, 1, 3, 2)

    ca_re = c_re[None, None] * p_re[:, :, :, None, :] - c_im[None, None] * p_im[:, :, :, None, :]
    ca_im = c_re[None, None] * p_im[:, :, :, None, :] + c_im[None, None] * p_re[:, :, :, None, :]
    car = ca_re.transpose(1, 2, 0, 3, 4)[:, :, None]
    cai = ca_im.transpose(1, 2, 0, 3, 4)[:, :, None]
    kern = jnp.sum(car * fbt_re[:, :, :, None, None, :] - cai * fbt_im[:, :, :, None, None, :], axis=-1)
    wid = L * C_GROUP
    fwd = kern[0, :, :, :L].reshape(g, C_GROUP, wid)
    bwd = kern[1, :, :, L - 1::-1][:, :, :L].reshape(g, C_GROUP, wid)
    rows = []
    for s in range(L):
        f = jnp.pad(fwd[:, :, :wid - s * C_GROUP], ((0, 0), (0, 0), (s * C_GROUP, 0)))
        bk = jnp.pad(bwd[:, :, (L - 1 - s) * C_GROUP:], ((0, 0), (0, 0), (0, (L - 1 - s) * C_GROUP)))
        rows.append(f + bk)
    m = jnp.stack(rows, axis=1).reshape(g, wid, wid)

    def to_state(pw_re, pw_im, d):
        re = pw_re[:, :, None, :] * fbt_re[d][None] - pw_im[:, :, None, :] * fbt_im[d][None]
        im = pw_re[:, :, None, :] * fbt_im[d][None] + pw_im[:, :, None, :] * fbt_re[d][None]
        return re.transpose(1, 0, 2, 3), im.transpose(1, 0, 2, 3)

    wf_re, wf_im = to_state(p_re[L - 1::-1, 0][:L], p_im[L - 1::-1, 0][:L], 0)
    wb_re, wb_im = to_state(p_re[:L, 1], p_im[:L, 1], 1)
    w = jnp.concatenate([wf_re, wb_re, wf_im, wb_im], axis=-1).reshape(g, wid, 4 * C_STATE)

    def from_state(car, cai):
        return car.transpose(1, 3, 0, 2), -cai.transpose(1, 3, 0, 2)

    vf_re, vf_im = from_state(ca_re[1:, 0], ca_im[1:, 0])
    vb_re, vb_im = from_state(ca_re[L:0:-1, 1], ca_im[L:0:-1, 1])
    zero = jnp.zeros_like(vf_re)
    v = jnp.stack([jnp.concatenate([vf_re, zero, vf_im, zero], axis=1),
                   jnp.concatenate([zero, vb_re, zero, vb_im], axis=1)], axis=1)
    v = v.reshape(g, 2, 4 * C_STATE, L * C_GROUP)

    a = jnp.stack([jnp.concatenate([p_re[L, 0], p_re[L, 1]], axis=-1),
                   jnp.concatenate([p_im[L, 0], p_im[L, 1]], axis=-1)], axis=1)
    return m.astype(BF16), w.astype(BF16), v.astype(BF16), a


def _diff_attn_kernel(*refs, latent, kc):
    if latent:
        bnd_ref, lam_ref, q_ref, k_ref, v_ref, ck_ref, cv_ref, o_ref, cn_ref = refs
    else:
        bnd_ref, lam_ref, q_ref, k_ref, v_ref, o_ref = refs
    lq = q_ref.shape[1]
    lo = _lane_lo()
    q = q_ref[0]
    qs = (jnp.where(lo, q, 0.0).astype(BF16), jnp.where(lo, 0.0, q).astype(BF16))

    def over_keys(step, carry):
        if latent:
            carry = step(ck_ref[0], cv_ref[0], carry)

            def body(c, carry):
                rows = pl.ds(pl.multiple_of(c * kc, kc), kc)
                return step(k_ref[0, rows, :], v_ref[0, rows, :], carry)

            return lax.fori_loop(0, k_ref.shape[1] // kc, body, carry)
        return step(k_ref[0], v_ref[0], carry)

    def finish(l0, acc0, l1, acc1):
        o_ref[0] = acc0 / l0 - lam_ref[0] * (acc1 / l1)

    def online_step(kb, vb, carry):
        kb = kb.astype(BF16)
        vb = vb.astype(BF16)
        out = []
        for qc, (m, l, acc) in zip(qs, carry):
            s = _mm_nt(qc, kb)
            m_new = jnp.maximum(m, jnp.max(s, axis=1, keepdims=True))
            alpha = jnp.exp2(m - m_new)
            p = jnp.exp2(s - m_new)
            l = alpha * l + jnp.sum(p, axis=1, keepdims=True)
            acc = alpha * acc + _mm(p, vb)
            out.append((m_new, l, acc))
        return tuple(out)

    def online():
        init = (jnp.full((lq, 1), NEG_INF, F32), jnp.zeros((lq, 1), F32), jnp.zeros((lq, LANES), F32))
        (_, l0, acc0), (_, l1, acc1) = over_keys(online_step, (init, init))
        finish(l0, acc0, l1, acc1)

    if latent:
        @pl.when(pl.program_id(2) == 0)
        def _():
            cn_ref[0] = _max_sq_head_norm(ck_ref[0])

        bound_sq = bnd_ref[0] * jnp.maximum(bnd_ref[1], cn_ref[0])
    else:
        bound_sq = bnd_ref[0] * bnd_ref[1]
    fixed = bound_sq <= SHIFT_LIMIT * SHIFT_LIMIT

    @pl.when(fixed)
    def _():
        shift = jnp.sqrt(jnp.full((1, 1), bound_sq, F32))

        def step(kb, vb, carry):
            kb = kb.astype(BF16)
            vb = vb.astype(BF16)
            out = []
            for qc, (l, acc) in zip(qs, carry):
                p = jnp.exp2(_mm_nt(qc, kb) - shift)
                out.append((l + jnp.sum(p, axis=1, keepdims=True), acc + _mm(p, vb)))
            return tuple(out)

        init = (jnp.zeros((lq, 1), F32), jnp.zeros((lq, LANES), F32))
        (l0, acc0), (l1, acc1) = over_keys(step, (init, init))
        finish(l0, acc0, l1, acc1)

    @pl.when(jnp.logical_not(fixed))
    def _():
        online()


def _diff_attn(qkv, lam, cache=None, bounds=None, lq=512, kc=4096):
    b, t, _ = qkv.shape
    nh = 4
    latent = cache is not None
    kc = min(kc, t)
    assert t % kc == 0 and t % lq == 0
    in_specs = [pl.BlockSpec(memory_space=pltpu.SMEM),
                pl.BlockSpec(memory_space=pltpu.SMEM),
                pl.BlockSpec((1, lq, LANES), lambda i, h, j: (i, j, h)),
                pl.BlockSpec((1, t, LANES), lambda i, h, j: (i, 0, nh + h)),
                pl.BlockSpec((1, t, LANES), lambda i, h, j: (i, 0, 2 * nh + h))]
    args = [bounds, lam, qkv, qkv, qkv]
    if latent:
        tc = cache[0].shape[1]
        in_specs += [pl.BlockSpec((1, tc, LANES), lambda i, h, j: (i, 0, h))] * 2
        args += list(cache)
    return pl.pallas_call(
        functools.partial(_diff_attn_kernel, latent=latent, kc=kc),
        grid=(b, nh, t // lq),
        in_specs=in_specs,
        out_specs=pl.BlockSpec((1, lq, LANES), lambda i, h, j: (i, j, h)),
        out_shape=jax.ShapeDtypeStruct((b, t, nh * LANES), F32),
        scratch_shapes=[pltpu.SMEM((1,), F32)] if latent else [],
        compiler_params=_params("parallel", "parallel", "arbitrary"),
        name="diff_attn_lat" if latent else "diff_attn_ctx",
    )(*args)


def _resid_ffn(x, mix, g1, n2g, sh2, sc2, g2, w1_ref, w3_ref, w2_ref):
    x1 = x + g1 * mix
    h = (_rms(x1) * n2g * (1.0 + sc2) + sh2).astype(BF16)
    tiles = w1_ref.shape[1] // MXU_DIM
    cuts = [MXU_DIM * (tiles * s // FFN_SPLIT) for s in range(FFN_SPLIT + 1)]
    f = None
    for c0, c1 in zip(cuts[:-1], cuts[1:]):
        a = jnp.dot(h, w1_ref[:, c0:c1], preferred_element_type=F32)
        b = jnp.dot(h, w3_ref[:, c0:c1], preferred_element_type=F32)
        part = _mm(_silu(a) * b, w2_ref[c0:c1, :])
        f = part if f is None else f + part
    return x1 + g2 * f


def _out_ab_kernel(x_ref, oa_ref, ob_ref, gb_ref, rg_ref, wo_ref, g1_ref, n2g_ref, sh2_ref, sc2_ref,
                   g2_ref, w1_ref, w3_ref, w2_ref, y_ref):
    half = wo_ref.shape[0] // 2
    ob = ob_ref[0]
    inv = jnp.concatenate([_head_rsqrt(ob[:, c0:c0 + MXU_DIM]) for c0 in range(0, half, MXU_DIM)], axis=1)
    ob = ob * inv * rg_ref[...] * _silu(gb_ref[0])
    mix = _mm(oa_ref[0], wo_ref[:half, :]) + _mm(ob, wo_ref[half:, :])
    y_ref[0] = _resid_ffn(x_ref[0], mix, g1_ref[0], n2g_ref[...], sh2_ref[0], sc2_ref[0], g2_ref[0],
                          w1_ref, w3_ref, w2_ref)


def _out_cd_kernel(x_ref, ys_ref, u_ref, od_ref, dsk_ref, gw_ref, gbias_ref, sg_ref, wo_ref,
                   g1_ref, n2g_ref, sh2_ref, sc2_ref, g2_ref, w1_ref, w3_ref, w2_ref, y_ref, *, od_scale):
    half = wo_ref.shape[0] // 2
    tiles = range(ys_ref.shape[1])
    y = (jnp.concatenate([ys_ref[0, j] for j in tiles], axis=1)
         + dsk_ref[...] * jnp.concatenate([u_ref[0, j] for j in tiles], axis=1))
    g = jax.nn.gelu(y)
    oc = g * jax.nn.sigmoid(_mm(g, gw_ref[...]) + gbias_ref[...])
    od = jnp.concatenate([_rms(od_ref[0, :, c0:c0 + LANES]) * sg_ref[...] * od_scale
                          for c0 in range(0, half, LANES)], axis=1)
    mix = _mm(oc, wo_ref[:half, :]) + _mm(od, wo_ref[half:, :])
    y_ref[0] = _resid_ffn(x_ref[0], mix, g1_ref[0], n2g_ref[...], sh2_ref[0], sc2_ref[0], g2_ref[0],
                          w1_ref, w3_ref, w2_ref)


def _const_spec(shape):
    nd = len(shape)
    return pl.BlockSpec(shape, lambda i, j: (0,) * nd, pipeline_mode=pl.Buffered(1))


def _out_call(kern, name, x, tok_args, const_args, mod, ffn, tm):
    b, t, d = x.shape
    g1, sh2, sc2, g2 = mod
    n2g, w1, w3, w2 = ffn

    def tok(a):
        if a.ndim == 4:
            return pl.BlockSpec((1, a.shape[1], tm, LANES), lambda i, j: (i, 0, j, 0))
        return pl.BlockSpec((1, tm, a.shape[2]), lambda i, j: (i, j, 0))

    def per_batch(a):
        return pl.BlockSpec((1, 1, d), lambda i, j: (i, 0, 0))

    args = [x] + list(tok_args) + list(const_args) + [g1, n2g, sh2, sc2, g2, w1, w3, w2]
    in_specs = ([tok(x)] + [tok(a) for a in tok_args] + [_const_spec(a.shape) for a in const_args]
                + [per_batch(g1), _const_spec(n2g.shape), per_batch(sh2), per_batch(sc2), per_batch(g2),
                   _const_spec(w1.shape), _const_spec(w3.shape), _const_spec(w2.shape)])
    return pl.pallas_call(
        kern,
        grid=(b, t // tm),
        in_specs=in_specs,
        out_specs=tok(x),
        out_shape=jax.ShapeDtypeStruct(x.shape, F32),
        compiler_params=_params("parallel", "parallel"),
        name=name,
    )(*args)


def _ab_plan():
    plan = [(NORM_Q, 0, j) for j in range(4)] + [(NORM_K, 0, 4), (NORM_K, 0, 5), (RAW, 0, 6), (RAW, 0, 7)]
    plan += [(SCALE_Q, 1, j) for j in range(4)] + [(RAW, 1, 4 + j) for j in range(8)]
    plan += [(RAW, 2, j) for j in range(4)]
    return tuple(plan)


def _dup_kv_cols(w):
    q, kv, rest = w[:, :4 * LANES], w[:, 4 * LANES:6 * LANES], w[:, 6 * LANES:]
    kv = jnp.repeat(kv.reshape(w.shape[0], 4, 1, HEAD_DIM), 2, axis=2).reshape(w.shape[0], 4 * LANES)
    return jnp.concatenate([q, kv, rest], axis=1)


def _dup_heads(x):
    return jnp.repeat(x[..., :, None, :], 2, axis=-2).reshape(x.shape[:-2] + (4 * HEAD_DIM,))


def _cd_plan():
    plan = [(RAW, 0, j) for j in range(4)]
    plan += [(NORM_Q, 1, j) for j in range(4)] + [(NORM_K, 1, 4 + j) for j in range(4)]
    plan += [(RAW, 1, 8 + j) for j in range(4)]
    return tuple(plan)


AB_PLAN = _ab_plan()
CD_PLAN = _cd_plan()


def _rope_tables(t):
    rows = t // GRID_W
    r = jnp.repeat(jnp.arange(rows, dtype=F32), GRID_W)
    col = jnp.tile(jnp.arange(GRID_W, dtype=F32), rows)
    n_freq = HEAD_DIM // 4
    inv = ROPE_BASE ** (-jnp.arange(n_freq, dtype=F32) / n_freq)
    ang = jnp.concatenate([r[:, None] * inv, col[:, None] * inv], axis=-1)
    cos, sin = jnp.cos(ang), jnp.sin(ang)
    cos128 = jnp.tile(cos, (1, 4))
    sin128 = jnp.tile(jnp.concatenate([-sin, sin], axis=-1), (1, 2))
    return cos128, sin128


def _tile2(g):
    return jnp.tile(g.astype(F32), 2).reshape(1, LANES)


def kernel(x_prompt, x_sample, c, c_ctx, cache_k_a, cache_v_a, state_ret, state_ssm_re, state_ssm_im,
           cache_k_d, cache_v_d, ada_w, ada_b, norm1_g, norm2_g, ffn_w1, ffn_w3, ffn_w2,
           ab_w_in, ab_w_out, a_q_norm, a_k_norm, a_sink, ret_decay, ret_norm,
           cd_w_in, cd_w_out, ssm_lambda_re, ssm_lambda_im, ssm_log_dt, ssm_b_re, ssm_b_im,
           ssm_c_re, ssm_c_im, ssm_d, ssm_glu_w, ssm_glu_b, d_q_norm, d_k_norm, d_lambda, d_subln):
    nbc, tcx, d = x_prompt.shape
    nbl, tl, _ = x_sample.shape
    depth = ada_w.shape[0]
    rope_tabs = _rope_tables(tl)

    ncond = -(-(nbl + 1) // SUBLANES) * SUBLANES
    cond = jnp.zeros((ncond, d), F32).at[:nbl].set(c).at[nbl].set(c_ctx)
    mods = _ada(cond, ada_w, ada_b)

    def mod_split(l, lo, hi):
        m = mods[l, lo:hi].reshape(hi - lo, 1, 6, d)
        return [m[:, :, k] for k in range(6)]

    yp = x_prompt.reshape(1, nbc * tcx, d)
    ys = x_sample
    new_ka, new_va, new_ret, new_sr, new_si, new_kd, new_vd = [], [], [], [], [], [], []
    for l in range(depth):
        j = l // 2
        ffn = (norm2_g[l].reshape(1, d), ffn_w1[l].astype(BF16), ffn_w3[l].astype(BF16), ffn_w2[l].astype(BF16))
        n1g = norm1_g[l].reshape(1, d)
        sh1c, sc1c, g1c, sh2c, sc2c, g2c = mod_split(l, nbl, nbl + 1)
        sh1l, sc1l, g1l, sh2l, sc2l, g2l = mod_split(l, 0, nbl)
        if l % 2 == 0:
            w_in = _dup_kv_cols(ab_w_in[j]).astype(BF16)
            w_out = ab_w_out[j].astype(BF16)
            gq, gk = _tile2(a_q_norm[j]), _tile2(a_k_norm[j])
            rg = jnp.tile(ret_norm[j].astype(F32), ab_w_out.shape[1] // 2 // HEAD_DIM).reshape(1, -1)
            log_g = jax.nn.log_sigmoid(ret_decay[j].astype(F32))
            widths = (8 * LANES, 12 * LANES, 4 * LANES)
            a_scale = Q_SCALE
            qkv_a, qkv_b, gb = _inproj(yp, sh1c, sc1c, n1g, w_in, gq, gk, None, AB_PLAN, widths,
                                       (F32, F32, F32), a_scale, TOKEN_TILE, "inproj_ab_ctx")
            qkv_a = qkv_a.reshape(nbc, tcx, -1)
            bounds = _norm_bounds(a_q_norm[j], a_k_norm[j])
            oa = _attn_a_ctx(qkv_a, a_sink[j], bounds)
            ob, s_ret = _retention(qkv_b.reshape(nbc, tcx, -1), log_g, None)
            yp = _out_call(_out_ab_kernel, "out_ab_ctx", yp,
                           [oa.reshape(1, nbc * tcx, -1), ob.reshape(1, nbc * tcx, -1), gb],
                           [rg, w_out], (g1c, sh2c, sc2c, g2c), ffn, TOKEN_TILE)
            new_ka.append(qkv_a[:, :, 4 * LANES:6 * LANES].reshape(nbc, tcx, 2, 2, HEAD_DIM)[:, :, :, 0])
            new_va.append(qkv_a[:, :, 6 * LANES:8 * LANES].reshape(nbc, tcx, 2, 2, HEAD_DIM)[:, :, :, 0])
            s_ret = jnp.stack([s_ret[..., :HEAD_DIM, :HEAD_DIM], s_ret[..., HEAD_DIM:, HEAD_DIM:]], axis=2)
            new_ret.append(s_ret.transpose(0, 3, 1, 2, 4, 5).reshape(nbc, 2, 8, HEAD_DIM, HEAD_DIM))
            qkv_a, qkv_b, gb = _inproj(ys, sh1l, sc1l, n1g, w_in, gq, gk, rope_tabs, AB_PLAN, widths,
                                       (BF16, F32, F32), a_scale, TOKEN_TILE, "inproj_ab_lat")
            ck = _dup_heads(cache_k_a[:, j]).astype(BF16)
            cv = _dup_heads(cache_v_a[:, j]).astype(BF16)
            oa = _attn_a_lat(qkv_a, ck, cv, a_sink[j], bounds)
            s0 = state_ret[:, j].reshape(nbl, 2, 4, 2, HEAD_DIM, HEAD_DIM)
            z = jnp.zeros_like(s0[:, :, :, 0])
            s0 = jnp.concatenate([jnp.concatenate([s0[:, :, :, 0], z], axis=-1),
                                  jnp.concatenate([z, s0[:, :, :, 1]], axis=-1)], axis=-2)
            s0 = s0.transpose(0, 2, 1, 3, 4)
            ob, _ = _retention(qkv_b, log_g, s0)
            ys = _out_call(_out_ab_kernel, "out_ab_lat", ys, [oa, ob, gb], [rg, w_out],
                           (g1l, sh2l, sc2l, g2l), ffn, TOKEN_TILE)
        else:
            lam_init = 0.8 - 0.6 * math.exp(-0.3 * l)
            w_in = cd_w_in[j].astype(BF16)
            w_out = cd_w_out[j].astype(BF16)
            gq, gk = _tile2(d_q_norm[j]), _tile2(d_k_norm[j])
            lp = d_lambda[j].astype(F32)
            lam = (jnp.exp(jnp.sum(lp[0] * lp[1])) - jnp.exp(jnp.sum(lp[2] * lp[3])) + lam_init).reshape(1)
            prm = _s5_params(ssm_lambda_re[j], ssm_lambda_im[j], ssm_log_dt[j], ssm_b_re[j], ssm_b_im[j],
                             ssm_c_re[j], ssm_c_im[j])
            consts =[ssm_d[j].reshape(1, -1), ssm_glu_w[j].astype(BF16), ssm_glu_b[j].reshape(1, -1),
                      d_subln[j].reshape(1, LANES), w_out]
            kern = functools.partial(_out_cd_kernel, od_scale=1.0 - lam_init)
            widths = (-4 * LANES, 12 * LANES)
            d_scale = Q_SCALE
            u, qkv_d = _inproj(yp, sh1c, sc1c, n1g, w_in, gq, gk, None, CD_PLAN, widths,
                               (F32, F32), d_scale, TOKEN_TILE, "inproj_cd_ctx")
            u_b = u.reshape(-1, nbc, tcx, LANES).transpose(1, 0, 2, 3)
            y, fin_re, fin_im = _s5(u_b, prm, None, None)
            y = y.transpose(1, 0, 2, 3).reshape(u.shape)
            qkv_d = qkv_d.reshape(nbc, tcx, -1)
            d_bounds = _norm_bounds(d_q_norm[j], d_k_norm[j])
            od = _diff_attn(qkv_d, lam, None, d_bounds, lq=tcx)
            yp = _out_call(kern, "out_cd_ctx", yp, [y, u, od.reshape(1, nbc * tcx, -1)],
                           consts, (g1c, sh2c, sc2c, g2c), ffn, TOKEN_TILE)
            new_sr.append(fin_re)
            new_si.append(fin_im)
            new_kd.append(qkv_d[:, :, 4 * LANES:8 * LANES].reshape(nbc, tcx, 4, 2, HEAD_DIM))
            new_vd.append(qkv_d[:, :, 8 * LANES:].reshape(nbc, tcx, 4, 2 * HEAD_DIM))
            u, qkv_d = _inproj(ys, sh1l, sc1l, n1g, w_in, gq, gk, rope_tabs, CD_PLAN, widths,
                               (F32, BF16), d_scale, TOKEN_TILE, "inproj_cd_lat")
            y, _, _ = _s5(u, prm, state_ssm_re[:, j], state_ssm_im[:, j])
            ckd = cache_k_d[:, j].reshape(nbl, -1, 4 * LANES)
            cvd = cache_v_d[:, j].reshape(nbl, -1, 4 * LANES)
            od = _diff_attn(qkv_d, lam, (ckd, cvd), d_bounds)
            ys = _out_call(kern, "out_cd_lat", ys, [y, u, od], consts, (g1l, sh2l, sc2l, g2l), ffn, TOKEN_TILE)
    return (yp.reshape(nbc, tcx, d), ys,
            jnp.stack(new_ka, axis=1), jnp.stack(new_va, axis=1), jnp.stack(new_ret, axis=1),
            jnp.stack(new_sr, axis=1), jnp.stack(new_si, axis=1),
            jnp.stack(new_kd, axis=1), jnp.stack(new_vd, axis=1))
```
